```python
import jax, jax.numpy as jnp
from jax import lax
import numpy as np

D_MODEL = 1024
BATCH = 4
SEQ = 4096
DEPTH = 2

GRID_W = 64
CTX_LEN = 256
HEAD_DIM_A = 128
HEADS_A = D_MODEL // (2 * HEAD_DIM_A)
KV_HEADS_A = HEADS_A // 2
GROUP_A = HEADS_A // KV_HEADS_A
HEAD_DIM_B = 64
HEADS_B = D_MODEL // (2 * HEAD_DIM_B)
NA_MAX_ROWS = 8
NA_WIN_COLS = 16
Q_BLOCK = 128
FOURIER_GROUPS = 4
D_FF = 2816
CONV_W = 3
ROPE_THETA = 10000.0
ROPE_PAIRS_AXIS = HEAD_DIM_A // 4
EPS = 1e-6
QA_COLS = HEADS_A * HEAD_DIM_A
KA_COLS = KV_HEADS_A * HEAD_DIM_A
B_COLS = HEADS_B * HEAD_DIM_B
Q_COLS = QA_COLS + B_COLS
IN_COLS = Q_COLS + 2 * KA_COLS + 2 * B_COLS
MIX_WIDTH = QA_COLS + B_COLS

kernel_name = "hybrid_gqa_natten_fnet_convffn_dit"


def rmsnorm(x, g):
    xf = x.astype(jnp.float32)
    y = xf * lax.rsqrt(jnp.mean(xf * xf, axis=-1, keepdims=True) + EPS)
    return (y * g.astype(jnp.float32)).astype(x.dtype)


def adaln(vec, w, b):
    m = jax.nn.silu(vec) @ w + b
    return jnp.split(m[..., None, :], 6, axis=-1)


def modulate(h, shift, scale):
    return h * (1.0 + scale) + shift


def heads(t, n):
    return t.reshape(*t.shape[:-1], n, -1)


def axial_rope(length):
    t = jnp.arange(length)
    row = (t // GRID_W).astype(jnp.float32)
    col = (t % GRID_W).astype(jnp.float32)
    freqs = ROPE_THETA ** (-jnp.arange(ROPE_PAIRS_AXIS, dtype=jnp.float32) / ROPE_PAIRS_AXIS)
    ang = jnp.concatenate([row[:, None] * freqs, col[:, None] * freqs], axis=-1)
    return jnp.cos(ang)[:, None, :], jnp.sin(ang)[:, None, :]


def apply_rope(x, cos, sin):
    xf = x.astype(jnp.float32).reshape(*x.shape[:-1], -1, 2)
    x0, x1 = xf[..., 0], xf[..., 1]
    out = jnp.stack([x0 * cos - x1 * sin, x0 * sin + x1 * cos], axis=-1)
    return out.reshape(x.shape).astype(x.dtype)


def attend(q, k, v):
    s = jnp.einsum('bqkgd,bskd->bkgqs', q, k).astype(jnp.float32)
    p = jax.nn.softmax(s, axis=-1).astype(v.dtype)
    o = jnp.einsum('bkgqs,bskd->bqkgd', p, v)
    return o.reshape(*o.shape[:2], -1)


def gqa_blocks(q, k, v):
    bn, length = q.shape[:2]
    nb = length // Q_BLOCK
    qb = q.reshape(bn, nb, Q_BLOCK, KV_HEADS_A, GROUP_A, HEAD_DIM_A).transpose(1, 0, 2, 3, 4, 5)
    o = lax.map(lambda qi: attend(qi, k, v), qb)
    return o.transpose(1, 0, 2, 3).reshape(bn, length, -1)


def neighbourhood_attention(q, k, v, ck, cv, rpb):
    bn, length, nh, hd = q.shape
    rows = length // GRID_W
    wr = min(NA_MAX_ROWS, rows)
    grid = lambda t: t.reshape(bn, rows, GRID_W, nh, hd)
    kg, vg = grid(k), grid(v)
    r = jnp.arange(rows)
    col = jnp.arange(GRID_W)
    row_start = jnp.clip(r - wr // 2, 0, rows - wr)
    col_idx = (jnp.clip(col - NA_WIN_COLS // 2, 0, GRID_W - NA_WIN_COLS)[:, None]
               + jnp.arange(NA_WIN_COLS))
    row_off = row_start[:, None] + jnp.arange(wr) - r[:, None] + NA_MAX_ROWS - 1
    col_off = col_idx - col[:, None] + NA_WIN_COLS - 1
    n_win = wr * NA_WIN_COLS

    def row_block(args):
        q_r, r0, off_r = args
        kw = lax.dynamic_slice_in_dim(kg, r0, wr, axis=1)[:, :, col_idx]
        vw = lax.dynamic_slice_in_dim(vg, r0, wr, axis=1)[:, :, col_idx]
        bias = rpb[:, off_r][:, :, col_off].transpose(0, 2, 1, 3)
        s_win = jnp.einsum('bchd,bicjhd->bhcij', q_r, kw).astype(jnp.float32) + bias.astype(jnp.float32)
        s_ctx = jnp.einsum('bchd,bshd->bhcs', q_r, ck).astype(jnp.float32)
        s = jnp.concatenate([s_win.reshape(bn, nh, GRID_W, n_win), s_ctx], axis=-1)
        p = jax.nn.softmax(s, axis=-1).astype(v.dtype)
        p_win = p[..., :n_win].reshape(bn, nh, GRID_W, wr, NA_WIN_COLS)
        o = (jnp.einsum('bhcij,bicjhd->bchd', p_win, vw)
             + jnp.einsum('bhcs,bshd->bchd', p[..., n_win:], cv))
        return o.reshape(bn, GRID_W, nh * hd)

    o = lax.map(row_block, (grid(q).transpose(1, 0, 2, 3, 4), row_start, row_off))
    return o.transpose(1, 0, 2, 3).reshape(bn, length, nh * hd)


def attention_mixer(h, hc, w_in, w_out, q_g, k_g, rpb, ctx_out):
    bn, length, _ = h.shape
    kv_splits = [KA_COLS, 2 * KA_COLS, 2 * KA_COLS + B_COLS]
    qa, qb, ka, va, kb, vb = jnp.split(h @ w_in, [QA_COLS, Q_COLS] + [Q_COLS + s for s in kv_splits], axis=-1)
    cka, cva, ckb, cvb = jnp.split(hc @ w_in[:, Q_COLS:], kv_splits, axis=-1)
    scale_a = HEAD_DIM_A ** -0.5
    scale_b = HEAD_DIM_B ** -0.5
    cos, sin = axial_rope(length)
    qa = apply_rope(rmsnorm(heads(qa, HEADS_A), q_g), cos, sin) * scale_a
    ka = apply_rope(rmsnorm(heads(ka, KV_HEADS_A), k_g), cos, sin)
    cka = rmsnorm(heads(cka, KV_HEADS_A), k_g)
    cva = heads(cva, KV_HEADS_A)
    oa = gqa_blocks(qa, jnp.concatenate([ka, cka], axis=1),
                    jnp.concatenate([heads(va, KV_HEADS_A), cva], axis=1))
    ckb = heads(ckb, HEADS_B)
    cvb = heads(cvb, HEADS_B)
    ob = neighbourhood_attention(heads(qb, HEADS_B) * scale_b, heads(kb, HEADS_B), heads(vb, HEADS_B),
                                 ckb, cvb, rpb)
    y = jnp.concatenate([oa, ob], axis=-1) @ w_out
    if not ctx_out:
        return y, None
    cq = hc @ w_in[:, :Q_COLS]
    cqa, cqb = jnp.split(cq, [QA_COLS], axis=-1)
    cqa = (rmsnorm(heads(cqa, HEADS_A), q_g) * scale_a).reshape(*hc.shape[:2], KV_HEADS_A, GROUP_A, HEAD_DIM_A)
    coa = attend(cqa, cka, cva)
    cqb = (heads(cqb, HEADS_B) * scale_b)[:, :, :, None, :]
    cob = attend(cqb.reshape(*hc.shape[:2], HEADS_B, 1, HEAD_DIM_B), ckb, cvb)
    yc = jnp.concatenate([coa, cob], axis=-1) @ w_out
    return y, yc


def fourier_mixer(h, w_out):
    bn, length, d = h.shape
    hg = h.astype(jnp.float32).reshape(bn, length, FOURIER_GROUPS, d // FOURIER_GROUPS)
    y = jnp.fft.fft2(hg, axes=(1, 3), norm='ortho').real
    return y.reshape(bn, length, d).astype(h.dtype) @ w_out


def conv_ffn(h, w_up, conv_w, conv_b, w_down):
    u = h @ w_up
    up = jnp.pad(u, ((0, 0), (1, 1), (0, 0)))
    u = up[:, :-2] * conv_w[0] + up[:, 1:-1] * conv_w[1] + up[:, 2:] * conv_w[2] + conv_b
    g, val = jnp.split(u, 2, axis=-1)
    return (jax.nn.silu(g) * val) @ w_down


def setup_inputs(seed: int = 0) -> dict:
    key = jax.random.key(seed)
    ks = jax.random.split(key, 20)
    n_even = (DEPTH + 1) // 2
    n_odd = DEPTH // 2
    nrm = lambda k, shape, s: jax.random.normal(k, shape, jnp.float32) * s
    return {
        "x": nrm(ks[0], (BATCH, SEQ, D_MODEL), 1.0),
        "c": nrm(ks[1], (BATCH, D_MODEL), 1.0),
        "ctx": nrm(ks[2], (BATCH, CTX_LEN, D_MODEL), 1.0),
        "c_ctx": nrm(ks[3], (D_MODEL,), 1.0),
        "mod_w": nrm(ks[4], (DEPTH, D_MODEL, 6 * D_MODEL), 0.5 * D_MODEL ** -0.5),
        "mod_b": nrm(ks[5], (DEPTH, 6 * D_MODEL), 0.02),
        "norm1_g": 1.0 + nrm(ks[6], (DEPTH, D_MODEL), 0.02),
        "norm2_g": 1.0 + nrm(ks[7], (DEPTH, D_MODEL), 0.02),
        "attn_w_in": nrm(ks[8], (n_even, D_MODEL, IN_COLS), D_MODEL ** -0.5),
        "attn_w_out": nrm(ks[9], (n_even, MIX_WIDTH, D_MODEL), MIX_WIDTH ** -0.5),
        "q_norm_g": 1.0 + nrm(ks[10], (n_even, HEAD_DIM_A), 0.02),
        "k_norm_g": 1.0 + nrm(ks[11], (n_even, HEAD_DIM_A), 0.02),
        "na_rpb": nrm(ks[12], (n_even, HEADS_B, 2 * NA_MAX_ROWS - 1, 2 * NA_WIN_COLS - 1), 0.1),
        "fourier_w_out": nrm(ks[13], (n_odd, D_MODEL, D_MODEL), D_MODEL ** -0.5),
        "ffn_w_up": nrm(ks[14], (DEPTH, D_MODEL, 2 * D_FF), D_MODEL ** -0.5),
        "ffn_conv_w": nrm(ks[15], (DEPTH, CONV_W, 2 * D_FF), CONV_W ** -0.5),
        "ffn_conv_b": nrm(ks[16], (DEPTH, 2 * D_FF), 0.02),
        "ffn_w_down": nrm(ks[17], (DEPTH, D_FF, D_MODEL), D_FF ** -0.5),
        "final_g": 1.0 + nrm(ks[18], (D_MODEL,), 0.02),
    }


def reference(x, c, ctx, c_ctx, mod_w, mod_b, norm1_g, norm2_g, attn_w_in, attn_w_out, q_norm_g,
              k_norm_g, na_rpb, fourier_w_out, ffn_w_up, ffn_conv_w, ffn_conv_b, ffn_w_down, final_g):
    for i in range(DEPTH):
        ctx_live = any(j % 2 == 0 for j in range(i + 1, DEPTH))
        sh1, sc1, g1, sh2, sc2, g2 = adaln(c, mod_w[i], mod_b[i])
        h = modulate(rmsnorm(x, norm1_g[i]), sh1, sc1)
        need_hc = (i % 2 == 0) or ctx_live
        if need_hc:
            csh1, csc1, cg1, csh2, csc2, cg2 = adaln(c_ctx, mod_w[i], mod_b[i])
            hc = modulate(rmsnorm(ctx, norm1_g[i]), csh1, csc1)
        if i % 2 == 0:
            e = i // 2
            y, yc = attention_mixer(h, hc, attn_w_in[e], attn_w_out[e], q_norm_g[e], k_norm_g[e],
                                    na_rpb[e], ctx_live)
        else:
            y = fourier_mixer(h, fourier_w_out[i // 2])
            yc = fourier_mixer(hc, fourier_w_out[i // 2]) if ctx_live else None
        x = x + g1 * y
        x = x + g2 * conv_ffn(modulate(rmsnorm(x, norm2_g[i]), sh2, sc2),
                              ffn_w_up[i], ffn_conv_w[i], ffn_conv_b[i], ffn_w_down[i])
        if ctx_live:
            ctx = ctx + cg1 * yc
            ctx = ctx + cg2 * conv_ffn(modulate(rmsnorm(ctx, norm2_g[i]), csh2, csc2),
                                       ffn_w_up[i], ffn_conv_w[i], ffn_conv_b[i], ffn_w_down[i])
    return rmsnorm(x, final_g)
```

```python
import functools

import numpy as np
import jax
import jax.numpy as jnp
from jax import lax
from jax.experimental import pallas as pl
from jax.experimental.pallas import tpu as pltpu

F32 = jnp.float32
BF16 = jnp.bfloat16

GRID_W = 64
HEAD_DIM_A = 128
HEADS_A = 4
KV_HEADS_A = 2
HEAD_DIM_B = 64
HEADS_B = 8
NA_ROWS = 8
NA_COLS = 16
FOURIER_GROUPS = 4
ROPE_THETA = 10000.0
EPS = 1e-6
QA_COLS = HEADS_A * HEAD_DIM_A
KA_COLS = KV_HEADS_A * HEAD_DIM_A
B_COLS = HEADS_B * HEAD_DIM_B
NEG = -1e30

VMEM_LIMIT = 56 * 1024 * 1024


def _cparams(sem):
    return pltpu.CompilerParams(dimension_semantics=sem, vmem_limit_bytes=VMEM_LIMIT)


def _rms(x, g):
    ms = jnp.mean(x * x, axis=-1, keepdims=True)
    return x * lax.rsqrt(ms + EPS) * g


def _norm_mod(x, g, shift, scale):
    return _rms(x, g) * (1.0 + scale) + shift


def _dot(a, b):
    return jnp.dot(a, b, preferred_element_type=F32)


def _dot_t(a, b):
    return lax.dot_general(a, b, (((1,), (1,)), ((), ())), preferred_element_type=F32)


def _adaln_kernel(v_ref, w_ref, b_ref, o_ref):
    v = v_ref[...]
    s = v * jax.nn.sigmoid(v)
    o_ref[0] = jnp.dot(s, w_ref[0], preferred_element_type=F32,
                       precision=lax.Precision.HIGHEST) + b_ref[0]


def _adaln(vec8, mod_w, mod_b):
    depth, d, n = mod_w.shape
    tn = 1536
    return pl.pallas_call(
        _adaln_kernel,
        grid=(depth, n // tn),
        in_specs=[
            pl.BlockSpec((8, d), lambda l, j: (0, 0)),
            pl.BlockSpec((1, d, tn), lambda l, j: (l, 0, j)),
            pl.BlockSpec((1, 1, tn), lambda l, j: (l, 0, j)),
        ],
        out_specs=pl.BlockSpec((1, 8, tn), lambda l, j: (l, 0, j)),
        out_shape=jax.ShapeDtypeStruct((depth, 8, n), F32),
        compiler_params=_cparams(("arbitrary", "arbitrary")),
        name="adaln",
    )(vec8, mod_w, mod_b.reshape(depth, 1, n))


def _proj_kernel(x_ref, sh_ref, sc_ref, g_ref, w_ref, cos_ref, sin_ref, qg_ref, kg_ref,
                 *out_refs, with_q):
    h = _norm_mod(x_ref[0], g_ref[...], sh_ref[0], sc_ref[0]).astype(BF16)
    cosf = cos_ref[...]
    sinf = sin_ref[...]

    def rope_heads(p, n_heads, gain, scale, o_ref):
        for hd in range(n_heads):
            t = _rms(p[:, hd * HEAD_DIM_A:(hd + 1) * HEAD_DIM_A], gain)
            t = t * cosf + pltpu.roll(t, HEAD_DIM_A // 2, axis=1) * sinf
            o_ref[0, :, hd * HEAD_DIM_A:(hd + 1) * HEAD_DIM_A] = (t * scale).astype(BF16)

    col = 0
    if with_q:
        qa_ref, qb_ref, ka_ref, va_ref, kb_ref, vb_ref = out_refs
        rope_heads(_dot(h, w_ref[:, 0:QA_COLS]), HEADS_A, qg_ref[...], HEAD_DIM_A ** -0.5, qa_ref)
        qb = _dot(h, w_ref[:, QA_COLS:QA_COLS + B_COLS])
        qb_ref[0] = (qb * HEAD_DIM_B ** -0.5).astype(BF16)
        col = QA_COLS + B_COLS
    else:
        ka_ref, va_ref, kb_ref, vb_ref = out_refs
    rope_heads(_dot(h, w_ref[:, col:col + KA_COLS]), KV_HEADS_A, kg_ref[...], 1.0, ka_ref)
    col += KA_COLS
    va_ref[0] = _dot(h, w_ref[:, col:col + KA_COLS]).astype(BF16)
    col += KA_COLS
    kb_ref[0] = _dot(h, w_ref[:, col:col + B_COLS]).astype(BF16)
    col += B_COLS
    vb_ref[0] = _dot(h, w_ref[:, col:col + B_COLS]).astype(BF16)


def _proj(x, shift, scale, gain, w, cosf, sinf, qg, kg, *, with_q, tm, name):
    b, n, d = x.shape
    ncol = w.shape[1]
    widths = ([QA_COLS, B_COLS] if with_q else []) + [KA_COLS, KA_COLS, B_COLS, B_COLS]
    vec = lambda bb, i: (bb, 0, 0)
    return pl.pallas_call(
        functools.partial(_proj_kernel, with_q=with_q),
        grid=(b, n // tm),
        in_specs=[
            pl.BlockSpec((1, tm, d), lambda bb, i: (bb, i, 0)),
            pl.BlockSpec((1, 1, d), vec),
            pl.BlockSpec((1, 1, d), vec),
            pl.BlockSpec((1, d), lambda bb, i: (0, 0)),
            pl.BlockSpec((d, ncol), lambda bb, i: (0, 0)),
            pl.BlockSpec((tm, HEAD_DIM_A), lambda bb, i: (i, 0)),
            pl.BlockSpec((tm, HEAD_DIM_A), lambda bb, i: (i, 0)),
            pl.BlockSpec((1, HEAD_DIM_A), lambda bb, i: (0, 0)),
            pl.BlockSpec((1, HEAD_DIM_A), lambda bb, i: (0, 0)),
        ],
        out_specs=[pl.BlockSpec((1, tm, wd), lambda bb, i: (bb, i, 0)) for wd in widths],
        out_shape=[jax.ShapeDtypeStruct((b, n, wd), BF16) for wd in widths],
        compiler_params=_cparams(("parallel", "parallel")),
        name=name,
    )(x, shift, scale, gain, w, cosf, sinf, qg, kg)


def _flash_update(q, k, v, m_ref, l_ref, acc_ref):
    s = _dot_t(q, k)
    m_old = m_ref[...]
    m_new = jnp.maximum(m_old, jnp.max(s, axis=-1, keepdims=True))
    alpha = jnp.exp(m_old - m_new)
    p = jnp.exp(s - m_new)
    l_ref[...] = alpha * l_ref[...] + jnp.sum(p, axis=-1, keepdims=True)
    acc_ref[...] = alpha * acc_ref[...] + _dot(p.astype(BF16), v)
    m_ref[...] = m_new


def _gqa_kernel(q_ref, k_ref, v_ref, ck_ref, cv_ref, o_ref, m_ref, l_ref, acc_ref, *, tq, tk):
    q = q_ref[0]
    group = q.shape[1] // HEAD_DIM_A
    q2 = jnp.concatenate([q[:, g * HEAD_DIM_A:(g + 1) * HEAD_DIM_A] for g in range(group)], axis=0)
    m_ref[...] = jnp.full(m_ref.shape, NEG, F32)
    l_ref[...] = jnp.zeros(l_ref.shape, F32)
    acc_ref[...] = jnp.zeros(acc_ref.shape, F32)

    def body(i, carry):
        off = pl.multiple_of(i * tk, tk)
        _flash_update(q2, k_ref[0, pl.ds(off, tk), :], v_ref[0, pl.ds(off, tk), :],
                      m_ref, l_ref, acc_ref)
        return carry

    lax.fori_loop(0, k_ref.shape[1] // tk, body, 0)
    _flash_update(q2, ck_ref[0], cv_ref[0], m_ref, l_ref, acc_ref)
    o = acc_ref[...] / l_ref[...]
    o_ref[0] = jnp.concatenate([o[g * tq:(g + 1) * tq] for g in range(group)], axis=1).astype(BF16)


def _gqa(qa, ka, va, cka, cva, *, tq, tk):
    b, n, _ = qa.shape
    c = cka.shape[1]
    group = HEADS_A // KV_HEADS_A
    gw = group * HEAD_DIM_A
    return pl.pallas_call(
        functools.partial(_gqa_kernel, tq=tq, tk=tk),
        grid=(b, KV_HEADS_A, n // tq),
        in_specs=[
            pl.BlockSpec((1, tq, gw), lambda bb, h, i: (bb, i, h)),
            pl.BlockSpec((1, n, HEAD_DIM_A), lambda bb, h, i: (bb, 0, h)),
            pl.BlockSpec((1, n, HEAD_DIM_A), lambda bb, h, i: (bb, 0, h)),
            pl.BlockSpec((1, c, HEAD_DIM_A), lambda bb, h, i: (bb, 0, h)),
            pl.BlockSpec((1, c, HEAD_DIM_A), lambda bb, h, i: (bb, 0, h)),
        ],
        out_specs=pl.BlockSpec((1, tq, gw), lambda bb, h, i: (bb, i, h)),
        out_shape=jax.ShapeDtypeStruct((b, n, QA_COLS), BF16),
        scratch_shapes=[
            pltpu.VMEM((group * tq, 1), F32),
            pltpu.VMEM((group * tq, 1), F32),
            pltpu.VMEM((group * tq, HEAD_DIM_A), F32),
        ],
        compiler_params=_cparams(("parallel", "parallel", "arbitrary")),
        name="gqa",
    )(qa, ka, va, cka, cva)


def _na_kernel(q_ref, k_ref, v_ref, ck_ref, cv_ref, bias_ref, o_ref):
    rows = q_ref.shape[1] // GRID_W
    win = NA_ROWS * GRID_W
    lane = lax.broadcasted_iota(jnp.int32, (GRID_W, 2 * HEAD_DIM_B), 1)
    lo = lane < HEAD_DIM_B
    ck = ck_ref[0]
    cv = cv_ref[0]

    def body(r, carry):
        r0 = jnp.clip(r - NA_ROWS // 2, 0, rows - NA_ROWS)
        q = q_ref[0, pl.ds(pl.multiple_of(r * GRID_W, GRID_W), GRID_W), :]
        zero = jnp.zeros_like(q)
        qq = jnp.concatenate([jnp.where(lo, q, zero), jnp.where(lo, zero, q)], axis=0)
        koff = pl.multiple_of(r0 * GRID_W, GRID_W)
        kw = k_ref[0, pl.ds(koff, win), :]
        vw = v_ref[0, pl.ds(koff, win), :]
        s = _dot_t(qq, kw) + bias_ref[r - r0]
        sc = _dot_t(qq, ck)
        m = jnp.maximum(jnp.max(s, axis=-1, keepdims=True), jnp.max(sc, axis=-1, keepdims=True))
        p = jnp.exp(s - m)
        pc = jnp.exp(sc - m)
        l = jnp.sum(p, axis=-1, keepdims=True) + jnp.sum(pc, axis=-1, keepdims=True)
        o = (_dot(p.astype(BF16), vw) + _dot(pc.astype(BF16), cv)) / l
        res = jnp.where(lo, o[:GRID_W], o[GRID_W:])
        o_ref[0, pl.ds(pl.multiple_of(r * GRID_W, GRID_W), GRID_W), :] = res.astype(BF16)
        return carry

    lax.fori_loop(0, rows, body, 0)


def _na(qb, kb, vb, ckb, cvb, bias):
    b, n, _ = qb.shape
    c = ckb.shape[1]
    pw = 2 * HEAD_DIM_B
    n_cls = bias.shape[0]
    tok = lambda bb, h: (bb, 0, h)
    return pl.pallas_call(
        _na_kernel,
        grid=(b, HEADS_B // 2),
        in_specs=[
            pl.BlockSpec((1, n, pw), tok),
            pl.BlockSpec((1, n, pw), tok),
            pl.BlockSpec((1, n, pw), tok),
            pl.BlockSpec((1, c, pw), tok),
            pl.BlockSpec((1, c, pw), tok),
            pl.BlockSpec((n_cls, 2 * GRID_W, NA_ROWS * GRID_W), lambda bb, h: (0, h, 0)),
        ],
        out_specs=pl.BlockSpec((1, n, pw), tok),
        out_shape=jax.ShapeDtypeStruct((b, n, B_COLS), BF16),
        compiler_params=_cparams(("parallel", "parallel")),
        name="natten",
    )(qb, kb, vb, ckb, cvb, bias)


def _na_bias_table(rpb):
    cls = np.arange(NA_ROWS)[:, None, None, None]
    c = np.arange(GRID_W)[None, :, None, None]
    i = np.arange(NA_ROWS)[None, None, :, None]
    j = np.arange(GRID_W)[None, None, None, :]
    cs = np.clip(c - NA_COLS // 2, 0, GRID_W - NA_COLS)
    valid = (j >= cs) & (j < cs + NA_COLS)
    valid = np.broadcast_to(valid, (NA_ROWS, GRID_W, NA_ROWS, GRID_W))
    ro = np.broadcast_to(i - cls + NA_ROWS - 1, valid.shape)
    co = np.broadcast_to(np.clip(j - c + NA_COLS - 1, 0, 2 * NA_COLS - 2), valid.shape)
    t = rpb[:, ro, co]
    t = jnp.where(valid[None], t, NEG)
    t = t.transpose(1, 0, 2, 3, 4)
    return t.reshape(NA_ROWS, HEADS_B * GRID_W, NA_ROWS * GRID_W)


def _outproj_kernel(x_ref, oa_ref, ob_ref, w_ref, g1_ref, o_ref):
    ka = oa_ref.shape[2]
    y = _dot(oa_ref[0], w_ref[0:ka, :]) + _dot(ob_ref[0], w_ref[ka:, :])
    o_ref[0] = x_ref[0] + g1_ref[0] * y


def _outproj(x, oa, ob, w, g1, *, tm):
    b, n, d = x.shape
    row = lambda bb, i: (bb, i, 0)
    return pl.pallas_call(
        _outproj_kernel,
        grid=(b, n // tm),
        in_specs=[
            pl.BlockSpec((1, tm, d), row),
            pl.BlockSpec((1, tm, oa.shape[2]), row),
            pl.BlockSpec((1, tm, ob.shape[2]), row),
            pl.BlockSpec(w.shape, lambda bb, i: (0, 0)),
            pl.BlockSpec((1, 1, d), lambda bb, i: (bb, 0, 0)),
        ],
        out_specs=pl.BlockSpec((1, tm, d), row),
        out_shape=jax.ShapeDtypeStruct(x.shape, F32),
        compiler_params=_cparams(("parallel", "parallel")),
        name="attn_out",
    )(x, oa, ob, w, g1)


HALO = 8
FFN_CHUNK = 256


def _ffn_kernel(x_ref, xp_ref, xn_ref, sh_ref, sc_ref, g2_ref, ng_ref, wup_ref, cw_ref, cb_ref,
                wdn_ref, fg_ref, o_ref, ug_ref, uv_ref, *, tm, d_ff, final_norm):
    i = pl.program_id(1)
    ng, sh, sc = ng_ref[...], sh_ref[0], sc_ref[0]
    x = x_ref[0]
    hp = jnp.where(i > 0, _norm_mod(xp_ref[0], ng, sh, sc), 0.0)
    hn = jnp.where(i < pl.num_programs(1) - 1, _norm_mod(xn_ref[0], ng, sh, sc), 0.0)
    h = jnp.concatenate([hp, _norm_mod(x, ng, sh, sc), hn], axis=0).astype(BF16)

    def conv(u_ref, col):
        cw = cw_ref[:, col:col + FFN_CHUNK]
        return (u_ref[pl.ds(HALO - 1, tm), :] * cw[0:1] + u_ref[pl.ds(HALO, tm), :] * cw[1:2]
                + u_ref[pl.ds(HALO + 1, tm), :] * cw[2:3] + cb_ref[:, col:col + FFN_CHUNK])

    acc = jnp.zeros((tm, x.shape[1]), F32)
    for f in range(d_ff // FFN_CHUNK):
        c0 = f * FFN_CHUNK
        ug_ref[...] = _dot(h, wup_ref[:, c0:c0 + FFN_CHUNK])
        uv_ref[...] = _dot(h, wup_ref[:, d_ff + c0:d_ff + c0 + FFN_CHUNK])
        g = conv(ug_ref, c0)
        v = conv(uv_ref, d_ff + c0)
        a = (g * jax.nn.sigmoid(g) * v).astype(BF16)
        acc = acc + _dot(a, wdn_ref[c0:c0 + FFN_CHUNK, :])
    y = x + g2_ref[0] * acc
    if final_norm:
        y = _rms(y, fg_ref[...])
    o_ref[0] = y


def _ffn(x, shift, scale, gate, norm_g, w_up, conv_w, conv_b, w_down, final_g, *, tm, final_norm):
    b, n, d = x.shape
    d_ff = w_down.shape[0]
    hb = tm // HALO
    last = n // HALO - 1
    vec = lambda bb, i: (bb, 0, 0)
    const2 = lambda bb, i: (0, 0)
    resident = dict(pipeline_mode=pl.Buffered(1))
    return pl.pallas_call(
        functools.partial(_ffn_kernel, tm=tm, d_ff=d_ff, final_norm=final_norm),
        grid=(b, n // tm),
        in_specs=[
            pl.BlockSpec((1, tm, d), lambda bb, i: (bb, i, 0)),
            pl.BlockSpec((1, HALO, d), lambda bb, i: (bb, jnp.maximum(i * hb - 1, 0), 0)),
            pl.BlockSpec((1, HALO, d), lambda bb, i: (bb, jnp.minimum((i + 1) * hb, last), 0)),
            pl.BlockSpec((1, 1, d), vec),
            pl.BlockSpec((1, 1, d), vec),
            pl.BlockSpec((1, 1, d), vec),
            pl.BlockSpec((1, d), const2),
            pl.BlockSpec(w_up.shape, const2, **resident),
            pl.BlockSpec(conv_w.shape, const2),
            pl.BlockSpec((1, 2 * d_ff), const2),
            pl.BlockSpec(w_down.shape, const2, **resident),
            pl.BlockSpec((1, d), const2),
        ],
        out_specs=pl.BlockSpec((1, tm, d), lambda bb, i: (bb, i, 0)),
        out_shape=jax.ShapeDtypeStruct(x.shape, F32),
        scratch_shapes=[
            pltpu.VMEM((tm + 2 * HALO, FFN_CHUNK), F32),
            pltpu.VMEM((tm + 2 * HALO, FFN_CHUNK), F32),
        ],
        compiler_params=_cparams(("parallel", "parallel")),
        name="conv_ffn",
    )(x, x, x, shift, scale, gate, norm_g, w_up, conv_w, conv_b.reshape(1, -1), w_down, final_g)


def _dft_tables(n_tok, cg):
    w = GRID_W
    idx = np.arange(w, dtype=np.float64)
    a64 = 2.0 * np.pi * np.outer(idx, idx) / w
    fc, fs = np.cos(a64), np.sin(a64)
    step1 = np.block([[fc, fs], [-fs, fc]])
    step3 = np.concatenate([fc, fs], axis=1)
    atw = 2.0 * np.pi * np.outer(idx, idx) / n_tok
    tw_c = np.repeat(np.cos(atw)[:, :, None], 128, axis=2)
    tw_s = np.repeat(np.sin(atw)[:, :, None], 128, axis=2)
    ch = np.arange(cg, dtype=np.float64)
    ach = 2.0 * np.pi * np.outer(ch, ch) / cg
    chan = np.concatenate([np.cos(ach), -np.sin(ach)], axis=1)
    f = lambda t: jnp.asarray(t, F32)
    return f(chan), f(step1), f(step3), f(tw_c), f(tw_s)


def _f1_kernel(x_ref, sh_ref, sc_ref, g_ref, chan_ref, s1_ref, twc_ref, tws_ref, ar_ref, ai_ref,
               *, nc, d):
    w = GRID_W
    cg = d // FOURIER_GROUPS
    hs = [_norm_mod(x_ref[0, :, c * d:(c + 1) * d], g_ref[...], sh_ref[0], sc_ref[0]).astype(BF16)
          for c in range(nc)]
    h = jnp.concatenate(hs, axis=0)
    us = [_dot(h[:, g * cg:(g + 1) * cg], chan_ref[...]) for g in range(FOURIER_GROUPS)]
    ur = jnp.concatenate([u[:, :cg] for u in us], axis=1)
    ui = jnp.concatenate([u[:, cg:] for u in us], axis=1)
    for c in range(nc):
        uc = jnp.concatenate([ur[c * w:(c + 1) * w], ui[c * w:(c + 1) * w]], axis=0).astype(BF16)
        a = _dot(s1_ref[...], uc)
        a_r, a_i = a[:w], a[w:]
        tc = jnp.tile(twc_ref[c], (1, d // 128))
        ts = jnp.tile(tws_ref[c], (1, d // 128))
        ar_ref[0, :, c * d:(c + 1) * d] = (a_r * tc + a_i * ts).astype(BF16)
        ai_ref[0, :, c * d:(c + 1) * d] = (a_i * tc - a_r * ts).astype(BF16)


def _f2_kernel(ar_ref, ai_ref, x_ref, g1_ref, s3_ref, w_ref, o_ref, *, nk, d, inv_norm):
    w = GRID_W
    zs = []
    for k in range(nk):
        a = jnp.concatenate([ar_ref[0, k * w:(k + 1) * w, :], ai_ref[0, k * w:(k + 1) * w, :]], axis=0)
        zs.append((_dot(s3_ref[...], a) * inv_norm).astype(BF16))
    y = _dot(jnp.concatenate(zs, axis=0), w_ref[...])
    for k in range(nk):
        o_ref[0, :, k * d:(k + 1) * d] = x_ref[0, :, k * d:(k + 1) * d] + g1_ref[0] * y[k * w:(k + 1) * w]


def _fourier(x, shift, scale, gate, norm_g, w_out, *, nc, nk):
    b, n, d = x.shape
    w = GRID_W
    assert n == w * w
    cg = d // FOURIER_GROUPS
    chan, step1, step3, tw_c, tw_s = _dft_tables(n, cg)
    chan, step1, step3 = chan.astype(BF16), step1.astype(BF16), step3.astype(BF16)
    xv = x.reshape(b, w, w * d)
    vec = lambda bb, j: (bb, 0, 0)
    const2 = lambda bb, j: (0, 0)
    a_r, a_i = pl.pallas_call(
        functools.partial(_f1_kernel, nc=nc, d=d),
        grid=(b, w // nc),
        in_specs=[
            pl.BlockSpec((1, w, nc * d), lambda bb, j: (bb, 0, j)),
            pl.BlockSpec((1, 1, d), vec),
            pl.BlockSpec((1, 1, d), vec),
            pl.BlockSpec((1, d), const2),
            pl.BlockSpec(chan.shape, const2),
            pl.BlockSpec(step1.shape, const2),
            pl.BlockSpec((nc, w, 128), lambda bb, j: (j, 0, 0)),
            pl.BlockSpec((nc, w, 128), lambda bb, j: (j, 0, 0)),
        ],
        out_specs=[pl.BlockSpec((1, w, nc * d), lambda bb, j: (bb, 0, j))] * 2,
        out_shape=[jax.ShapeDtypeStruct((b, w, w * d), BF16)] * 2,
        compiler_params=_cparams(("parallel", "parallel")),
        name="fourier_rows",
    )(xv, shift, scale, norm_g, chan, step1, tw_c, tw_s)
    a_r = a_r.reshape(b, n, d)
    a_i = a_i.reshape(b, n, d)
    out = pl.pallas_call(
        functools.partial(_f2_kernel, nk=nk, d=d, inv_norm=1.0 / float(np.sqrt(n * cg))),
        grid=(b, w // nk),
        in_specs=[
            pl.BlockSpec((1, nk * w, d), lambda bb, j: (bb, j, 0)),
            pl.BlockSpec((1, nk * w, d), lambda bb, j: (bb, j, 0)),
            pl.BlockSpec((1, w, nk * d), lambda bb, j: (bb, 0, j)),
            pl.BlockSpec((1, 1, d), vec),
            pl.BlockSpec(step3.shape, const2),
            pl.BlockSpec(w_out.shape, const2),
        ],
        out_specs=pl.BlockSpec((1, w, nk * d), lambda bb, j: (bb, 0, j)),
        out_shape=jax.ShapeDtypeStruct((b, w, w * d), F32),
        compiler_params=_cparams(("parallel", "parallel")),
        name="fourier_cols",
    )(a_r, a_i, xv, gate, step3, w_out)
    return out.reshape(b, n, d)


def _rope_tables(n):
    t = jnp.arange(n)
    row = (t // GRID_W).astype(F32)
    col = (t % GRID_W).astype(F32)
    pairs = HEAD_DIM_A // 4
    freqs = ROPE_THETA ** (-jnp.arange(pairs, dtype=F32) / pairs)
    ang = jnp.concatenate([row[:, None] * freqs, col[:, None] * freqs], axis=-1)
    cos, sin = jnp.cos(ang), jnp.sin(ang)
    return jnp.concatenate([cos, cos], axis=-1), jnp.concatenate([-sin, sin], axis=-1)


def kernel(x, c, ctx, c_ctx, mod_w, mod_b, norm1_g, norm2_g, attn_w_in, attn_w_out, q_norm_g,
           k_norm_g, na_rpb, fourier_w_out, ffn_w_up, ffn_conv_w, ffn_conv_b, ffn_w_down, final_g):
    b, n, d = x.shape
    depth = mod_w.shape[0]
    assert depth == 2 and b + 1 <= 8

    vec8 = jnp.zeros((8, d), F32).at[:b].set(c).at[b].set(c_ctx)
    mod = _adaln(vec8, mod_w, mod_b)
    term = lambda l, k: mod[l, :b, k * d:(k + 1) * d].reshape(b, 1, d)
    cterm = lambda l, k: jnp.broadcast_to(mod[l, b, k * d:(k + 1) * d].reshape(1, 1, d), (b, 1, d))

    perm = np.concatenate([np.arange(0, HEAD_DIM_A, 2), np.arange(1, HEAD_DIM_A, 2)])
    cols = np.arange(attn_w_in.shape[2])
    q_cols = QA_COLS + B_COLS
    for base, nh in ((0, HEADS_A), (q_cols, KV_HEADS_A)):
        for hd in range(nh):
            s0 = base + hd * HEAD_DIM_A
            cols[s0:s0 + HEAD_DIM_A] = s0 + perm
    w_in = attn_w_in[0][:, cols].astype(BF16)
    qg = q_norm_g[0][perm].reshape(1, -1)
    kg = k_norm_g[0][perm].reshape(1, -1)
    cosf, sinf = _rope_tables(n)
    n_ctx = ctx.shape[1]
    g1row = norm1_g[0].reshape(1, d)

    qa, qb, ka, va, kb, vb = _proj(x, term(0, 0), term(0, 1), g1row, w_in, cosf, sinf, qg, kg,
                                   with_q=True, tm=512, name="proj_latent")
    cka, cva, ckb, cvb = _proj(ctx, cterm(0, 0), cterm(0, 1), g1row, w_in[:, q_cols:],
                               jnp.ones((n_ctx, HEAD_DIM_A), F32), jnp.zeros((n_ctx, HEAD_DIM_A), F32),
                               qg, kg, with_q=False, tm=n_ctx, name="proj_context")
    oa = _gqa(qa, ka, va, cka, cva, tq=256, tk=512)
    ob = _na(qb, kb, vb, ckb, cvb, _na_bias_table(na_rpb[0]))
    x = _outproj(x, oa, ob, attn_w_out[0].astype(BF16), term(0, 2), tm=1024)
    x = _ffn(x, term(0, 3), term(0, 4), term(0, 5), norm2_g[0].reshape(1, d),
             ffn_w_up[0].astype(BF16), ffn_conv_w[0], ffn_conv_b[0], ffn_w_down[0].astype(BF16),
             final_g.reshape(1, d), tm=512, final_norm=False)

    x = _fourier(x, term(1, 0), term(1, 1), term(1, 2), norm1_g[1].reshape(1, d),
                 fourier_w_out[0].astype(BF16), nc=4, nk=4)
    x = _ffn(x, term(1, 3), term(1, 4), term(1, 5), norm2_g[1].reshape(1, d),
             ffn_w_up[1].astype(BF16), ffn_conv_w[1], ffn_conv_b[1], ffn_w_down[1].astype(BF16),
             final_g.reshape(1, d), tm=512, final_norm=True)
    return x
```

```python
import functools

import numpy as np
import jax
import jax.numpy as jnp
from jax import lax
from jax.experimental import pallas as pl
from jax.experimental.pallas import tpu as pltpu

F32 = jnp.float32
BF16 = jnp.bfloat16

GRID_W = 64
HEAD_DIM_A = 128
HEADS_A = 4
KV_HEADS_A = 2
HEAD_DIM_B = 64
HEADS_B = 8
NA_ROWS = 8
NA_COLS = 16
FOURIER_GROUPS = 4
ROPE_THETA = 10000.0
EPS = 1e-6
QA_COLS = HEADS_A * HEAD_DIM_A
KA_COLS = KV_HEADS_A * HEAD_DIM_A
B_COLS = HEADS_B * HEAD_DIM_B
NEG = -1e30
LOG2E = 1.4426950408889634

VMEM_LIMIT = 56 * 1024 * 1024


def _cparams(sem):
    return pltpu.CompilerParams(dimension_semantics=sem, vmem_limit_bytes=VMEM_LIMIT)


def _rms(x, g):
    ms = jnp.mean(x * x, axis=-1, keepdims=True)
    return x * lax.rsqrt(ms + EPS) * g


def _norm_mod(x, g, shift, scale):
    return _rms(x, g) * (1.0 + scale) + shift


def _dot(a, b):
    return jnp.dot(a, b, preferred_element_type=F32)


def _dot_t(a, b):
    return lax.dot_general(a, b, (((1,), (1,)), ((), ())), preferred_element_type=F32)


def _adaln_kernel(v_ref, w_ref, b_ref, o_ref):
    v = v_ref[...]
    s = v * jax.nn.sigmoid(v)
    o_ref[0] = jnp.dot(s, w_ref[0], preferred_element_type=F32,
                       precision=lax.Precision.HIGHEST) + b_ref[0]


def _adaln(vec8, mod_w, mod_b):
    depth, d, n = mod_w.shape
    tn = 1536
    return pl.pallas_call(
        _adaln_kernel,
        grid=(depth, n // tn),
        in_specs=[
            pl.BlockSpec((8, d), lambda l, j: (0, 0)),
            pl.BlockSpec((1, d, tn), lambda l, j: (l, 0, j)),
            pl.BlockSpec((1, 1, tn), lambda l, j: (l, 0, j)),
        ],
        out_specs=pl.BlockSpec((1, 8, tn), lambda l, j: (l, 0, j)),
        out_shape=jax.ShapeDtypeStruct((depth, 8, n), F32),
        compiler_params=_cparams(("arbitrary", "arbitrary")),
        name="adaln",
    )(vec8, mod_w, mod_b.reshape(depth, 1, n))


def _proj_kernel(x_ref, sh_ref, sc_ref, g_ref, w_ref, cos_ref, sin_ref, qg_ref, kg_ref,
                 *out_refs, with_q):
    h = _norm_mod(x_ref[0], g_ref[...], sh_ref[0], sc_ref[0]).astype(BF16)
    cosf = cos_ref[...]
    sinf = sin_ref[...]

    def rope_heads(p, n_heads, gain, scale, o_ref):
        for hd in range(n_heads):
            t = _rms(p[:, hd * HEAD_DIM_A:(hd + 1) * HEAD_DIM_A], gain)
            t = t * cosf + pltpu.roll(t, HEAD_DIM_A // 2, axis=1) * sinf
            o_ref[0, :, hd * HEAD_DIM_A:(hd + 1) * HEAD_DIM_A] = (t * scale).astype(BF16)

    col = 0
    if with_q:
        qa_ref, qb_ref, ka_ref, va_ref, kb_ref, vb_ref = out_refs
        rope_heads(_dot(h, w_ref[:, 0:QA_COLS]), HEADS_A, qg_ref[...], HEAD_DIM_A ** -0.5 * LOG2E,
                   qa_ref)
        qb = _dot(h, w_ref[:, QA_COLS:QA_COLS + B_COLS])
        qb_ref[0] = (qb * HEAD_DIM_B ** -0.5).astype(BF16)
        col = QA_COLS + B_COLS
    else:
        ka_ref, va_ref, kb_ref, vb_ref = out_refs
    rope_heads(_dot(h, w_ref[:, col:col + KA_COLS]), KV_HEADS_A, kg_ref[...], 1.0, ka_ref)
    col += KA_COLS
    va_ref[0] = _dot(h, w_ref[:, col:col + KA_COLS]).astype(BF16)
    col += KA_COLS
    kb_ref[0] = _dot(h, w_ref[:, col:col + B_COLS]).astype(BF16)
    col += B_COLS
    vb_ref[0] = _dot(h, w_ref[:, col:col + B_COLS]).astype(BF16)


def _proj(x, shift, scale, gain, w, cosf, sinf, qg, kg, *, with_q, tm, name):
    b, n, d = x.shape
    ncol = w.shape[1]
    widths = ([QA_COLS, B_COLS] if with_q else []) + [KA_COLS, KA_COLS, B_COLS, B_COLS]
    vec = lambda bb, i: (bb, 0, 0)
    return pl.pallas_call(
        functools.partial(_proj_kernel, with_q=with_q),
        grid=(b, n // tm),
        in_specs=[
            pl.BlockSpec((1, tm, d), lambda bb, i: (bb, i, 0)),
            pl.BlockSpec((1, 1, d), vec),
            pl.BlockSpec((1, 1, d), vec),
            pl.BlockSpec((1, d), lambda bb, i: (0, 0)),
            pl.BlockSpec((d, ncol), lambda bb, i: (0, 0)),
            pl.BlockSpec((tm, HEAD_DIM_A), lambda bb, i: (i, 0)),
            pl.BlockSpec((tm, HEAD_DIM_A), lambda bb, i: (i, 0)),
            pl.BlockSpec((1, HEAD_DIM_A), lambda bb, i: (0, 0)),
            pl.BlockSpec((1, HEAD_DIM_A), lambda bb, i: (0, 0)),
        ],
        out_specs=[pl.BlockSpec((1, tm, wd), lambda bb, i: (bb, i, 0)) for wd in widths],
        out_shape=[jax.ShapeDtypeStruct((b, n, wd), BF16) for wd in widths],
        compiler_params=_cparams(("parallel", "parallel")),
        name=name,
    )(x, shift, scale, gain, w, cosf, sinf, qg, kg)


def _flash_update(q, k, v, m_ref, l_ref, acc_ref):
    d = v.shape[1]
    s = _dot_t(q, k)
    m_old = m_ref[...]
    m_new = jnp.maximum(m_old, jnp.max(s, axis=-1, keepdims=True))
    alpha = jnp.exp2(m_old - m_new)
    p = jnp.exp2(s - pltpu.repeat(m_new, s.shape[1] // m_new.shape[1], axis=1))
    o = _dot(p.astype(BF16), jnp.concatenate([v, jnp.ones_like(v)], axis=1))
    acc_ref[...] = alpha * acc_ref[...] + o[:, :d]
    l_ref[...] = alpha * l_ref[...] + o[:, d:]
    m_ref[...] = m_new


def _gqa_kernel(q_ref, k_ref, v_ref, ck_ref, cv_ref, o_ref, m_ref, l_ref, acc_ref, *, tk):
    group = q_ref.shape[2] // HEAD_DIM_A
    m_ref[...] = jnp.full(m_ref.shape, NEG, F32)
    l_ref[...] = jnp.zeros(l_ref.shape, F32)
    acc_ref[...] = jnp.zeros(acc_ref.shape, F32)

    def update(k, v):
        for g in range(group):
            _flash_update(q_ref[0, :, g * HEAD_DIM_A:(g + 1) * HEAD_DIM_A], k, v,
                          m_ref.at[g], l_ref.at[g], acc_ref.at[g])

    def body(i, carry):
        off = pl.multiple_of(i * tk, tk)
        update(k_ref[0, pl.ds(off, tk), :], v_ref[0, pl.ds(off, tk), :])
        return carry

    lax.fori_loop(0, k_ref.shape[1] // tk, body, 0)
    update(ck_ref[0], cv_ref[0])
    o_ref[0] = jnp.concatenate([acc_ref[g] / l_ref[g] for g in range(group)], axis=1).astype(BF16)


def _gqa(qa, ka, va, cka, cva, *, tq, tk):
    b, n, _ = qa.shape
    c = cka.shape[1]
    group = HEADS_A // KV_HEADS_A
    gw = group * HEAD_DIM_A
    return pl.pallas_call(
        functools.partial(_gqa_kernel, tk=tk),
        grid=(b, KV_HEADS_A, n // tq),
        in_specs=[
            pl.BlockSpec((1, tq, gw), lambda bb, h, i: (bb, i, h)),
            pl.BlockSpec((1, n, HEAD_DIM_A), lambda bb, h, i: (bb, 0, h)),
            pl.BlockSpec((1, n, HEAD_DIM_A), lambda bb, h, i: (bb, 0, h)),
            pl.BlockSpec((1, c, HEAD_DIM_A), lambda bb, h, i: (bb, 0, h)),
            pl.BlockSpec((1, c, HEAD_DIM_A), lambda bb, h, i: (bb, 0, h)),
        ],
        out_specs=pl.BlockSpec((1, tq, gw), lambda bb, h, i: (bb, i, h)),
        out_shape=jax.ShapeDtypeStruct((b, n, QA_COLS), BF16),
        scratch_shapes=[
            pltpu.VMEM((group, tq, HEAD_DIM_A), F32),
            pltpu.VMEM((group, tq, HEAD_DIM_A), F32),
            pltpu.VMEM((group, tq, HEAD_DIM_A), F32),
        ],
        compiler_params=_cparams(("parallel", "parallel", "arbitrary")),
        name="gqa",
    )(qa, ka, va, cka, cva)


def _na_kernel(q_ref, k_ref, v_ref, ck_ref, cv_ref, bias_ref, o_ref):
    rows = q_ref.shape[1] // GRID_W
    win = NA_ROWS * GRID_W
    lane = lax.broadcasted_iota(jnp.int32, (GRID_W, 2 * HEAD_DIM_B), 1)
    lo = lane < HEAD_DIM_B
    ck = ck_ref[0]
    cv = cv_ref[0]

    def body(r, carry):
        r0 = jnp.clip(r - NA_ROWS // 2, 0, rows - NA_ROWS)
        q = q_ref[0, pl.ds(pl.multiple_of(r * GRID_W, GRID_W), GRID_W), :]
        zero = jnp.zeros_like(q)
        qq = jnp.concatenate([jnp.where(lo, q, zero), jnp.where(lo, zero, q)], axis=0)
        koff = pl.multiple_of(r0 * GRID_W, GRID_W)
        kw = k_ref[0, pl.ds(koff, win), :]
        vw = v_ref[0, pl.ds(koff, win), :]
        s = _dot_t(qq, kw) + bias_ref[r - r0]
        sc = _dot_t(qq, ck)
        m = jnp.maximum(jnp.max(s, axis=-1, keepdims=True), jnp.max(sc, axis=-1, keepdims=True))
        p = jnp.exp(s - m)
        pc = jnp.exp(sc - m)
        l = jnp.sum(p, axis=-1, keepdims=True) + jnp.sum(pc, axis=-1, keepdims=True)
        o = (_dot(p.astype(BF16), vw) + _dot(pc.astype(BF16), cv)) / l
        res = jnp.where(lo, o[:GRID_W], o[GRID_W:])
        o_ref[0, pl.ds(pl.multiple_of(r * GRID_W, GRID_W), GRID_W), :] = res.astype(BF16)
        return carry

    lax.fori_loop(0, rows, body, 0)


def _na(qb, kb, vb, ckb, cvb, bias):
    b, n, _ = qb.shape
    c = ckb.shape[1]
    pw = 2 * HEAD_DIM_B
    n_cls = bias.shape[0]
    tok = lambda bb, h: (bb, 0, h)
    return pl.pallas_call(
        _na_kernel,
        grid=(b, HEADS_B // 2),
        in_specs=[
            pl.BlockSpec((1, n, pw), tok),
            pl.BlockSpec((1, n, pw), tok),
            pl.BlockSpec((1, n, pw), tok),
            pl.BlockSpec((1, c, pw), tok),
            pl.BlockSpec((1, c, pw), tok),
            pl.BlockSpec((n_cls, 2 * GRID_W, NA_ROWS * GRID_W), lambda bb, h: (0, h, 0)),
        ],
        out_specs=pl.BlockSpec((1, n, pw), tok),
        out_shape=jax.ShapeDtypeStruct((b, n, B_COLS), BF16),
        compiler_params=_cparams(("parallel", "parallel")),
        name="natten",
    )(qb, kb, vb, ckb, cvb, bias)


def _na_bias_table(rpb):
    edge = GRID_W - NA_COLS
    padded = jnp.pad(rpb, ((0, 0), (0, 0), (edge, edge)))
    e = jnp.stack([padded[:, :, GRID_W - 1 - c:2 * GRID_W - 1 - c] for c in range(GRID_W)], axis=2)
    t = jnp.stack([e[:, NA_ROWS - 1 - s:2 * NA_ROWS - 1 - s] for s in range(NA_ROWS)], axis=0)
    t = t.transpose(0, 1, 3, 2, 4)
    c = np.arange(GRID_W)[:, None, None]
    j = np.arange(GRID_W)[None, None, :]
    cs = np.clip(c - NA_COLS // 2, 0, edge)
    valid = (j >= cs) & (j < cs + NA_COLS)
    t = jnp.where(valid, t, NEG)
    return t.reshape(NA_ROWS, HEADS_B * GRID_W, NA_ROWS * GRID_W)


def _outproj_kernel(x_ref, oa_ref, ob_ref, w_ref, g1_ref, o_ref):
    ka = oa_ref.shape[2]
    y = _dot(oa_ref[0], w_ref[0:ka, :]) + _dot(ob_ref[0], w_ref[ka:, :])
    o_ref[0] = x_ref[0] + g1_ref[0] * y


def _outproj(x, oa, ob, w, g1, *, tm):
    b, n, d = x.shape
    row = lambda bb, i: (bb, i, 0)
    return pl.pallas_call(
        _outproj_kernel,
        grid=(b, n // tm),
        in_specs=[
            pl.BlockSpec((1, tm, d), row),
            pl.BlockSpec((1, tm, oa.shape[2]), row),
            pl.BlockSpec((1, tm, ob.shape[2]), row),
            pl.BlockSpec(w.shape, lambda bb, i: (0, 0)),
            pl.BlockSpec((1, 1, d), lambda bb, i: (bb, 0, 0)),
        ],
        out_specs=pl.BlockSpec((1, tm, d), row),
        out_shape=jax.ShapeDtypeStruct(x.shape, F32),
        compiler_params=_cparams(("parallel", "parallel")),
        name="attn_out",
    )(x, oa, ob, w, g1)


HALO = 8
FFN_CHUNK = 256


def _ffn_kernel(x_ref, xp_ref, xn_ref, sh_ref, sc_ref, g2_ref, ng_ref, wup_ref, cw_ref, cb_ref,
                wdn_ref, fg_ref, o_ref, ug_ref, uv_ref, *, tm, d_ff, final_norm):
    i = pl.program_id(1)
    ng, sh, sc = ng_ref[...], sh_ref[0], sc_ref[0]
    x = x_ref[0]
    hp = jnp.where(i > 0, _norm_mod(xp_ref[0], ng, sh, sc), 0.0)
    hn = jnp.where(i < pl.num_programs(1) - 1, _norm_mod(xn_ref[0], ng, sh, sc), 0.0)
    h = jnp.concatenate([hp, _norm_mod(x, ng, sh, sc), hn], axis=0).astype(BF16)

    def conv(u_ref, col):
        cw = cw_ref[:, col:col + FFN_CHUNK]
        return (u_ref[pl.ds(HALO - 1, tm), :] * cw[0:1] + u_ref[pl.ds(HALO, tm), :] * cw[1:2]
                + u_ref[pl.ds(HALO + 1, tm), :] * cw[2:3] + cb_ref[:, col:col + FFN_CHUNK])

    acc = jnp.zeros((tm, x.shape[1]), F32)
    for f in range(d_ff // FFN_CHUNK):
        c0 = f * FFN_CHUNK
        ug_ref[...] = _dot(h, wup_ref[:, c0:c0 + FFN_CHUNK])
        uv_ref[...] = _dot(h, wup_ref[:, d_ff + c0:d_ff + c0 + FFN_CHUNK])
        g = conv(ug_ref, c0)
        v = conv(uv_ref, d_ff + c0)
        a = (g * jax.nn.sigmoid(g) * v).astype(BF16)
        acc = acc + _dot(a, wdn_ref[c0:c0 + FFN_CHUNK, :])
    y = x + g2_ref[0] * acc
    if final_norm:
        y = _rms(y, fg_ref[...])
    o_ref[0] = y


def _ffn(x, shift, scale, gate, norm_g, w_up, conv_w, conv_b, w_down, final_g, *, tm, final_norm):
    b, n, d = x.shape
    d_ff = w_down.shape[0]
    hb = tm // HALO
    last = n // HALO - 1
    vec = lambda bb, i: (bb, 0, 0)
    const2 = lambda bb, i: (0, 0)
    resident = dict(pipeline_mode=pl.Buffered(1))
    return pl.pallas_call(
        functools.partial(_ffn_kernel, tm=tm, d_ff=d_ff, final_norm=final_norm),
        grid=(b, n // tm),
        in_specs=[
            pl.BlockSpec((1, tm, d), lambda bb, i: (bb, i, 0)),
            pl.BlockSpec((1, HALO, d), lambda bb, i: (bb, jnp.maximum(i * hb - 1, 0), 0)),
            pl.BlockSpec((1, HALO, d), lambda bb, i: (bb, jnp.minimum((i + 1) * hb, last), 0)),
            pl.BlockSpec((1, 1, d), vec),
            pl.BlockSpec((1, 1, d), vec),
            pl.BlockSpec((1, 1, d), vec),
            pl.BlockSpec((1, d), const2),
            pl.BlockSpec(w_up.shape, const2, **resident),
            pl.BlockSpec(conv_w.shape, const2),
            pl.BlockSpec((1, 2 * d_ff), const2),
            pl.BlockSpec(w_down.shape, const2, **resident),
            pl.BlockSpec((1, d), const2),
        ],
        out_specs=pl.BlockSpec((1, tm, d), lambda bb, i: (bb, i, 0)),
        out_shape=jax.ShapeDtypeStruct(x.shape, F32),
        scratch_shapes=[
            pltpu.VMEM((tm + 2 * HALO, FFN_CHUNK), F32),
            pltpu.VMEM((tm + 2 * HALO, FFN_CHUNK), F32),
        ],
        compiler_params=_cparams(("parallel", "parallel")),
        name="conv_ffn",
    )(x, x, x, shift, scale, gate, norm_g, w_up, conv_w, conv_b.reshape(1, -1), w_down, final_g)


LANES = 128
F1_NC = 8
F2_NK = 16


def _dft_tables(n_tok, cg):
    w = GRID_W
    idx = np.arange(w, dtype=np.float64)
    a64 = 2.0 * np.pi * np.outer(idx, idx) / w
    fc, fs = np.cos(a64), np.sin(a64)
    step1 = np.block([[fc, fs], [-fs, fc]])
    eye = np.eye(F2_NK)
    step3 = (np.kron(fc, eye), np.kron(fs, eye))
    atw = 2.0 * np.pi * np.outer(idx, idx) / n_tok
    tw_c = np.repeat(np.cos(atw)[:, :, None], 128, axis=2)
    tw_s = np.repeat(np.sin(atw)[:, :, None], 128, axis=2)
    ch = np.arange(cg, dtype=np.float64)
    ach = 2.0 * np.pi * np.outer(ch, ch) / cg
    chan = np.concatenate([np.cos(ach), -np.sin(ach)], axis=1)
    f = lambda t: jnp.asarray(t, F32)
    return f(chan), f(step1), f(step3[0]), f(step3[1]), f(tw_c), f(tw_s)


def _f1_kernel(x_ref, sh_ref, sc_ref, g_ref, chan_ref, s1_ref, twc_ref, tws_ref, ar_ref, ai_ref,
               u_ref, *, nc, d):
    w = GRID_W
    cg = d // FOURIER_GROUPS
    tiles = d // LANES
    gt = cg // LANES
    x = x_ref[0].reshape(w * nc, d)
    h = _norm_mod(x, g_ref[...], sh_ref[0], sc_ref[0]).astype(BF16)
    for g in range(FOURIER_GROUPS):
        u = _dot(h[:, g * cg:(g + 1) * cg], chan_ref[...])
        for t in range(2 * gt):
            part, tt = divmod(t, gt)
            u_ref[part * tiles + g * gt + tt] = u[:, t * LANES:(t + 1) * LANES]
    for c in range(nc):
        rows = pl.ds(c, w, stride=nc)
        ur = jnp.concatenate([u_ref[s, rows, :] for s in range(tiles)], axis=1)
        ui = jnp.concatenate([u_ref[tiles + s, rows, :] for s in range(tiles)], axis=1)
        a = _dot(s1_ref[...], jnp.concatenate([ur, ui], axis=0).astype(BF16))
        a_r, a_i = a[:w], a[w:]
        tc = jnp.tile(twc_ref[c], (1, tiles))
        ts = jnp.tile(tws_ref[c], (1, tiles))
        ar_ref[0, c] = (a_r * tc + a_i * ts).astype(BF16)
        ai_ref[0, c] = (a_i * tc - a_r * ts).astype(BF16)


def _f2_kernel(ar_ref, ai_ref, x_ref, g1_ref, kc_ref, ks_ref, w_ref, o_ref, *, nk, d, inv_norm):
    w = GRID_W
    a_r = ar_ref[0].reshape(w * nk, d)
    a_i = ai_ref[0].reshape(w * nk, d)
    z = _dot(kc_ref[...], a_r) + _dot(ks_ref[...], a_i)
    y = _dot((z * inv_norm).astype(BF16), w_ref[...])
    o_ref[0] = x_ref[0] + g1_ref[0] * y.reshape(w, nk, d)


def _fourier(x, shift, scale, gate, norm_g, w_out):
    b, n, d = x.shape
    w = GRID_W
    assert n == w * w
    nc, nk = F1_NC, F2_NK
    cg = d // FOURIER_GROUPS
    chan, step1, kc, ks, tw_c, tw_s = _dft_tables(n, cg)
    chan, step1, kc, ks = (t.astype(BF16) for t in (chan, step1, kc, ks))
    x4 = x.reshape(b, w, w, d)
    vec = lambda bb, j: (bb, 0, 0)
    const2 = lambda bb, j: (0, 0)
    resident = dict(pipeline_mode=pl.Buffered(1))
    a_r, a_i = pl.pallas_call(
        functools.partial(_f1_kernel, nc=nc, d=d),
        grid=(b, w // nc),
        in_specs=[
            pl.BlockSpec((1, w, nc, d), lambda bb, j: (bb, 0, j, 0)),
            pl.BlockSpec((1, 1, d), vec),
            pl.BlockSpec((1, 1, d), vec),
            pl.BlockSpec((1, d), const2),
            pl.BlockSpec(chan.shape, const2),
            pl.BlockSpec(step1.shape, const2),
            pl.BlockSpec((nc, w, LANES), lambda bb, j: (j, 0, 0)),
            pl.BlockSpec((nc, w, LANES), lambda bb, j: (j, 0, 0)),
        ],
        out_specs=[pl.BlockSpec((1, nc, w, d), lambda bb, j: (bb, j, 0, 0))] * 2,
        out_shape=[jax.ShapeDtypeStruct((b, w, w, d), BF16)] * 2,
        scratch_shapes=[pltpu.VMEM((2 * d // LANES, w * nc, LANES), F32)],
        compiler_params=_cparams(("parallel", "parallel")),
        name="fourier_rows",
    )(x4, shift, scale, norm_g, chan, step1, tw_c, tw_s)
    out = pl.pallas_call(
        functools.partial(_f2_kernel, nk=nk, d=d, inv_norm=1.0 / float(np.sqrt(n * cg))),
        grid=(b, w // nk),
        in_specs=[
            pl.BlockSpec((1, w, nk, d), lambda bb, j: (bb, 0, j, 0)),
            pl.BlockSpec((1, w, nk, d), lambda bb, j: (bb, 0, j, 0)),
            pl.BlockSpec((1, w, nk, d), lambda bb, j: (bb, 0, j, 0)),
            pl.BlockSpec((1, 1, d), vec),
            pl.BlockSpec(kc.shape, const2, **resident),
            pl.BlockSpec(ks.shape, const2, **resident),
            pl.BlockSpec(w_out.shape, const2, **resident),
        ],
        out_specs=pl.BlockSpec((1, w, nk, d), lambda bb, j: (bb, 0, j, 0)),
        out_shape=jax.ShapeDtypeStruct((b, w, w, d), F32),
        compiler_params=_cparams(("parallel", "parallel")),
        name="fourier_cols",
    )(a_r, a_i, x4, gate, kc, ks, w_out)
    return out.reshape(b, n, d)


def _rope_tables(n):
    t = jnp.arange(n)
    row = (t // GRID_W).astype(F32)
    col = (t % GRID_W).astype(F32)
    pairs = HEAD_DIM_A // 4
    freqs = ROPE_THETA ** (-jnp.arange(pairs, dtype=F32) / pairs)
    ang = jnp.concatenate([row[:, None] * freqs, col[:, None] * freqs], axis=-1)
    cos, sin = jnp.cos(ang), jnp.sin(ang)
    return jnp.concatenate([cos, cos], axis=-1), jnp.concatenate([-sin, sin], axis=-1)


def kernel(x, c, ctx, c_ctx, mod_w, mod_b, norm1_g, norm2_g, attn_w_in, attn_w_out, q_norm_g,
           k_norm_g, na_rpb, fourier_w_out, ffn_w_up, ffn_conv_w, ffn_conv_b, ffn_w_down, final_g):
    b, n, d = x.shape
    depth = mod_w.shape[0]
    assert depth == 2 and b + 1 <= 8

    vec8 = jnp.zeros((8, d), F32).at[:b].set(c).at[b].set(c_ctx)
    mod = _adaln(vec8, mod_w, mod_b)
    term = lambda l, k: mod[l, :b, k * d:(k + 1) * d].reshape(b, 1, d)
    cterm = lambda l, k: jnp.broadcast_to(mod[l, b, k * d:(k + 1) * d].reshape(1, 1, d), (b, 1, d))

    perm = np.concatenate([np.arange(0, HEAD_DIM_A, 2), np.arange(1, HEAD_DIM_A, 2)])
    cols = np.arange(attn_w_in.shape[2])
    q_cols = QA_COLS + B_COLS
    for base, nh in ((0, HEADS_A), (q_cols, KV_HEADS_A)):
        for hd in range(nh):
            s0 = base + hd * HEAD_DIM_A
            cols[s0:s0 + HEAD_DIM_A] = s0 + perm
    w_in = attn_w_in[0][:, cols].astype(BF16)
    qg = q_norm_g[0][perm].reshape(1, -1)
    kg = k_norm_g[0][perm].reshape(1, -1)
    cosf, sinf = _rope_tables(n)
    n_ctx = ctx.shape[1]
    g1row = norm1_g[0].reshape(1, d)

    qa, qb, ka, va, kb, vb = _proj(x, term(0, 0), term(0, 1), g1row, w_in, cosf, sinf, qg, kg,
                                   with_q=True, tm=512, name="proj_latent")
    cka, cva, ckb, cvb = _proj(ctx, cterm(0, 0), cterm(0, 1), g1row, w_in[:, q_cols:],
                               jnp.ones((n_ctx, HEAD_DIM_A), F32), jnp.zeros((n_ctx, HEAD_DIM_A), F32),
                               qg, kg, with_q=False, tm=n_ctx, name="proj_context")
    oa = _gqa(qa, ka, va, cka, cva, tq=512, tk=512)
    ob = _na(qb, kb, vb, ckb, cvb, _na_bias_table(na_rpb[0]))
    x = _outproj(x, oa, ob, attn_w_out[0].astype(BF16), term(0, 2), tm=1024)
    x = _ffn(x, term(0, 3), term(0, 4), term(0, 5), norm2_g[0].reshape(1, d),
             ffn_w_up[0].astype(BF16), ffn_conv_w[0], ffn_conv_b[0], ffn_w_down[0].astype(BF16),
             final_g.reshape(1, d), tm=512, final_norm=False)

    x = _fourier(x, term(1, 0), term(1, 1), term(1, 2), norm1_g[1].reshape(1, d),
                 fourier_w_out[0].astype(BF16))
    x = _ffn(x, term(1, 3), term(1, 4), term(1, 5), norm2_g[1].reshape(1, d),
             ffn_w_up[1].astype(BF16), ffn_conv_w[1], ffn_conv_b[1], ffn_w_down[1].astype(BF16),
             final_g.reshape(1, d), tm=512, final_norm=True)
    return x
```

```python
import functools

import numpy as np
import jax
import jax.numpy as jnp
from jax import lax
from jax.experimental import pallas as pl
from jax.experimental.pallas import tpu as pltpu

F32 = jnp.float32
BF16 = jnp.bfloat16

GRID_W = 64
HEAD_DIM_A = 128
HEADS_A = 4
KV_HEADS_A = 2
HEAD_DIM_B = 64
HEADS_B = 8
NA_ROWS = 8
NA_COLS = 16
FOURIER_GROUPS = 4
ROPE_THETA = 10000.0
EPS = 1e-6
QA_COLS = HEADS_A * HEAD_DIM_A
KA_COLS = KV_HEADS_A * HEAD_DIM_A
B_COLS = HEADS_B * HEAD_DIM_B
NEG = -1e30
LOG2E = 1.4426950408889634
LANES = 128

VMEM_LIMIT = 56 * 1024 * 1024


def _cparams(sem):
    return pltpu.CompilerParams(dimension_semantics=sem, vmem_limit_bytes=VMEM_LIMIT)


def _rms(x, g):
    ms = jnp.mean(x * x, axis=-1, keepdims=True)
    return x * lax.rsqrt(ms + EPS) * g


def _norm_mod(x, g, shift, scale):
    return _rms(x, g) * (1.0 + scale) + shift


def _dot(a, b):
    return jnp.dot(a, b, preferred_element_type=F32)


def _dot_t(a, b):
    return lax.dot_general(a, b, (((1,), (1,)), ((), ())), preferred_element_type=F32)


def _adaln_kernel(v_ref, w_ref, b_ref, o_ref):
    v = v_ref[...]
    s = v * jax.nn.sigmoid(v)
    o_ref[0] = jnp.dot(s, w_ref[0], preferred_element_type=F32,
                       precision=lax.Precision.HIGHEST) + b_ref[0]


def _adaln(vec8, mod_w, mod_b):
    depth, d, n = mod_w.shape
    tn = 1536
    return pl.pallas_call(
        _adaln_kernel,
        grid=(depth, n // tn),
        in_specs=[
            pl.BlockSpec((8, d), lambda l, j: (0, 0)),
            pl.BlockSpec((1, d, tn), lambda l, j: (l, 0, j)),
            pl.BlockSpec((1, 1, tn), lambda l, j: (l, 0, j)),
        ],
        out_specs=pl.BlockSpec((1, 8, tn), lambda l, j: (l, 0, j)),
        out_shape=jax.ShapeDtypeStruct((depth, 8, n), F32),
        compiler_params=_cparams(("arbitrary", "arbitrary")),
        name="adaln",
    )(vec8, mod_w, mod_b.reshape(depth, 1, n))


def _proj_kernel(x_ref, sh_ref, sc_ref, g_ref, w_ref, cos_ref, sin_ref, qg_ref, kg_ref,
                 *out_refs, with_q):
    h = _norm_mod(x_ref[0], g_ref[...], sh_ref[0], sc_ref[0]).astype(BF16)
    cosf = cos_ref[...]
    sinf = sin_ref[...]

    def rope_heads(p, n_heads, gain, scale, o_ref):
        for hd in range(n_heads):
            t = _rms(p[:, hd * HEAD_DIM_A:(hd + 1) * HEAD_DIM_A], gain)
            t = t * cosf + pltpu.roll(t, HEAD_DIM_A // 2, axis=1) * sinf
            o_ref[0, :, hd * HEAD_DIM_A:(hd + 1) * HEAD_DIM_A] = (t * scale).astype(BF16)

    col = 0
    if with_q:
        qa_ref, qb_ref, ka_ref, va_ref, kb_ref, vb_ref = out_refs
        rope_heads(_dot(h, w_ref[:, 0:QA_COLS]), HEADS_A, qg_ref[...], HEAD_DIM_A ** -0.5 * LOG2E,
                   qa_ref)
        qb = _dot(h, w_ref[:, QA_COLS:QA_COLS + B_COLS])
        qb_ref[0] = (qb * (HEAD_DIM_B ** -0.5 * LOG2E)).astype(BF16)
        col = QA_COLS + B_COLS
    else:
        ka_ref, va_ref, kb_ref, vb_ref = out_refs
    rope_heads(_dot(h, w_ref[:, col:col + KA_COLS]), KV_HEADS_A, kg_ref[...], 1.0, ka_ref)
    col += KA_COLS
    va_ref[0] = _dot(h, w_ref[:, col:col + KA_COLS]).astype(BF16)
    col += KA_COLS
    kb_ref[0] = _dot(h, w_ref[:, col:col + B_COLS]).astype(BF16)
    col += B_COLS
    vb_ref[0] = _dot(h, w_ref[:, col:col + B_COLS]).astype(BF16)


def _proj(x, shift, scale, gain, w, cosf, sinf, qg, kg, *, with_q, tm, name):
    b, n, d = x.shape
    ncol = w.shape[1]
    widths = ([QA_COLS, B_COLS] if with_q else []) + [KA_COLS, KA_COLS, B_COLS, B_COLS]
    vec = lambda bb, i: (bb, 0, 0)
    return pl.pallas_call(
        functools.partial(_proj_kernel, with_q=with_q),
        grid=(b, n // tm),
        in_specs=[
            pl.BlockSpec((1, tm, d), lambda bb, i: (bb, i, 0)),
            pl.BlockSpec((1, 1, d), vec),
            pl.BlockSpec((1, 1, d), vec),
            pl.BlockSpec((1, d), lambda bb, i: (0, 0)),
            pl.BlockSpec((d, ncol), lambda bb, i: (0, 0)),
            pl.BlockSpec((tm, HEAD_DIM_A), lambda bb, i: (i, 0)),
            pl.BlockSpec((tm, HEAD_DIM_A), lambda bb, i: (i, 0)),
            pl.BlockSpec((1, HEAD_DIM_A), lambda bb, i: (0, 0)),
            pl.BlockSpec((1, HEAD_DIM_A), lambda bb, i: (0, 0)),
        ],
        out_specs=[pl.BlockSpec((1, tm, wd), lambda bb, i: (bb, i, 0)) for wd in widths],
        out_shape=[jax.ShapeDtypeStruct((b, n, wd), BF16) for wd in widths],
        compiler_params=_cparams(("parallel", "parallel")),
        name=name,
    )(x, shift, scale, gain, w, cosf, sinf, qg, kg)


def _flash_update(q, k, v, m_ref, l_ref, acc_ref):
    d = v.shape[1]
    s = _dot_t(q, k)
    m_old = m_ref[...]
    m_new = jnp.maximum(m_old, jnp.max(s, axis=-1, keepdims=True))
    alpha = jnp.exp2(m_old - m_new)
    p = jnp.exp2(s - jnp.tile(m_new, (1, s.shape[1] // m_new.shape[1])))
    o = _dot(p.astype(BF16), jnp.concatenate([v, jnp.ones_like(v)], axis=1))
    acc_ref[...] = alpha * acc_ref[...] + o[:, :d]
    l_ref[...] = alpha * l_ref[...] + o[:, d:]
    m_ref[...] = m_new


def _gqa_kernel(q_ref, k_ref, v_ref, ck_ref, cv_ref, o_ref, m_ref, l_ref, acc_ref, *, tk):
    group = q_ref.shape[2] // HEAD_DIM_A
    m_ref[...] = jnp.full(m_ref.shape, NEG, F32)
    l_ref[...] = jnp.zeros(l_ref.shape, F32)
    acc_ref[...] = jnp.zeros(acc_ref.shape, F32)

    def update(k, v):
        for g in range(group):
            _flash_update(q_ref[0, :, g * HEAD_DIM_A:(g + 1) * HEAD_DIM_A], k, v,
                          m_ref.at[g], l_ref.at[g], acc_ref.at[g])

    def body(i, carry):
        off = pl.multiple_of(i * tk, tk)
        update(k_ref[0, pl.ds(off, tk), :], v_ref[0, pl.ds(off, tk), :])
        return carry

    lax.fori_loop(0, k_ref.shape[1] // tk, body, 0)
    update(ck_ref[0], cv_ref[0])
    o_ref[0] = jnp.concatenate([acc_ref[g] / l_ref[g] for g in range(group)], axis=1).astype(BF16)


def _gqa(qa, ka, va, cka, cva, *, tq, tk):
    b, n, _ = qa.shape
    c = cka.shape[1]
    group = HEADS_A // KV_HEADS_A
    gw = group * HEAD_DIM_A
    return pl.pallas_call(
        functools.partial(_gqa_kernel, tk=tk),
        grid=(b, KV_HEADS_A, n // tq),
        in_specs=[
            pl.BlockSpec((1, tq, gw), lambda bb, h, i: (bb, i, h)),
            pl.BlockSpec((1, n, HEAD_DIM_A), lambda bb, h, i: (bb, 0, h)),
            pl.BlockSpec((1, n, HEAD_DIM_A), lambda bb, h, i: (bb, 0, h)),
            pl.BlockSpec((1, c, HEAD_DIM_A), lambda bb, h, i: (bb, 0, h)),
            pl.BlockSpec((1, c, HEAD_DIM_A), lambda bb, h, i: (bb, 0, h)),
        ],
        out_specs=pl.BlockSpec((1, tq, gw), lambda bb, h, i: (bb, i, h)),
        out_shape=jax.ShapeDtypeStruct((b, n, QA_COLS), BF16),
        scratch_shapes=[
            pltpu.VMEM((group, tq, HEAD_DIM_A), F32),
            pltpu.VMEM((group, tq, HEAD_DIM_A), F32),
            pltpu.VMEM((group, tq, HEAD_DIM_A), F32),
        ],
        compiler_params=_cparams(("parallel", "parallel", "arbitrary")),
        name="gqa",
    )(qa, ka, va, cka, cva)


NA_UNROLL = 4


def _na_kernel(q_ref, k_ref, v_ref, ck_ref, cv_ref, bias_ref, o_ref):
    rows = q_ref.shape[1] // GRID_W
    win = NA_ROWS * GRID_W
    lane = lax.broadcasted_iota(jnp.int32, (GRID_W, 2 * HEAD_DIM_B), 1)
    lo = lane < HEAD_DIM_B
    pw = 2 * HEAD_DIM_B

    def with_ones(v):
        return jnp.concatenate([v, jnp.ones_like(v)], axis=1)

    def body(r, carry):
        r0 = jnp.clip(r - NA_ROWS // 2, 0, rows - NA_ROWS)
        q = q_ref[0, pl.ds(pl.multiple_of(r * GRID_W, GRID_W), GRID_W), :]
        zero = jnp.zeros_like(q)
        qq = jnp.concatenate([jnp.where(lo, q, zero), jnp.where(lo, zero, q)], axis=0)
        koff = pl.multiple_of(r0 * GRID_W, GRID_W)
        s = _dot_t(qq, k_ref[0, pl.ds(koff, win), :]) + bias_ref[r - r0]
        sc = _dot_t(qq, ck_ref[0])
        m = jnp.maximum(jnp.max(s, axis=-1, keepdims=True), jnp.max(sc, axis=-1, keepdims=True))
        p = jnp.exp2(s - m).astype(BF16)
        pc = jnp.exp2(sc - m).astype(BF16)
        o = _dot(p, with_ones(v_ref[0, pl.ds(koff, win), :])) + _dot(pc, with_ones(cv_ref[0]))
        o = o[:, :pw] / o[:, pw:]
        res = jnp.where(lo, o[:GRID_W], o[GRID_W:])
        o_ref[0, pl.ds(pl.multiple_of(r * GRID_W, GRID_W), GRID_W), :] = res.astype(BF16)
        return carry

    lax.fori_loop(0, rows, body, 0, unroll=NA_UNROLL)


def _na(qb, kb, vb, ckb, cvb, bias):
    b, n, _ = qb.shape
    c = ckb.shape[1]
    pw = 2 * HEAD_DIM_B
    n_cls = bias.shape[0]
    tok = lambda bb, h: (bb, 0, h)
    return pl.pallas_call(
        _na_kernel,
        grid=(b, HEADS_B // 2),
        in_specs=[
            pl.BlockSpec((1, n, pw), tok),
            pl.BlockSpec((1, n, pw), tok),
            pl.BlockSpec((1, n, pw), tok),
            pl.BlockSpec((1, c, pw), tok),
            pl.BlockSpec((1, c, pw), tok),
            pl.BlockSpec((n_cls, 2 * GRID_W, NA_ROWS * GRID_W), lambda bb, h: (0, h, 0)),
        ],
        out_specs=pl.BlockSpec((1, n, pw), tok),
        out_shape=jax.ShapeDtypeStruct((b, n, B_COLS), BF16),
        compiler_params=_cparams(("parallel", "parallel")),
        name="natten",
    )(qb, kb, vb, ckb, cvb, bias)


def _na_bias_table(rpb):
    edge = GRID_W - NA_COLS
    padded = jnp.pad(rpb, ((0, 0), (0, 0), (edge, edge)))
    e = jnp.stack([padded[:, :, GRID_W - 1 - c:2 * GRID_W - 1 - c] for c in range(GRID_W)], axis=2)
    t = jnp.stack([e[:, NA_ROWS - 1 - s:2 * NA_ROWS - 1 - s] for s in range(NA_ROWS)], axis=0)
    t = t.transpose(0, 1, 3, 2, 4)
    c = np.arange(GRID_W)[:, None, None]
    j = np.arange(GRID_W)[None, None, :]
    cs = np.clip(c - NA_COLS // 2, 0, edge)
    valid = (j >= cs) & (j < cs + NA_COLS)
    t = jnp.where(valid, t * LOG2E, NEG)
    return t.reshape(NA_ROWS, HEADS_B * GRID_W, NA_ROWS * GRID_W)


def _outproj_kernel(x_ref, oa_ref, ob_ref, w_ref, g1_ref, o_ref):
    ka = oa_ref.shape[2]
    y = _dot(oa_ref[0], w_ref[0:ka, :]) + _dot(ob_ref[0], w_ref[ka:, :])
    o_ref[0] = x_ref[0] + g1_ref[0] * y


def _outproj(x, oa, ob, w, g1, *, tm):
    b, n, d = x.shape
    row = lambda bb, i: (bb, i, 0)
    return pl.pallas_call(
        _outproj_kernel,
        grid=(b, n // tm),
        in_specs=[
            pl.BlockSpec((1, tm, d), row),
            pl.BlockSpec((1, tm, oa.shape[2]), row),
            pl.BlockSpec((1, tm, ob.shape[2]), row),
            pl.BlockSpec(w.shape, lambda bb, i: (0, 0)),
            pl.BlockSpec((1, 1, d), lambda bb, i: (bb, 0, 0)),
        ],
        out_specs=pl.BlockSpec((1, tm, d), row),
        out_shape=jax.ShapeDtypeStruct(x.shape, F32),
        compiler_params=_cparams(("parallel", "parallel")),
        name="attn_out",
    )(x, oa, ob, w, g1)


HALO = 8
FFN_CHUNK = 256


def _ffn_kernel(x_ref, xp_ref, xn_ref, sh_ref, sc_ref, g2_ref, ng_ref, wup_ref, cw_ref, cb_ref,
                wdn_ref, fg_ref, o_ref, u_ref, y_ref, *, tm, d_ff, final_norm):
    i = pl.program_id(1)
    ng, sh, sc = ng_ref[...], sh_ref[0], sc_ref[0]
    x = x_ref[0]
    d = x.shape[1]
    half = tm // 2
    slabs = FFN_CHUNK // LANES
    hp = jnp.where(i > 0, _norm_mod(xp_ref[0], ng, sh, sc), 0.0)
    hn = jnp.where(i < pl.num_programs(1) - 1, _norm_mod(xn_ref[0], ng, sh, sc), 0.0)
    h = jnp.concatenate([hp, _norm_mod(x, ng, sh, sc), hn], axis=0).astype(BF16)

    def project(buf, part, col):
        u = _dot(h, wup_ref[:, col:col + FFN_CHUNK])
        for s in range(slabs):
            u_ref[buf, part, s] = u[:, s * LANES:(s + 1) * LANES]

    def conv(buf, part, col):
        even, odd = [], []
        for s in range(slabs):
            tap = lambda k: u_ref[buf, part, s, pl.ds(HALO - 1 + k, half, stride=2), :]
            c = col + s * LANES
            w0, w1, w2 = (cw_ref[k:k + 1, c:c + LANES] for k in range(3))
            bias = cb_ref[:, c:c + LANES]
            t0, t1, t2, t3 = tap(0), tap(1), tap(2), tap(3)
            even.append(t0 * w0 + t1 * w1 + t2 * w2 + bias)
            odd.append(t1 * w0 + t2 * w1 + t3 * w2 + bias)
        return jnp.concatenate([jnp.concatenate(even, axis=1), jnp.concatenate(odd, axis=1)], axis=0)

    acc = jnp.zeros((tm, d), F32)
    for f in range(d_ff // FFN_CHUNK):
        c0 = f * FFN_CHUNK
        buf = f % 2
        project(buf, 0, c0)
        project(buf, 1, d_ff + c0)
        g = conv(buf, 0, c0)
        v = conv(buf, 1, d_ff + c0)
        a = (g * jax.nn.sigmoid(g) * v).astype(BF16)
        acc = acc + _dot(a, wdn_ref[c0:c0 + FFN_CHUNK, :])
    for s in range(d // LANES):
        y_ref[s, pl.ds(0, half, stride=2), :] = acc[:half, s * LANES:(s + 1) * LANES]
        y_ref[s, pl.ds(1, half, stride=2), :] = acc[half:, s * LANES:(s + 1) * LANES]
    ffn = jnp.concatenate([y_ref[s] for s in range(d // LANES)], axis=1)
    y = x + g2_ref[0] * ffn
    if final_norm:
        y = _rms(y, fg_ref[...])
    o_ref[0] = y


def _ffn(x, shift, scale, gate, norm_g, w_up, conv_w, conv_b, w_down, final_g, *, tm, final_norm):
    b, n, d = x.shape
    d_ff = w_down.shape[0]
    hb = tm // HALO
    last = n // HALO - 1
    vec = lambda bb, i: (bb, 0, 0)
    const2 = lambda bb, i: (0, 0)
    resident = dict(pipeline_mode=pl.Buffered(1))
    return pl.pallas_call(
        functools.partial(_ffn_kernel, tm=tm, d_ff=d_ff, final_norm=final_norm),
        grid=(b, n // tm),
        in_specs=[
            pl.BlockSpec((1, tm, d), lambda bb, i: (bb, i, 0)),
            pl.BlockSpec((1, HALO, d), lambda bb, i: (bb, jnp.maximum(i * hb - 1, 0), 0)),
            pl.BlockSpec((1, HALO, d), lambda bb, i: (bb, jnp.minimum((i + 1) * hb, last), 0)),
            pl.BlockSpec((1, 1, d), vec),
            pl.BlockSpec((1, 1, d), vec),
            pl.BlockSpec((1, 1, d), vec),
            pl.BlockSpec((1, d), const2),
            pl.BlockSpec(w_up.shape, const2, **resident),
            pl.BlockSpec(conv_w.shape, const2),
            pl.BlockSpec((1, 2 * d_ff), const2),
            pl.BlockSpec(w_down.shape, const2, **resident),
            pl.BlockSpec((1, d), const2),
        ],
        out_specs=pl.BlockSpec((1, tm, d), lambda bb, i: (bb, i, 0)),
        out_shape=jax.ShapeDtypeStruct(x.shape, F32),
        scratch_shapes=[
            pltpu.VMEM((2, 2, FFN_CHUNK // LANES, tm + 2 * HALO, LANES), F32),
            pltpu.VMEM((d // LANES, tm, LANES), F32),
        ],
        compiler_params=_cparams(("parallel", "parallel")),
        name="conv_ffn",
    )(x, x, x, shift, scale, gate, norm_g, w_up, conv_w, conv_b.reshape(1, -1), w_down, final_g)


F1_NC = 8
F2_NK = 16


def _dft_tables(n_tok, cg):
    w = GRID_W
    idx = np.arange(w, dtype=np.float64)
    a64 = 2.0 * np.pi * np.outer(idx, idx) / w
    fc, fs = np.cos(a64), np.sin(a64)
    step1 = np.block([[fc, fs], [-fs, fc]])
    eye = np.eye(F2_NK)
    step3 = (np.kron(fc, eye), np.kron(fs, eye))
    atw = 2.0 * np.pi * np.outer(idx, idx) / n_tok
    tw_c = np.repeat(np.cos(atw)[:, :, None], 128, axis=2)
    tw_s = np.repeat(np.sin(atw)[:, :, None], 128, axis=2)
    ch = np.arange(cg, dtype=np.float64)
    ach = 2.0 * np.pi * np.outer(ch, ch) / cg
    chan = np.concatenate([np.cos(ach), -np.sin(ach)], axis=1)
    f = lambda t: jnp.asarray(t, F32)
    return f(chan), f(step1), f(step3[0]), f(step3[1]), f(tw_c), f(tw_s)


def _f1_kernel(x_ref, sh_ref, sc_ref, g_ref, chan_ref, s1_ref, twc_ref, tws_ref, ar_ref, ai_ref,
               u_ref, *, nc, d):
    w = GRID_W
    cg = d // FOURIER_GROUPS
    tiles = d // LANES
    gt = cg // LANES
    x = x_ref[0].reshape(w * nc, d)
    h = _norm_mod(x, g_ref[...], sh_ref[0], sc_ref[0]).astype(BF16)
    for g in range(FOURIER_GROUPS):
        u = _dot(h[:, g * cg:(g + 1) * cg], chan_ref[...])
        for t in range(2 * gt):
            part, tt = divmod(t, gt)
            u_ref[part * tiles + g * gt + tt] = u[:, t * LANES:(t + 1) * LANES]
    for c in range(nc):
        rows = pl.ds(c, w, stride=nc)
        ur = jnp.concatenate([u_ref[s, rows, :] for s in range(tiles)], axis=1)
        ui = jnp.concatenate([u_ref[tiles + s, rows, :] for s in range(tiles)], axis=1)
        a = _dot(s1_ref[...], jnp.concatenate([ur, ui], axis=0).astype(BF16))
        a_r, a_i = a[:w], a[w:]
        tc = jnp.tile(twc_ref[c], (1, tiles))
        ts = jnp.tile(tws_ref[c], (1, tiles))
        ar_ref[0, c] = (a_r * tc + a_i * ts).astype(BF16)
        ai_ref[0, c] = (a_i * tc - a_r * ts).astype(BF16)


def _f2_kernel(ar_ref, ai_ref, x_ref, g1_ref, kc_ref, ks_ref, w_ref, o_ref, *, nk, d, inv_norm):
    w = GRID_W
    a_r = ar_ref[0].reshape(w * nk, d)
    a_i = ai_ref[0].reshape(w * nk, d)
    z = _dot(kc_ref[...], a_r) + _dot(ks_ref[...], a_i)
    y = _dot((z * inv_norm).astype(BF16), w_ref[...])
    o_ref[0] = x_ref[0] + g1_ref[0] * y.reshape(w, nk, d)


def _fourier(x, shift, scale, gate, norm_g, w_out):
    b, n, d = x.shape
    w = GRID_W
    assert n == w * w
    nc, nk = F1_NC, F2_NK
    cg = d // FOURIER_GROUPS
    chan, step1, kc, ks, tw_c, tw_s = _dft_tables(n, cg)
    chan, step1, kc, ks = (t.astype(BF16) for t in (chan, step1, kc, ks))
    x4 = x.reshape(b, w, w, d)
    vec = lambda bb, j: (bb, 0, 0)
    const2 = lambda bb, j: (0, 0)
    resident = dict(pipeline_mode=pl.Buffered(1))
    a_r, a_i = pl.pallas_call(
        functools.partial(_f1_kernel, nc=nc, d=d),
        grid=(b, w // nc),
        in_specs=[
            pl.BlockSpec((1, w, nc, d), lambda bb, j: (bb, 0, j, 0)),
            pl.BlockSpec((1, 1, d), vec),
            pl.BlockSpec((1, 1, d), vec),
            pl.BlockSpec((1, d), const2),
            pl.BlockSpec(chan.shape, const2),
            pl.BlockSpec(step1.shape, const2),
            pl.BlockSpec((nc, w, LANES), lambda bb, j: (j, 0, 0)),
            pl.BlockSpec((nc, w, LANES), lambda bb, j: (j, 0, 0)),
        ],
        out_specs=[pl.BlockSpec((1, nc, w, d), lambda bb, j: (bb, j, 0, 0))] * 2,
        out_shape=[jax.ShapeDtypeStruct((b, w, w, d), BF16)] * 2,
        scratch_shapes=[pltpu.VMEM((2 * d // LANES, w * nc, LANES), F32)],
        compiler_params=_cparams(("parallel", "parallel")),
        name="fourier_rows",
    )(x4, shift, scale, norm_g, chan, step1, tw_c, tw_s)
    out = pl.pallas_call(
        functools.partial(_f2_kernel, nk=nk, d=d, inv_norm=1.0 / float(np.sqrt(n * cg))),
        grid=(b, w // nk),
        in_specs=[
            pl.BlockSpec((1, w, nk, d), lambda bb, j: (bb, 0, j, 0)),
            pl.BlockSpec((1, w, nk, d), lambda bb, j: (bb, 0, j, 0)),
            pl.BlockSpec((1, w, nk, d), lambda bb, j: (bb, 0, j, 0)),
            pl.BlockSpec((1, 1, d), vec),
            pl.BlockSpec(kc.shape, const2, **resident),
            pl.BlockSpec(ks.shape, const2, **resident),
            pl.BlockSpec(w_out.shape, const2, **resident),
        ],
        out_specs=pl.BlockSpec((1, w, nk, d), lambda bb, j: (bb, 0, j, 0)),
        out_shape=jax.ShapeDtypeStruct((b, w, w, d), F32),
        compiler_params=_cparams(("parallel", "parallel")),
        name="fourier_cols",
    )(a_r, a_i, x4, gate, kc, ks, w_out)
    return out.reshape(b, n, d)


def _rope_tables(n):
    t = jnp.arange(n)
    row = (t // GRID_W).astype(F32)
    col = (t % GRID_W).astype(F32)
    pairs = HEAD_DIM_A // 4
    freqs = ROPE_THETA ** (-jnp.arange(pairs, dtype=F32) / pairs)
    ang = jnp.concatenate([row[:, None] * freqs, col[:, None] * freqs], axis=-1)
    cos, sin = jnp.cos(ang), jnp.sin(ang)
    return jnp.concatenate([cos, cos], axis=-1), jnp.concatenate([-sin, sin], axis=-1)


def kernel(x, c, ctx, c_ctx, mod_w, mod_b, norm1_g, norm2_g, attn_w_in, attn_w_out, q_norm_g,
           k_norm_g, na_rpb, fourier_w_out, ffn_w_up, ffn_conv_w, ffn_conv_b, ffn_w_down, final_g):
    b, n, d = x.shape
    depth = mod_w.shape[0]
    assert depth == 2 and b + 1 <= 8

    vec8 = jnp.zeros((8, d), F32).at[:b].set(c).at[b].set(c_ctx)
    mod = _adaln(vec8, mod_w, mod_b)
    term = lambda l, k: mod[l, :b, k * d:(k + 1) * d].reshape(b, 1, d)
    cterm = lambda l, k: jnp.broadcast_to(mod[l, b, k * d:(k + 1) * d].reshape(1, 1, d), (b, 1, d))

    perm = np.concatenate([np.arange(0, HEAD_DIM_A, 2), np.arange(1, HEAD_DIM_A, 2)])
    cols = np.arange(attn_w_in.shape[2])
    q_cols = QA_COLS + B_COLS
    for base, nh in ((0, HEADS_A), (q_cols, KV_HEADS_A)):
        for hd in range(nh):
            s0 = base + hd * HEAD_DIM_A
            cols[s0:s0 + HEAD_DIM_A] = s0 + perm
    w_in = attn_w_in[0][:, cols].astype(BF16)
    qg = q_norm_g[0][perm].reshape(1, -1)
    kg = k_norm_g[0][perm].reshape(1, -1)
    cosf, sinf = _rope_tables(n)
    n_ctx = ctx.shape[1]
    g1row = norm1_g[0].reshape(1, d)

    qa, qb, ka, va, kb, vb = _proj(x, term(0, 0), term(0, 1), g1row, w_in, cosf, sinf, qg, kg,
                                   with_q=True, tm=512, name="proj_latent")
    cka, cva, ckb, cvb = _proj(ctx, cterm(0, 0), cterm(0, 1), g1row, w_in[:, q_cols:],
                               jnp.ones((n_ctx, HEAD_DIM_A), F32), jnp.zeros((n_ctx, HEAD_DIM_A), F32),
                               qg, kg, with_q=False, tm=n_ctx, name="proj_context")
    oa = _gqa(qa, ka, va, cka, cva, tq=512, tk=512)
    ob = _na(qb, kb, vb, ckb, cvb, _na_bias_table(na_rpb[0]))
    x = _outproj(x, oa, ob, attn_w_out[0].astype(BF16), term(0, 2), tm=1024)
    x = _ffn(x, term(0, 3), term(0, 4), term(0, 5), norm2_g[0].reshape(1, d),
             ffn_w_up[0].astype(BF16), ffn_conv_w[0], ffn_conv_b[0], ffn_w_down[0].astype(BF16),
             final_g.reshape(1, d), tm=512, final_norm=False)

    x = _fourier(x, term(1, 0), term(1, 1), term(1, 2), norm1_g[1].reshape(1, d),
                 fourier_w_out[0].astype(BF16))
    x = _ffn(x, term(1, 3), term(1, 4), term(1, 5), norm2_g[1].reshape(1, d),
             ffn_w_up[1].astype(BF16), ffn_conv_w[1], ffn_conv_b[1], ffn_w_down[1].astype(BF16),
             final_g.reshape(1, d), tm=512, final_norm=True)
    return x
```

```python
import functools

import numpy as np
import jax
import jax.numpy as jnp
from jax import lax
from jax.experimental import pallas as pl
from jax.experimental.pallas import tpu as pltpu

F32 = jnp.float32
BF16 = jnp.bfloat16

GRID_W = 64
HEAD_DIM_A = 128
HEADS_A = 4
KV_HEADS_A = 2
HEAD_DIM_B = 64
HEADS_B = 8
NA_ROWS = 8
NA_COLS = 16
FOURIER_GROUPS = 4
ROPE_THETA = 10000.0
EPS = 1e-6
QA_COLS = HEADS_A * HEAD_DIM_A
KA_COLS = KV_HEADS_A * HEAD_DIM_A
B_COLS = HEADS_B * HEAD_DIM_B
NEG = -1e30
LOG2E = 1.4426950408889634
LANES = 128

VMEM_LIMIT = 56 * 1024 * 1024


def _cparams(sem):
    return pltpu.CompilerParams(dimension_semantics=sem, vmem_limit_bytes=VMEM_LIMIT)


def _rms(x, g):
    ms = jnp.mean(x * x, axis=-1, keepdims=True)
    return x * lax.rsqrt(ms + EPS) * g


def _norm_mod(x, g, shift, scale):
    return _rms(x, g) * (1.0 + scale) + shift


def _dot(a, b):
    return jnp.dot(a, b, preferred_element_type=F32)


def _dot_t(a, b):
    return lax.dot_general(a, b, (((1,), (1,)), ((), ())), preferred_element_type=F32)


def _adaln_kernel(v_ref, w_ref, b_ref, o_ref):
    v = v_ref[...]
    s = v * jax.nn.sigmoid(v)
    o_ref[0] = jnp.dot(s, w_ref[0], preferred_element_type=F32,
                       precision=lax.Precision.HIGHEST) + b_ref[0]


def _adaln(vec8, mod_w, mod_b):
    depth, d, n = mod_w.shape
    tn = 1536
    return pl.pallas_call(
        _adaln_kernel,
        grid=(depth, n // tn),
        in_specs=[
            pl.BlockSpec((8, d), lambda l, j: (0, 0)),
            pl.BlockSpec((1, d, tn), lambda l, j: (l, 0, j)),
            pl.BlockSpec((1, 1, tn), lambda l, j: (l, 0, j)),
        ],
        out_specs=pl.BlockSpec((1, 8, tn), lambda l, j: (l, 0, j)),
        out_shape=jax.ShapeDtypeStruct((depth, 8, n), F32),
        compiler_params=_cparams(("arbitrary", "arbitrary")),
        name="adaln",
    )(vec8, mod_w, mod_b.reshape(depth, 1, n))


def _proj_kernel(x_ref, sh_ref, sc_ref, g_ref, w_ref, cos_ref, sin_ref, qg_ref, kg_ref,
                 *out_refs, with_q):
    h = _norm_mod(x_ref[0], g_ref[...], sh_ref[0], sc_ref[0]).astype(BF16)
    cosf = cos_ref[...]
    sinf = sin_ref[...]

    def rope_heads(p, n_heads, gain, scale, o_ref):
        for hd in range(n_heads):
            t = _rms(p[:, hd * HEAD_DIM_A:(hd + 1) * HEAD_DIM_A], gain)
            t = t * cosf + pltpu.roll(t, HEAD_DIM_A // 2, axis=1) * sinf
            o_ref[0, :, hd * HEAD_DIM_A:(hd + 1) * HEAD_DIM_A] = (t * scale).astype(BF16)

    col = 0
    if with_q:
        qa_ref, qb_ref, ka_ref, va_ref, kb_ref, vb_ref = out_refs
        rope_heads(_dot(h, w_ref[:, 0:QA_COLS]), HEADS_A, qg_ref[...], HEAD_DIM_A ** -0.5 * LOG2E,
                   qa_ref)
        qb = _dot(h, w_ref[:, QA_COLS:QA_COLS + B_COLS])
        qb_ref[0] = (qb * (HEAD_DIM_B ** -0.5 * LOG2E)).astype(BF16)
        col = QA_COLS + B_COLS
    else:
        ka_ref, va_ref, kb_ref, vb_ref = out_refs
    rope_heads(_dot(h, w_ref[:, col:col + KA_COLS]), KV_HEADS_A, kg_ref[...], 1.0, ka_ref)
    col += KA_COLS
    va_ref[0] = _dot(h, w_ref[:, col:col + KA_COLS]).astype(BF16)
    col += KA_COLS
    kb_ref[0] = _dot(h, w_ref[:, col:col + B_COLS]).astype(BF16)
    col += B_COLS
    vb_ref[0] = _dot(h, w_ref[:, col:col + B_COLS]).astype(BF16)


def _proj(x, shift, scale, gain, w, cosf, sinf, qg, kg, *, with_q, tm, name):
    b, n, d = x.shape
    ncol = w.shape[1]
    widths = ([QA_COLS, B_COLS] if with_q else []) + [KA_COLS, KA_COLS, B_COLS, B_COLS]
    vec = lambda bb, i: (bb, 0, 0)
    return pl.pallas_call(
        functools.partial(_proj_kernel, with_q=with_q),
        grid=(b, n // tm),
        in_specs=[
            pl.BlockSpec((1, tm, d), lambda bb, i: (bb, i, 0)),
            pl.BlockSpec((1, 1, d), vec),
            pl.BlockSpec((1, 1, d), vec),
            pl.BlockSpec((1, d), lambda bb, i: (0, 0)),
            pl.BlockSpec((d, ncol), lambda bb, i: (0, 0)),
            pl.BlockSpec((tm, HEAD_DIM_A), lambda bb, i: (i, 0)),
            pl.BlockSpec((tm, HEAD_DIM_A), lambda bb, i: (i, 0)),
            pl.BlockSpec((1, HEAD_DIM_A), lambda bb, i: (0, 0)),
            pl.BlockSpec((1, HEAD_DIM_A), lambda bb, i: (0, 0)),
        ],
        out_specs=[pl.BlockSpec((1, tm, wd), lambda bb, i: (bb, i, 0)) for wd in widths],
        out_shape=[jax.ShapeDtypeStruct((b, n, wd), BF16) for wd in widths],
        compiler_params=_cparams(("parallel", "parallel")),
        name=name,
    )(x, shift, scale, gain, w, cosf, sinf, qg, kg)


GQA_UNROLL = 8


def _flash_update(q, k, v, m_ref, l_ref, acc_ref):
    d = v.shape[1]
    s = _dot_t(q, k)
    m_old = m_ref[...]
    m_new = jnp.maximum(m_old, jnp.max(s, axis=-1, keepdims=True))
    alpha = jnp.exp2(m_old - m_new)
    p = jnp.exp2(s - jnp.tile(m_new, (1, s.shape[1] // m_new.shape[1])))
    o = _dot(p.astype(BF16), jnp.concatenate([v, jnp.ones_like(v)], axis=1))
    acc_ref[...] = alpha * acc_ref[...] + o[:, :d]
    l_ref[...] = alpha * l_ref[...] + o[:, d:]
    m_ref[...] = m_new


def _gqa_kernel(q_ref, k_ref, v_ref, ck_ref, cv_ref, o_ref, m_ref, l_ref, acc_ref, *, tk):
    group = q_ref.shape[2] // HEAD_DIM_A
    m_ref[...] = jnp.full(m_ref.shape, NEG, F32)
    l_ref[...] = jnp.zeros(l_ref.shape, F32)
    acc_ref[...] = jnp.zeros(acc_ref.shape, F32)

    def update(k, v):
        for g in range(group):
            _flash_update(q_ref[0, :, g * HEAD_DIM_A:(g + 1) * HEAD_DIM_A], k, v,
                          m_ref.at[g], l_ref.at[g], acc_ref.at[g])

    def body(i, carry):
        off = pl.multiple_of(i * tk, tk)
        update(k_ref[0, pl.ds(off, tk), :], v_ref[0, pl.ds(off, tk), :])
        return carry

    lax.fori_loop(0, k_ref.shape[1] // tk, body, 0, unroll=GQA_UNROLL)
    update(ck_ref[0], cv_ref[0])
    o_ref[0] = jnp.concatenate([acc_ref[g] / l_ref[g] for g in range(group)], axis=1).astype(BF16)


def _gqa(qa, ka, va, cka, cva, *, tq, tk):
    b, n, _ = qa.shape
    c = cka.shape[1]
    group = HEADS_A // KV_HEADS_A
    gw = group * HEAD_DIM_A
    return pl.pallas_call(
        functools.partial(_gqa_kernel, tk=tk),
        grid=(b, KV_HEADS_A, n // tq),
        in_specs=[
            pl.BlockSpec((1, tq, gw), lambda bb, h, i: (bb, i, h)),
            pl.BlockSpec((1, n, HEAD_DIM_A), lambda bb, h, i: (bb, 0, h)),
            pl.BlockSpec((1, n, HEAD_DIM_A), lambda bb, h, i: (bb, 0, h)),
            pl.BlockSpec((1, c, HEAD_DIM_A), lambda bb, h, i: (bb, 0, h)),
            pl.BlockSpec((1, c, HEAD_DIM_A), lambda bb, h, i: (bb, 0, h)),
        ],
        out_specs=pl.BlockSpec((1, tq, gw), lambda bb, h, i: (bb, i, h)),
        out_shape=jax.ShapeDtypeStruct((b, n, QA_COLS), BF16),
        scratch_shapes=[
            pltpu.VMEM((group, tq, HEAD_DIM_A), F32),
            pltpu.VMEM((group, tq, HEAD_DIM_A), F32),
            pltpu.VMEM((group, tq, HEAD_DIM_A), F32),
        ],
        compiler_params=_cparams(("parallel", "parallel", "arbitrary")),
        name="gqa",
    )(qa, ka, va, cka, cva)


NA_UNROLL = 16


def _na_kernel(q_ref, k_ref, v_ref, ck_ref, cv_ref, bias_ref, o_ref):
    rows = q_ref.shape[1] // GRID_W
    win = NA_ROWS * GRID_W
    lane = lax.broadcasted_iota(jnp.int32, (GRID_W, 2 * HEAD_DIM_B), 1)
    lo = lane < HEAD_DIM_B
    pw = 2 * HEAD_DIM_B

    def with_ones(v):
        return jnp.concatenate([v, jnp.ones_like(v)], axis=1)

    def body(r, carry):
        r0 = jnp.clip(r - NA_ROWS // 2, 0, rows - NA_ROWS)
        q = q_ref[0, pl.ds(pl.multiple_of(r * GRID_W, GRID_W), GRID_W), :]
        zero = jnp.zeros_like(q)
        qq = jnp.concatenate([jnp.where(lo, q, zero), jnp.where(lo, zero, q)], axis=0)
        koff = pl.multiple_of(r0 * GRID_W, GRID_W)
        s = _dot_t(qq, k_ref[0, pl.ds(koff, win), :]) + bias_ref[r - r0]
        sc = _dot_t(qq, ck_ref[0])
        m = jnp.maximum(jnp.max(s, axis=-1, keepdims=True), jnp.max(sc, axis=-1, keepdims=True))
        p = jnp.exp2(s - m).astype(BF16)
        pc = jnp.exp2(sc - m).astype(BF16)
        o = _dot(p, with_ones(v_ref[0, pl.ds(koff, win), :])) + _dot(pc, with_ones(cv_ref[0]))
        o = o[:, :pw] / o[:, pw:]
        res = jnp.where(lo, o[:GRID_W], o[GRID_W:])
        o_ref[0, pl.ds(pl.multiple_of(r * GRID_W, GRID_W), GRID_W), :] = res.astype(BF16)
        return carry

    lax.fori_loop(0, rows, body, 0, unroll=NA_UNROLL)


def _na(qb, kb, vb, ckb, cvb, bias):
    b, n, _ = qb.shape
    c = ckb.shape[1]
    pw = 2 * HEAD_DIM_B
    n_cls = bias.shape[0]
    tok = lambda bb, h: (bb, 0, h)
    return pl.pallas_call(
        _na_kernel,
        grid=(b, HEADS_B // 2),
        in_specs=[
            pl.BlockSpec((1, n, pw), tok),
            pl.BlockSpec((1, n, pw), tok),
            pl.BlockSpec((1, n, pw), tok),
            pl.BlockSpec((1, c, pw), tok),
            pl.BlockSpec((1, c, pw), tok),
            pl.BlockSpec((n_cls, 2 * GRID_W, NA_ROWS * GRID_W), lambda bb, h: (0, h, 0)),
        ],
        out_specs=pl.BlockSpec((1, n, pw), tok),
        out_shape=jax.ShapeDtypeStruct((b, n, B_COLS), BF16),
        compiler_params=_cparams(("parallel", "parallel")),
        name="natten",
    )(qb, kb, vb, ckb, cvb, bias)


def _na_bias_table(rpb):
    edge = GRID_W - NA_COLS
    padded = jnp.pad(rpb, ((0, 0), (0, 0), (edge, edge)))
    e = jnp.stack([padded[:, :, GRID_W - 1 - c:2 * GRID_W - 1 - c] for c in range(GRID_W)], axis=2)
    t = jnp.stack([e[:, NA_ROWS - 1 - s:2 * NA_ROWS - 1 - s] for s in range(NA_ROWS)], axis=0)
    t = t.transpose(0, 1, 3, 2, 4)
    c = np.arange(GRID_W)[:, None, None]
    j = np.arange(GRID_W)[None, None, :]
    cs = np.clip(c - NA_COLS // 2, 0, edge)
    valid = (j >= cs) & (j < cs + NA_COLS)
    t = jnp.where(valid, t * LOG2E, NEG)
    return t.reshape(NA_ROWS, HEADS_B * GRID_W, NA_ROWS * GRID_W)


def _outproj_kernel(x_ref, oa_ref, ob_ref, w_ref, g1_ref, o_ref):
    ka = oa_ref.shape[2]
    y = _dot(oa_ref[0], w_ref[0:ka, :]) + _dot(ob_ref[0], w_ref[ka:, :])
    o_ref[0] = x_ref[0] + g1_ref[0] * y


def _outproj(x, oa, ob, w, g1, *, tm):
    b, n, d = x.shape
    row = lambda bb, i: (bb, i, 0)
    return pl.pallas_call(
        _outproj_kernel,
        grid=(b, n // tm),
        in_specs=[
            pl.BlockSpec((1, tm, d), row),
            pl.BlockSpec((1, tm, oa.shape[2]), row),
            pl.BlockSpec((1, tm, ob.shape[2]), row),
            pl.BlockSpec(w.shape, lambda bb, i: (0, 0)),
            pl.BlockSpec((1, 1, d), lambda bb, i: (bb, 0, 0)),
        ],
        out_specs=pl.BlockSpec((1, tm, d), row),
        out_shape=jax.ShapeDtypeStruct(x.shape, F32),
        compiler_params=_cparams(("parallel", "parallel")),
        name="attn_out",
    )(x, oa, ob, w, g1)


HALO = 8
FFN_CHUNK = 256
FFN_TM = 512


def _ffn_kernel(x_ref, xp_ref, xn_ref, sh_ref, sc_ref, g2_ref, ng_ref, wup_ref, cw_ref, cb_ref,
                wdn_ref, fg_ref, o_ref, u_ref, y_ref, *, tm, d_ff, final_norm):
    i = pl.program_id(1)
    ng, sh, sc = ng_ref[...], sh_ref[0], sc_ref[0]
    x = x_ref[0]
    d = x.shape[1]
    half = tm // 2
    slabs = FFN_CHUNK // LANES
    hp = jnp.where(i > 0, _norm_mod(xp_ref[0], ng, sh, sc), 0.0)
    hn = jnp.where(i < pl.num_programs(1) - 1, _norm_mod(xn_ref[0], ng, sh, sc), 0.0)
    h = jnp.concatenate([hp, _norm_mod(x, ng, sh, sc), hn], axis=0).astype(BF16)

    def project(buf, part, col):
        u = _dot(h, wup_ref[:, col:col + FFN_CHUNK])
        for s in range(slabs):
            u_ref[buf, part, s] = u[:, s * LANES:(s + 1) * LANES]

    def conv(buf, part, col):
        even, odd = [], []
        for s in range(slabs):
            tap = lambda k: u_ref[buf, part, s, pl.ds(HALO - 1 + k, half, stride=2), :]
            c = col + s * LANES
            w0, w1, w2 = (cw_ref[k:k + 1, c:c + LANES] for k in range(3))
            bias = cb_ref[:, c:c + LANES]
            t0, t1, t2, t3 = tap(0), tap(1), tap(2), tap(3)
            even.append(t0 * w0 + t1 * w1 + t2 * w2 + bias)
            odd.append(t1 * w0 + t2 * w1 + t3 * w2 + bias)
        return jnp.concatenate([jnp.concatenate(even, axis=1), jnp.concatenate(odd, axis=1)], axis=0)

    acc = jnp.zeros((tm, d), F32)
    n_chunks = d_ff // FFN_CHUNK
    project(0, 0, 0)
    project(0, 1, d_ff)
    for f in range(n_chunks):
        c0 = f * FFN_CHUNK
        buf = f % 2
        if f + 1 < n_chunks:
            project(1 - buf, 0, c0 + FFN_CHUNK)
            project(1 - buf, 1, d_ff + c0 + FFN_CHUNK)
        g = conv(buf, 0, c0)
        v = conv(buf, 1, d_ff + c0)
        a = (g * jax.nn.sigmoid(g) * v).astype(BF16)
        acc = acc + _dot(a, wdn_ref[c0:c0 + FFN_CHUNK, :])
    for s in range(d // LANES):
        y_ref[s, pl.ds(0, half, stride=2), :] = acc[:half, s * LANES:(s + 1) * LANES]
        y_ref[s, pl.ds(1, half, stride=2), :] = acc[half:, s * LANES:(s + 1) * LANES]
    ffn = jnp.concatenate([y_ref[s] for s in range(d // LANES)], axis=1)
    y = x + g2_ref[0] * ffn
    if final_norm:
        y = _rms(y, fg_ref[...])
    o_ref[0] = y


def _ffn(x, shift, scale, gate, norm_g, w_up, conv_w, conv_b, w_down, final_g, *, tm, final_norm):
    b, n, d = x.shape
    d_ff = w_down.shape[0]
    hb = tm // HALO
    last = n // HALO - 1
    vec = lambda bb, i: (bb, 0, 0)
    const2 = lambda bb, i: (0, 0)
    resident = dict(pipeline_mode=pl.Buffered(1))
    return pl.pallas_call(
        functools.partial(_ffn_kernel, tm=tm, d_ff=d_ff, final_norm=final_norm),
        grid=(b, n // tm),
        in_specs=[
            pl.BlockSpec((1, tm, d), lambda bb, i: (bb, i, 0)),
            pl.BlockSpec((1, HALO, d), lambda bb, i: (bb, jnp.maximum(i * hb - 1, 0), 0)),
            pl.BlockSpec((1, HALO, d), lambda bb, i: (bb, jnp.minimum((i + 1) * hb, last), 0)),
            pl.BlockSpec((1, 1, d), vec),
            pl.BlockSpec((1, 1, d), vec),
            pl.BlockSpec((1, 1, d), vec),
            pl.BlockSpec((1, d), const2),
            pl.BlockSpec(w_up.shape, const2, **resident),
            pl.BlockSpec(conv_w.shape, const2),
            pl.BlockSpec((1, 2 * d_ff), const2),
            pl.BlockSpec(w_down.shape, const2, **resident),
            pl.BlockSpec((1, d), const2),
        ],
        out_specs=pl.BlockSpec((1, tm, d), lambda bb, i: (bb, i, 0)),
        out_shape=jax.ShapeDtypeStruct(x.shape, F32),
        scratch_shapes=[
            pltpu.VMEM((2, 2, FFN_CHUNK // LANES, tm + 2 * HALO, LANES), F32),
            pltpu.VMEM((d // LANES, tm, LANES), F32),
        ],
        compiler_params=_cparams(("parallel", "parallel")),
        name="conv_ffn",
    )(x, x, x, shift, scale, gate, norm_g, w_up, conv_w, conv_b.reshape(1, -1), w_down, final_g)


F1_NC = 8
F2_NK = 16


def _dft_tables(n_tok, cg):
    w = GRID_W
    idx = np.arange(w, dtype=np.float64)
    a64 = 2.0 * np.pi * np.outer(idx, idx) / w
    fc, fs = np.cos(a64), np.sin(a64)
    step1 = np.block([[fc, fs], [-fs, fc]])
    eye = np.eye(F2_NK)
    step3 = (np.kron(fc, eye), np.kron(fs, eye))
    atw = 2.0 * np.pi * np.outer(idx, idx) / n_tok
    tw_c = np.repeat(np.cos(atw)[:, :, None], 128, axis=2)
    tw_s = np.repeat(np.sin(atw)[:, :, None], 128, axis=2)
    ch = np.arange(cg, dtype=np.float64)
    ach = 2.0 * np.pi * np.outer(ch, ch) / cg
    chan = np.concatenate([np.cos(ach), -np.sin(ach)], axis=1)
    f = lambda t: jnp.asarray(t, F32)
    return f(chan), f(step1), f(step3[0]), f(step3[1]), f(tw_c), f(tw_s)


def _f1_kernel(x_ref, sh_ref, sc_ref, g_ref, chan_ref, s1_ref, twc_ref, tws_ref, ar_ref, ai_ref,
               u_ref, *, nc, d):
    w = GRID_W
    cg = d // FOURIER_GROUPS
    tiles = d // LANES
    gt = cg // LANES
    x = x_ref[0].reshape(w * nc, d)
    h = _norm_mod(x, g_ref[...], sh_ref[0], sc_ref[0]).astype(BF16)
    for g in range(FOURIER_GROUPS):
        u = _dot(h[:, g * cg:(g + 1) * cg], chan_ref[...])
        for t in range(2 * gt):
            part, tt = divmod(t, gt)
            u_ref[part * tiles + g * gt + tt] = u[:, t * LANES:(t + 1) * LANES]
    for c in range(nc):
        rows = pl.ds(c, w, stride=nc)
        ur = jnp.concatenate([u_ref[s, rows, :] for s in range(tiles)], axis=1)
        ui = jnp.concatenate([u_ref[tiles + s, rows, :] for s in range(tiles)], axis=1)
        a = _dot(s1_ref[...], jnp.concatenate([ur, ui], axis=0).astype(BF16))
        a_r, a_i = a[:w], a[w:]
        tc = jnp.tile(twc_ref[c], (1, tiles))
        ts = jnp.tile(tws_ref[c], (1, tiles))
        ar_ref[0, c] = (a_r * tc + a_i * ts).astype(BF16)
        ai_ref[0, c] = (a_i * tc - a_r * ts).astype(BF16)


def _f2_kernel(ar_ref, ai_ref, x_ref, g1_ref, kc_ref, ks_ref, w_ref, o_ref, *, nk, d, inv_norm):
    w = GRID_W
    a_r = ar_ref[0].reshape(w * nk, d)
    a_i = ai_ref[0].reshape(w * nk, d)
    z = _dot(kc_ref[...], a_r) + _dot(ks_ref[...], a_i)
    y = _dot((z * inv_norm).astype(BF16), w_ref[...])
    o_ref[0] = x_ref[0] + g1_ref[0] * y.reshape(w, nk, d)


def _fourier(x, shift, scale, gate, norm_g, w_out):
    b, n, d = x.shape
    w = GRID_W
    assert n == w * w
    nc, nk = F1_NC, F2_NK
    cg = d // FOURIER_GROUPS
    chan, step1, kc, ks, tw_c, tw_s = _dft_tables(n, cg)
    chan, step1, kc, ks = (t.astype(BF16) for t in (chan, step1, kc, ks))
    x4 = x.reshape(b, w, w, d)
    vec = lambda bb, j: (bb, 0, 0)
    const2 = lambda bb, j: (0, 0)
    resident = dict(pipeline_mode=pl.Buffered(1))
    a_r, a_i = pl.pallas_call(
        functools.partial(_f1_kernel, nc=nc, d=d),
        grid=(b, w // nc),
        in_specs=[
            pl.BlockSpec((1, w, nc, d), lambda bb, j: (bb, 0, j, 0)),
            pl.BlockSpec((1, 1, d), vec),
            pl.BlockSpec((1, 1, d), vec),
            pl.BlockSpec((1, d), const2),
            pl.BlockSpec(chan.shape, const2),
            pl.BlockSpec(step1.shape, const2),
            pl.BlockSpec((nc, w, LANES), lambda bb, j: (j, 0, 0)),
            pl.BlockSpec((nc, w, LANES), lambda bb, j: (j, 0, 0)),
        ],
        out_specs=[pl.BlockSpec((1, nc, w, d), lambda bb, j: (bb, j, 0, 0))] * 2,
        out_shape=[jax.ShapeDtypeStruct((b, w, w, d), BF16)] * 2,
        scratch_shapes=[pltpu.VMEM((2 * d // LANES, w * nc, LANES), F32)],
        compiler_params=_cparams(("parallel", "parallel")),
        name="fourier_rows",
    )(x4, shift, scale, norm_g, chan, step1, tw_c, tw_s)
    out = pl.pallas_call(
        functools.partial(_f2_kernel, nk=nk, d=d, inv_norm=1.0 / float(np.sqrt(n * cg))),
        grid=(b, w // nk),
        in_specs=[
            pl.BlockSpec((1, w, nk, d), lambda bb, j: (bb, 0, j, 0)),
            pl.BlockSpec((1, w, nk, d), lambda bb, j: (bb, 0, j, 0)),
            pl.BlockSpec((1, w, nk, d), lambda bb, j: (bb, 0, j, 0)),
            pl.BlockSpec((1, 1, d), vec),
            pl.BlockSpec(kc.shape, const2, **resident),
            pl.BlockSpec(ks.shape, const2, **resident),
            pl.BlockSpec(w_out.shape, const2, **resident),
        ],
        out_specs=pl.BlockSpec((1, w, nk, d), lambda bb, j: (bb, 0, j, 0)),
        out_shape=jax.ShapeDtypeStruct((b, w, w, d), F32),
        compiler_params=_cparams(("parallel", "parallel")),
        name="fourier_cols",
    )(a_r, a_i, x4, gate, kc, ks, w_out)
    return out.reshape(b, n, d)


def _rope_tables(n):
    t = jnp.arange(n)
    row = (t // GRID_W).astype(F32)
    col = (t % GRID_W).astype(F32)
    pairs = HEAD_DIM_A // 4
    freqs = ROPE_THETA ** (-jnp.arange(pairs, dtype=F32) / pairs)
    ang = jnp.concatenate([row[:, None] * freqs, col[:, None] * freqs], axis=-1)
    cos, sin = jnp.cos(ang), jnp.sin(ang)
    return jnp.concatenate([cos, cos], axis=-1), jnp.concatenate([-sin, sin], axis=-1)


def kernel(x, c, ctx, c_ctx, mod_w, mod_b, norm1_g, norm2_g, attn_w_in, attn_w_out, q_norm_g,
           k_norm_g, na_rpb, fourier_w_out, ffn_w_up, ffn_conv_w, ffn_conv_b, ffn_w_down, final_g):
    b, n, d = x.shape
    depth = mod_w.shape[0]
    assert depth == 2 and b + 1 <= 8

    vec8 = jnp.zeros((8, d), F32).at[:b].set(c).at[b].set(c_ctx)
    mod = _adaln(vec8, mod_w, mod_b)
    term = lambda l, k: mod[l, :b, k * d:(k + 1) * d].reshape(b, 1, d)
    cterm = lambda l, k: jnp.broadcast_to(mod[l, b, k * d:(k + 1) * d].reshape(1, 1, d), (b, 1, d))

    perm = np.concatenate([np.arange(0, HEAD_DIM_A, 2), np.arange(1, HEAD_DIM_A, 2)])
    cols = np.arange(attn_w_in.shape[2])
    q_cols = QA_COLS + B_COLS
    for base, nh in ((0, HEADS_A), (q_cols, KV_HEADS_A)):
        for hd in range(nh):
            s0 = base + hd * HEAD_DIM_A
            cols[s0:s0 + HEAD_DIM_A] = s0 + perm
    w_in = attn_w_in[0][:, cols].astype(BF16)
    qg = q_norm_g[0][perm].reshape(1, -1)
    kg = k_norm_g[0][perm].reshape(1, -1)
    cosf, sinf = _rope_tables(n)
    n_ctx = ctx.shape[1]
    g1row = norm1_g[0].reshape(1, d)

    qa, qb, ka, va, kb, vb = _proj(x, term(0, 0), term(0, 1), g1row, w_in, cosf, sinf, qg, kg,
                                   with_q=True, tm=512, name="proj_latent")
    cka, cva, ckb, cvb = _proj(ctx, cterm(0, 0), cterm(0, 1), g1row, w_in[:, q_cols:],
                               jnp.ones((n_ctx, HEAD_DIM_A), F32), jnp.zeros((n_ctx, HEAD_DIM_A), F32),
                               qg, kg, with_q=False, tm=n_ctx, name="proj_context")
    oa = _gqa(qa, ka, va, cka, cva, tq=512, tk=512)
    ob = _na(qb, kb, vb, ckb, cvb, _na_bias_table(na_rpb[0]))
    x = _outproj(x, oa, ob, attn_w_out[0].astype(BF16), term(0, 2), tm=1024)
    x = _ffn(x, term(0, 3), term(0, 4), term(0, 5), norm2_g[0].reshape(1, d),
             ffn_w_up[0].astype(BF16), ffn_conv_w[0], ffn_conv_b[0], ffn_w_down[0].astype(BF16),
             final_g.reshape(1, d), tm=FFN_TM, final_norm=False)

    x = _fourier(x, term(1, 0), term(1, 1), term(1, 2), norm1_g[1].reshape(1, d),
                 fourier_w_out[0].astype(BF16))
    x = _ffn(x, term(1, 3), term(1, 4), term(1, 5), norm2_g[1].reshape(1, d),
             ffn_w_up[1].astype(BF16), ffn_conv_w[1], ffn_conv_b[1], ffn_w_down[1].astype(BF16),
             final_g.reshape(1, d), tm=FFN_TM, final_norm=True)
    return x
```

```python
import functools

import numpy as np
import jax
import jax.numpy as jnp
from jax import lax
from jax.experimental import pallas as pl
from jax.experimental.pallas import tpu as pltpu

F32 = jnp.float32
BF16 = jnp.bfloat16

GRID_W = 64
HEAD_DIM_A = 128
HEADS_A = 4
KV_HEADS_A = 2
HEAD_DIM_B = 64
HEADS_B = 8
NA_ROWS = 8
NA_COLS = 16
FOURIER_GROUPS = 4
ROPE_THETA = 10000.0
EPS = 1e-6
QA_COLS = HEADS_A * HEAD_DIM_A
KA_COLS = KV_HEADS_A * HEAD_DIM_A
B_COLS = HEADS_B * HEAD_DIM_B
NEG = -1e30
LOG2E = 1.4426950408889634
LANES = 128

VMEM_LIMIT = 56 * 1024 * 1024


def _cparams(sem):
    return pltpu.CompilerParams(dimension_semantics=sem, vmem_limit_bytes=VMEM_LIMIT)


def _rms(x, g):
    ms = jnp.mean(x * x, axis=-1, keepdims=True)
    return x * lax.rsqrt(ms + EPS) * g


def _norm_mod(x, g, shift, scale):
    return _rms(x, g) * (1.0 + scale) + shift


def _dot(a, b):
    return jnp.dot(a, b, preferred_element_type=F32)


def _dot_t(a, b):
    return lax.dot_general(a, b, (((1,), (1,)), ((), ())), preferred_element_type=F32)


def _adaln_kernel(v_ref, w_ref, b_ref, o_ref):
    v = v_ref[...]
    s = v * jax.nn.sigmoid(v)
    o_ref[0] = jnp.dot(s, w_ref[0], preferred_element_type=F32,
                       precision=lax.Precision.HIGHEST) + b_ref[0]


def _adaln(vec8, mod_w, mod_b):
    depth, d, n = mod_w.shape
    tn = 1536
    return pl.pallas_call(
        _adaln_kernel,
        grid=(depth, n // tn),
        in_specs=[
            pl.BlockSpec((8, d), lambda l, j: (0, 0)),
            pl.BlockSpec((1, d, tn), lambda l, j: (l, 0, j)),
            pl.BlockSpec((1, 1, tn), lambda l, j: (l, 0, j)),
        ],
        out_specs=pl.BlockSpec((1, 8, tn), lambda l, j: (l, 0, j)),
        out_shape=jax.ShapeDtypeStruct((depth, 8, n), F32),
        compiler_params=_cparams(("arbitrary", "arbitrary")),
        name="adaln",
    )(vec8, mod_w, mod_b.reshape(depth, 1, n))


def _proj_kernel(x_ref, sh_ref, sc_ref, g_ref, w_ref, cos_ref, sin_ref, qg_ref, kg_ref,
                 *out_refs, with_q):
    h = _norm_mod(x_ref[0], g_ref[...], sh_ref[0], sc_ref[0]).astype(BF16)
    cosf = cos_ref[...]
    sinf = sin_ref[...]

    def rope_heads(p, n_heads, gain, scale, o_ref):
        for hd in range(n_heads):
            t = _rms(p[:, hd * HEAD_DIM_A:(hd + 1) * HEAD_DIM_A], gain)
            t = t * cosf + pltpu.roll(t, HEAD_DIM_A // 2, axis=1) * sinf
            o_ref[0, :, hd * HEAD_DIM_A:(hd + 1) * HEAD_DIM_A] = (t * scale).astype(BF16)

    col = 0
    if with_q:
        qa_ref, qb_ref, ka_ref, va_ref, kb_ref, vb_ref = out_refs
        rope_heads(_dot(h, w_ref[:, 0:QA_COLS]), HEADS_A, qg_ref[...], HEAD_DIM_A ** -0.5 * LOG2E,
                   qa_ref)
        qb = _dot(h, w_ref[:, QA_COLS:QA_COLS + B_COLS])
        qb_ref[0] = (qb * (HEAD_DIM_B ** -0.5 * LOG2E)).astype(BF16)
        col = QA_COLS + B_COLS
    else:
        ka_ref, va_ref, kb_ref, vb_ref = out_refs
    rope_heads(_dot(h, w_ref[:, col:col + KA_COLS]), KV_HEADS_A, kg_ref[...], 1.0, ka_ref)
    col += KA_COLS
    va_ref[0] = _dot(h, w_ref[:, col:col + KA_COLS]).astype(BF16)
    col += KA_COLS
    kb_ref[0] = _dot(h, w_ref[:, col:col + B_COLS]).astype(BF16)
    col += B_COLS
    vb_ref[0] = _dot(h, w_ref[:, col:col + B_COLS]).astype(BF16)


def _proj(x, shift, scale, gain, w, cosf, sinf, qg, kg, *, with_q, tm, name):
    b, n, d = x.shape
    ncol = w.shape[1]
    widths = ([QA_COLS, B_COLS] if with_q else []) + [KA_COLS, KA_COLS, B_COLS, B_COLS]
    vec = lambda bb, i: (bb, 0, 0)
    return pl.pallas_call(
        functools.partial(_proj_kernel, with_q=with_q),
        grid=(b, n // tm),
        in_specs=[
            pl.BlockSpec((1, tm, d), lambda bb, i: (bb, i, 0)),
            pl.BlockSpec((1, 1, d), vec),
            pl.BlockSpec((1, 1, d), vec),
            pl.BlockSpec((1, d), lambda bb, i: (0, 0)),
            pl.BlockSpec((d, ncol), lambda bb, i: (0, 0)),
            pl.BlockSpec((tm, HEAD_DIM_A), lambda bb, i: (i, 0)),
            pl.BlockSpec((tm, HEAD_DIM_A), lambda bb, i: (i, 0)),
            pl.BlockSpec((1, HEAD_DIM_A), lambda bb, i: (0, 0)),
            pl.BlockSpec((1, HEAD_DIM_A), lambda bb, i: (0, 0)),
        ],
        out_specs=[pl.BlockSpec((1, tm, wd), lambda bb, i: (bb, i, 0)) for wd in widths],
        out_shape=[jax.ShapeDtypeStruct((b, n, wd), BF16) for wd in widths],
        compiler_params=_cparams(("parallel", "parallel")),
        name=name,
    )(x, shift, scale, gain, w, cosf, sinf, qg, kg)


GQA_UNROLL = 8


def _flash_update(q, k, v, m_ref, l_ref, acc_ref):
    d = v.shape[1]
    s = _dot_t(q, k)
    m_old = m_ref[...]
    m_new = jnp.maximum(m_old, jnp.max(s, axis=-1, keepdims=True))
    alpha = jnp.exp2(m_old - m_new)
    p = jnp.exp2(s - jnp.tile(m_new, (1, s.shape[1] // m_new.shape[1])))
    o = _dot(p.astype(BF16), jnp.concatenate([v, jnp.ones_like(v)], axis=1))
    acc_ref[...] = alpha * acc_ref[...] + o[:, :d]
    l_ref[...] = alpha * l_ref[...] + o[:, d:]
    m_ref[...] = m_new


def _gqa_kernel(q_ref, k_ref, v_ref, ck_ref, cv_ref, o_ref, m_ref, l_ref, acc_ref, *, tk):
    group = q_ref.shape[2] // HEAD_DIM_A
    m_ref[...] = jnp.full(m_ref.shape, NEG, F32)
    l_ref[...] = jnp.zeros(l_ref.shape, F32)
    acc_ref[...] = jnp.zeros(acc_ref.shape, F32)

    def update(k, v):
        for g in range(group):
            _flash_update(q_ref[0, :, g * HEAD_DIM_A:(g + 1) * HEAD_DIM_A], k, v,
                          m_ref.at[g], l_ref.at[g], acc_ref.at[g])

    def body(i, carry):
        off = pl.multiple_of(i * tk, tk)
        update(k_ref[0, pl.ds(off, tk), :], v_ref[0, pl.ds(off, tk), :])
        return carry

    lax.fori_loop(0, k_ref.shape[1] // tk, body, 0, unroll=GQA_UNROLL)
    update(ck_ref[0], cv_ref[0])
    o_ref[0] = jnp.concatenate([acc_ref[g] / l_ref[g] for g in range(group)], axis=1).astype(BF16)


def _gqa(qa, ka, va, cka, cva, *, tq, tk):
    b, n, _ = qa.shape
    c = cka.shape[1]
    group = HEADS_A // KV_HEADS_A
    gw = group * HEAD_DIM_A
    return pl.pallas_call(
        functools.partial(_gqa_kernel, tk=tk),
        grid=(b, KV_HEADS_A, n // tq),
        in_specs=[
            pl.BlockSpec((1, tq, gw), lambda bb, h, i: (bb, i, h)),
            pl.BlockSpec((1, n, HEAD_DIM_A), lambda bb, h, i: (bb, 0, h)),
            pl.BlockSpec((1, n, HEAD_DIM_A), lambda bb, h, i: (bb, 0, h)),
            pl.BlockSpec((1, c, HEAD_DIM_A), lambda bb, h, i: (bb, 0, h)),
            pl.BlockSpec((1, c, HEAD_DIM_A), lambda bb, h, i: (bb, 0, h)),
        ],
        out_specs=pl.BlockSpec((1, tq, gw), lambda bb, h, i: (bb, i, h)),
        out_shape=jax.ShapeDtypeStruct((b, n, QA_COLS), BF16),
        scratch_shapes=[
            pltpu.VMEM((group, tq, HEAD_DIM_A), F32),
            pltpu.VMEM((group, tq, HEAD_DIM_A), F32),
            pltpu.VMEM((group, tq, HEAD_DIM_A), F32),
        ],
        compiler_params=_cparams(("parallel", "parallel", "arbitrary")),
        name="gqa",
    )(qa, ka, va, cka, cva)


NA_UNROLL = 16


def _na_kernel(q_ref, k_ref, v_ref, ck_ref, cv_ref, bias_ref, o_ref):
    rows = q_ref.shape[1] // GRID_W
    win = NA_ROWS * GRID_W
    lane = lax.broadcasted_iota(jnp.int32, (GRID_W, 2 * HEAD_DIM_B), 1)
    lo = lane < HEAD_DIM_B
    pw = 2 * HEAD_DIM_B

    def with_ones(v):
        return jnp.concatenate([v, jnp.ones_like(v)], axis=1)

    def body(r, carry):
        r0 = jnp.clip(r - NA_ROWS // 2, 0, rows - NA_ROWS)
        q = q_ref[0, pl.ds(pl.multiple_of(r * GRID_W, GRID_W), GRID_W), :]
        zero = jnp.zeros_like(q)
        qq = jnp.concatenate([jnp.where(lo, q, zero), jnp.where(lo, zero, q)], axis=0)
        koff = pl.multiple_of(r0 * GRID_W, GRID_W)
        s = _dot_t(qq, k_ref[0, pl.ds(koff, win), :]) + bias_ref[r - r0]
        sc = _dot_t(qq, ck_ref[0])
        m = jnp.maximum(jnp.max(s, axis=-1, keepdims=True), jnp.max(sc, axis=-1, keepdims=True))
        p = jnp.exp2(s - m).astype(BF16)
        pc = jnp.exp2(sc - m).astype(BF16)
        o = _dot(p, with_ones(v_ref[0, pl.ds(koff, win), :])) + _dot(pc, with_ones(cv_ref[0]))
        o = o[:, :pw] / o[:, pw:]
        res = jnp.where(lo, o[:GRID_W], o[GRID_W:])
        o_ref[0, pl.ds(pl.multiple_of(r * GRID_W, GRID_W), GRID_W), :] = res.astype(BF16)
        return carry

    lax.fori_loop(0, rows, body, 0, unroll=NA_UNROLL)


def _na(qb, kb, vb, ckb, cvb, bias):
    b, n, _ = qb.shape
    c = ckb.shape[1]
    pw = 2 * HEAD_DIM_B
    n_cls = bias.shape[0]
    tok = lambda bb, h: (bb, 0, h)
    return pl.pallas_call(
        _na_kernel,
        grid=(b, HEADS_B // 2),
        in_specs=[
            pl.BlockSpec((1, n, pw), tok),
            pl.BlockSpec((1, n, pw), tok),
            pl.BlockSpec((1, n, pw), tok),
            pl.BlockSpec((1, c, pw), tok),
            pl.BlockSpec((1, c, pw), tok),
            pl.BlockSpec((n_cls, 2 * GRID_W, NA_ROWS * GRID_W), lambda bb, h: (0, h, 0)),
        ],
        out_specs=pl.BlockSpec((1, n, pw), tok),
        out_shape=jax.ShapeDtypeStruct((b, n, B_COLS), BF16),
        compiler_params=_cparams(("parallel", "parallel")),
        name="natten",
    )(qb, kb, vb, ckb, cvb, bias)


def _na_bias_table(rpb):
    edge = GRID_W - NA_COLS
    n_off = rpb.shape[1]
    padded = jnp.pad(rpb, ((0, 0), (0, 1), (edge, 2 * GRID_W - rpb.shape[2] - edge)))
    return pl.pallas_call(
        functools.partial(_na_bias_kernel, edge=edge),
        grid=(NA_ROWS,),
        in_specs=[pl.BlockSpec((HEADS_B, n_off + 1, 2 * GRID_W), lambda s: (0, 0, 0))],
        out_specs=pl.BlockSpec((1, HEADS_B * GRID_W, NA_ROWS * GRID_W), lambda s: (s, 0, 0)),
        out_shape=jax.ShapeDtypeStruct((NA_ROWS, HEADS_B * GRID_W, NA_ROWS * GRID_W), F32),
        compiler_params=_cparams(("parallel",)),
        name="na_bias",
    )(padded)


def _na_bias_kernel(rp_ref, o_ref, *, edge):
    cls = pl.program_id(0)
    shape = (GRID_W, 2 * GRID_W)
    c = lax.broadcasted_iota(jnp.int32, shape, 0)
    lane = lax.broadcasted_iota(jnp.int32, shape, 1)
    lo = lane < GRID_W
    j = jnp.where(lo, lane, lane - GRID_W)
    cs = jnp.clip(c - NA_COLS // 2, 0, edge)
    valid = (j >= cs) & (j < cs + NA_COLS)

    def toeplitz(h, i, shift):
        row = rp_ref[h, pl.ds(i - cls + NA_ROWS - 1, 1), :]
        return pltpu.roll(jnp.broadcast_to(row, shape), shift, 1, stride=1, stride_axis=0)

    for h in range(HEADS_B):
        for pair in range(NA_ROWS // 2):
            t = jnp.where(lo, toeplitz(h, 2 * pair, GRID_W + 1), toeplitz(h, 2 * pair + 1, 1))
            o_ref[0, h * GRID_W:(h + 1) * GRID_W, pair * 2 * GRID_W:(pair + 1) * 2 * GRID_W] = (
                jnp.where(valid, t * LOG2E, NEG))


def _outproj_kernel(x_ref, oa_ref, ob_ref, w_ref, g1_ref, o_ref):
    ka = oa_ref.shape[2]
    y = _dot(oa_ref[0], w_ref[0:ka, :]) + _dot(ob_ref[0], w_ref[ka:, :])
    o_ref[0] = x_ref[0] + g1_ref[0] * y


def _outproj(x, oa, ob, w, g1, *, tm):
    b, n, d = x.shape
    row = lambda bb, i: (bb, i, 0)
    return pl.pallas_call(
        _outproj_kernel,
        grid=(b, n // tm),
        in_specs=[
            pl.BlockSpec((1, tm, d), row),
            pl.BlockSpec((1, tm, oa.shape[2]), row),
            pl.BlockSpec((1, tm, ob.shape[2]), row),
            pl.BlockSpec(w.shape, lambda bb, i: (0, 0)),
            pl.BlockSpec((1, 1, d), lambda bb, i: (bb, 0, 0)),
        ],
        out_specs=pl.BlockSpec((1, tm, d), row),
        out_shape=jax.ShapeDtypeStruct(x.shape, F32),
        compiler_params=_cparams(("parallel", "parallel")),
        name="attn_out",
    )(x, oa, ob, w, g1)


HALO = 8
FFN_CHUNK = 256
FFN_TM = 512
FFN_SLOTS = 3
FFN_ROWS = 32


def _ffn_kernel(x_ref, xp_ref, xn_ref, sh_ref, sc_ref, g2_ref, ng_ref, wup_ref, cw_ref, cb_ref,
                wdn_ref, fg_ref, o_ref, u_ref, a_ref, y_ref, *, tm, d_ff, final_norm):
    i = pl.program_id(1)
    ng, sh, sc = ng_ref[...], sh_ref[0], sc_ref[0]
    d = x_ref.shape[2]
    half = tm // 2
    slabs = FFN_CHUNK // LANES

    hp = jnp.where(i > 0, _norm_mod(xp_ref[0], ng, sh, sc), 0.0)
    hn = jnp.where(i < pl.num_programs(1) - 1, _norm_mod(xn_ref[0], ng, sh, sc), 0.0)
    h = jnp.concatenate([hp, _norm_mod(x_ref[0], ng, sh, sc), hn], axis=0).astype(BF16)

    def project(buf, part, col):
        u = _dot(h, wup_ref[:, col:col + FFN_CHUNK])
        for s in range(slabs):
            u_ref[buf, part, s] = u[:, s * LANES:(s + 1) * LANES]

    def conv_gate(buf, c0):
        for s in range(slabs):
            cols = (c0 + s * LANES, d_ff + c0 + s * LANES)
            taps_w = [[cw_ref[k:k + 1, c:c + LANES] for k in range(3)] for c in cols]
            bias = [cb_ref[:, c:c + LANES] for c in cols]
            for rb in range(half // FFN_ROWS):
                res = []
                for part in range(2):
                    t = [u_ref[buf, part, s, pl.ds(HALO - 1 + k + 2 * rb * FFN_ROWS, FFN_ROWS, stride=2), :]
                         for k in range(4)]
                    w0, w1, w2 = taps_w[part]
                    res.append((t[0] * w0 + t[1] * w1 + t[2] * w2 + bias[part],
                                t[1] * w0 + t[2] * w1 + t[3] * w2 + bias[part]))
                (g_even, g_odd), (v_even, v_odd) = res
                r0 = rb * FFN_ROWS
                col = slice(c0 + s * LANES, c0 + (s + 1) * LANES)
                a_ref[r0:r0 + FFN_ROWS, col] = (g_even * jax.nn.sigmoid(g_even) * v_even).astype(BF16)
                a_ref[half + r0:half + r0 + FFN_ROWS, col] = (
                    g_odd * jax.nn.sigmoid(g_odd) * v_odd).astype(BF16)

    n_chunks = d_ff // FFN_CHUNK
    n_buf = u_ref.shape[0]

    def project_chunk(f):
        project(f % n_buf, 0, f * FFN_CHUNK)
        project(f % n_buf, 1, d_ff + f * FFN_CHUNK)

    for f in range(min(n_buf - 1, n_chunks)):
        project_chunk(f)
    for f in range(n_chunks):
        c0 = f * FFN_CHUNK
        if f + n_buf - 1 < n_chunks:
            project_chunk(f + n_buf - 1)
        conv_gate(f % n_buf, c0)
    acc = _dot(a_ref[...], wdn_ref[...])
    for s in range(d // LANES):
        y_ref[s, pl.ds(0, half, stride=2), :] = acc[:half, s * LANES:(s + 1) * LANES]
        y_ref[s, pl.ds(1, half, stride=2), :] = acc[half:, s * LANES:(s + 1) * LANES]
    ffn = jnp.concatenate([y_ref[s] for s in range(d // LANES)], axis=1)
    y = x_ref[0] + g2_ref[0] * ffn
    if final_norm:
        y = _rms(y, fg_ref[...])
    o_ref[0] = y


def _ffn(x, shift, scale, gate, norm_g, w_up, conv_w, conv_b, w_down, final_g, *, tm, final_norm):
    b, n, d = x.shape
    d_ff = w_down.shape[0]
    hb = tm // HALO
    last = n // HALO - 1
    vec = lambda bb, i: (bb, 0, 0)
    const2 = lambda bb, i: (0, 0)
    resident = dict(pipeline_mode=pl.Buffered(1))
    return pl.pallas_call(
        functools.partial(_ffn_kernel, tm=tm, d_ff=d_ff, final_norm=final_norm),
        grid=(b, n // tm),
        in_specs=[
            pl.BlockSpec((1, tm, d), lambda bb, i: (bb, i, 0)),
            pl.BlockSpec((1, HALO, d), lambda bb, i: (bb, jnp.maximum(i * hb - 1, 0), 0)),
            pl.BlockSpec((1, HALO, d), lambda bb, i: (bb, jnp.minimum((i + 1) * hb, last), 0)),
            pl.BlockSpec((1, 1, d), vec),
            pl.BlockSpec((1, 1, d), vec),
            pl.BlockSpec((1, 1, d), vec),
            pl.BlockSpec((1, d), const2),
            pl.BlockSpec(w_up.shape, const2, **resident),
            pl.BlockSpec(conv_w.shape, const2),
            pl.BlockSpec((1, 2 * d_ff), const2),
            pl.BlockSpec(w_down.shape, const2, **resident),
            pl.BlockSpec((1, d), const2),
        ],
        out_specs=pl.BlockSpec((1, tm, d), lambda bb, i: (bb, i, 0)),
        out_shape=jax.ShapeDtypeStruct(x.shape, F32),
        scratch_shapes=[
            pltpu.VMEM((FFN_SLOTS, 2, FFN_CHUNK // LANES, tm + 2 * HALO, LANES), F32),
            pltpu.VMEM((tm, d_ff), BF16),
            pltpu.VMEM((d // LANES, tm, LANES), F32),
        ],
        compiler_params=_cparams(("parallel", "parallel")),
        name="conv_ffn",
    )(x, x, x, shift, scale, gate, norm_g, w_up, conv_w, conv_b.reshape(1, -1), w_down, final_g)


F1_NC = 8
F2_NK = 16


def _dft_tables(n_tok, cg):
    w = GRID_W
    idx = np.arange(w, dtype=np.float64)
    a64 = 2.0 * np.pi * np.outer(idx, idx) / w
    fc, fs = np.cos(a64), np.sin(a64)
    step1 = np.block([[fc, fs], [-fs, fc]])
    eye = np.eye(F2_NK)
    step3 = (np.kron(fc[:w // 2 + 1], eye), np.kron(fs[:w // 2 + 1], eye))
    atw = 2.0 * np.pi * np.outer(idx, idx) / n_tok
    tw_c = np.repeat(np.cos(atw)[:, :, None], 128, axis=2)
    tw_s = np.repeat(np.sin(atw)[:, :, None], 128, axis=2)
    ch = np.arange(cg, dtype=np.float64)
    ach = 2.0 * np.pi * np.outer(ch, ch) / cg
    chan = np.concatenate([np.cos(ach), -np.sin(ach)], axis=1)
    f = lambda t: jnp.asarray(t, F32)
    return f(chan), f(step1), f(step3[0]), f(step3[1]), f(tw_c), f(tw_s)


def _f1_kernel(x_ref, sh_ref, sc_ref, g_ref, chan_ref, s1_ref, twc_ref, tws_ref, ar_ref, ai_ref,
               u_ref, *, nc, d):
    w = GRID_W
    cg = d // FOURIER_GROUPS
    tiles = d // LANES
    gt = cg // LANES
    x = x_ref[0].reshape(w * nc, d)
    h = _norm_mod(x, g_ref[...], sh_ref[0], sc_ref[0]).astype(BF16)
    for g in range(FOURIER_GROUPS):
        u = _dot(h[:, g * cg:(g + 1) * cg], chan_ref[...])
        for t in range(2 * gt):
            part, tt = divmod(t, gt)
            u_ref[part * tiles + g * gt + tt] = u[:, t * LANES:(t + 1) * LANES]
    for c in range(nc):
        rows = pl.ds(c, w, stride=nc)
        ur = jnp.concatenate([u_ref[s, rows, :] for s in range(tiles)], axis=1)
        ui = jnp.concatenate([u_ref[tiles + s, rows, :] for s in range(tiles)], axis=1)
        a = _dot(s1_ref[...], jnp.concatenate([ur, ui], axis=0).astype(BF16))
        a_r, a_i = a[:w], a[w:]
        tc = jnp.tile(twc_ref[c], (1, tiles))
        ts = jnp.tile(tws_ref[c], (1, tiles))
        ar_ref[0, c] = (a_r * tc + a_i * ts).astype(BF16)
        ai_ref[0, c] = (a_i * tc - a_r * ts).astype(BF16)


def _f2_kernel(ar_ref, ai_ref, x_ref, g1_ref, kc_ref, ks_ref, w_ref, o_ref, *, nk, d, inv_norm):
    w = GRID_W
    a_r = ar_ref[0].reshape(w * nk, d)
    a_i = ai_ref[0].reshape(w * nk, d)
    p = _dot(kc_ref[...], a_r)
    q = _dot(ks_ref[...], a_i)
    diff = p - q
    mirrored = [diff[(w - k2) * nk:(w - k2 + 1) * nk] for k2 in range(w // 2 + 1, w)]
    z = jnp.concatenate([p + q] + mirrored, axis=0)
    y = _dot((z * inv_norm).astype(BF16), w_ref[...])
    o_ref[0] = x_ref[0] + g1_ref[0] * y.reshape(w, nk, d)


def _fourier(x, shift, scale, gate, norm_g, w_out):
    b, n, d = x.shape
    w = GRID_W
    assert n == w * w
    nc, nk = F1_NC, F2_NK
    cg = d // FOURIER_GROUPS
    chan, step1, kc, ks, tw_c, tw_s = _dft_tables(n, cg)
    chan, step1, kc, ks = (t.astype(BF16) for t in (chan, step1, kc, ks))
    x4 = x.reshape(b, w, w, d)
    vec = lambda bb, j: (bb, 0, 0)
    const2 = lambda bb, j: (0, 0)
    resident = dict(pipeline_mode=pl.Buffered(1))
    a_r, a_i = pl.pallas_call(
        functools.partial(_f1_kernel, nc=nc, d=d),
        grid=(b, w // nc),
        in_specs=[
            pl.BlockSpec((1, w, nc, d), lambda bb, j: (bb, 0, j, 0)),
            pl.BlockSpec((1, 1, d), vec),
            pl.BlockSpec((1, 1, d), vec),
            pl.BlockSpec((1, d), const2),
            pl.BlockSpec(chan.shape, const2),
            pl.BlockSpec(step1.shape, const2),
            pl.BlockSpec((nc, w, LANES), lambda bb, j: (j, 0, 0)),
            pl.BlockSpec((nc, w, LANES), lambda bb, j: (j, 0, 0)),
        ],
        out_specs=[pl.BlockSpec((1, nc, w, d), lambda bb, j: (bb, j, 0, 0))] * 2,
        out_shape=[jax.ShapeDtypeStruct((b, w, w, d), BF16)] * 2,
        scratch_shapes=[pltpu.VMEM((2 * d // LANES, w * nc, LANES), F32)],
        compiler_params=_cparams(("parallel", "parallel")),
        name="fourier_rows",
    )(x4, shift, scale, norm_g, chan, step1, tw_c, tw_s)
    out = pl.pallas_call(
        functools.partial(_f2_kernel, nk=nk, d=d, inv_norm=1.0 / float(np.sqrt(n * cg))),
        grid=(b, w // nk),
        in_specs=[
            pl.BlockSpec((1, w, nk, d), lambda bb, j: (bb, 0, j, 0)),
            pl.BlockSpec((1, w, nk, d), lambda bb, j: (bb, 0, j, 0)),
            pl.BlockSpec((1, w, nk, d), lambda bb, j: (bb, 0, j, 0)),
            pl.BlockSpec((1, 1, d), vec),
            pl.BlockSpec(kc.shape, const2, **resident),
            pl.BlockSpec(ks.shape, const2, **resident),
            pl.BlockSpec(w_out.shape, const2, **resident),
        ],
        out_specs=pl.BlockSpec((1, w, nk, d), lambda bb, j: (bb, 0, j, 0)),
        out_shape=jax.ShapeDtypeStruct((b, w, w, d), F32),
        compiler_params=_cparams(("parallel", "parallel")),
        name="fourier_cols",
    )(a_r, a_i, x4, gate, kc, ks, w_out)
    return out.reshape(b, n, d)


def _rope_tables(n):
    t = jnp.arange(n)
    row = (t // GRID_W).astype(F32)
    col = (t % GRID_W).astype(F32)
    pairs = HEAD_DIM_A // 4
    freqs = ROPE_THETA ** (-jnp.arange(pairs, dtype=F32) / pairs)
    ang = jnp.concatenate([row[:, None] * freqs, col[:, None] * freqs], axis=-1)
    cos, sin = jnp.cos(ang), jnp.sin(ang)
    return jnp.concatenate([cos, cos], axis=-1), jnp.concatenate([-sin, sin], axis=-1)


def kernel(x, c, ctx, c_ctx, mod_w, mod_b, norm1_g, norm2_g, attn_w_in, attn_w_out, q_norm_g,
           k_norm_g, na_rpb, fourier_w_out, ffn_w_up, ffn_conv_w, ffn_conv_b, ffn_w_down, final_g):
    b, n, d = x.shape
    depth = mod_w.shape[0]
    assert depth == 2 and b + 1 <= 8

    vec8 = jnp.zeros((8, d), F32).at[:b].set(c).at[b].set(c_ctx)
    mod = _adaln(vec8, mod_w, mod_b)
    term = lambda l, k: mod[l, :b, k * d:(k + 1) * d].reshape(b, 1, d)
    cterm = lambda l, k: jnp.broadcast_to(mod[l, b, k * d:(k + 1) * d].reshape(1, 1, d), (b, 1, d))

    perm = np.concatenate([np.arange(0, HEAD_DIM_A, 2), np.arange(1, HEAD_DIM_A, 2)])
    cols = np.arange(attn_w_in.shape[2])
    q_cols = QA_COLS + B_COLS
    for base, nh in ((0, HEADS_A), (q_cols, KV_HEADS_A)):
        for hd in range(nh):
            s0 = base + hd * HEAD_DIM_A
            cols[s0:s0 + HEAD_DIM_A] = s0 + perm
    w_in = attn_w_in[0][:, cols].astype(BF16)
    qg = q_norm_g[0][perm].reshape(1, -1)
    kg = k_norm_g[0][perm].reshape(1, -1)
    cosf, sinf = _rope_tables(n)
    n_ctx = ctx.shape[1]
    g1row = norm1_g[0].reshape(1, d)

    qa, qb, ka, va, kb, vb = _proj(x, term(0, 0), term(0, 1), g1row, w_in, cosf, sinf, qg, kg,
                                   with_q=True, tm=512, name="proj_latent")
    cka, cva, ckb, cvb = _proj(ctx, cterm(0, 0), cterm(0, 1), g1row, w_in[:, q_cols:],
                               jnp.ones((n_ctx, HEAD_DIM_A), F32), jnp.zeros((n_ctx, HEAD_DIM_A), F32),
                               qg, kg, with_q=False, tm=n_ctx, name="proj_context")
    oa = _gqa(qa, ka, va, cka, cva, tq=512, tk=512)
    ob = _na(qb, kb, vb, ckb, cvb, _na_bias_table(na_rpb[0]))
    x = _outproj(x, oa, ob, attn_w_out[0].astype(BF16), term(0, 2), tm=1024)
    x = _ffn(x, term(0, 3), term(0, 4), term(0, 5), norm2_g[0].reshape(1, d),
             ffn_w_up[0].astype(BF16), ffn_conv_w[0], ffn_conv_b[0], ffn_w_down[0].astype(BF16),
             final_g.reshape(1, d), tm=FFN_TM, final_norm=False)

    x = _fourier(x, term(1, 0), term(1, 1), term(1, 2), norm1_g[1].reshape(1, d),
                 fourier_w_out[0].astype(BF16))
    x = _ffn(x, term(1, 3), term(1, 4), term(1, 5), norm2_g[1].reshape(1, d),
             ffn_w_up[1].astype(BF16), ffn_conv_w[1], ffn_conv_b[1], ffn_w_down[1].astype(BF16),
             final_g.reshape(1, d), tm=FFN_TM, final_norm=True)
    return x
```

```python
import functools

import numpy as np
import jax
import jax.numpy as jnp
from jax import lax
from jax.experimental import pallas as pl
from jax.experimental.pallas import tpu as pltpu

F32 = jnp.float32
BF16 = jnp.bfloat16

GRID_W = 64
HEAD_DIM_A = 128
HEADS_A = 4
KV_HEADS_A = 2
HEAD_DIM_B = 64
HEADS_B = 8
NA_ROWS = 8
NA_COLS = 16
FOURIER_GROUPS = 4
ROPE_THETA = 10000.0
EPS = 1e-6
QA_COLS = HEADS_A * HEAD_DIM_A
KA_COLS = KV_HEADS_A * HEAD_DIM_A
B_COLS = HEADS_B * HEAD_DIM_B
NEG = -1e30
LOG2E = 1.4426950408889634
LANES = 128

VMEM_LIMIT = 56 * 1024 * 1024


def _cparams(sem):
    return pltpu.CompilerParams(dimension_semantics=sem, vmem_limit_bytes=VMEM_LIMIT)


def _rms(x, g):
    ms = jnp.mean(x * x, axis=-1, keepdims=True)
    return x * lax.rsqrt(ms + EPS) * g


def _norm_mod(x, g, shift, scale):
    return _rms(x, g) * (1.0 + scale) + shift


def _dot(a, b):
    return jnp.dot(a, b, preferred_element_type=F32)


def _dot_t(a, b):
    return lax.dot_general(a, b, (((1,), (1,)), ((), ())), preferred_element_type=F32)


CAST_ROW_BLOCKS = 4


def _cast_kernel(w_ref, o_ref):
    o_ref[...] = w_ref[...].astype(o_ref.dtype)


def _to_bf16(w):
    layers, rows, cols = w.shape
    rb = rows // CAST_ROW_BLOCKS
    spec = pl.BlockSpec((1, rb, cols), lambda l, r: (l, r, 0))
    return pl.pallas_call(
        _cast_kernel,
        grid=(layers, CAST_ROW_BLOCKS),
        in_specs=[spec],
        out_specs=spec,
        out_shape=jax.ShapeDtypeStruct(w.shape, BF16),
        compiler_params=_cparams(("parallel", "parallel")),
        name="cast_bf16",
    )(w)


def _adaln_kernel(v_ref, w_ref, b_ref, o_ref):
    v = v_ref[...]
    s = v * jax.nn.sigmoid(v)
    o_ref[0] = jnp.dot(s, w_ref[0], preferred_element_type=F32,
                       precision=lax.Precision.HIGHEST) + b_ref[0]


def _adaln(vec8, mod_w, mod_b):
    depth, d, n = mod_w.shape
    tn = 1536
    return pl.pallas_call(
        _adaln_kernel,
        grid=(depth, n // tn),
        in_specs=[
            pl.BlockSpec((8, d), lambda l, j: (0, 0)),
            pl.BlockSpec((1, d, tn), lambda l, j: (l, 0, j)),
            pl.BlockSpec((1, 1, tn), lambda l, j: (l, 0, j)),
        ],
        out_specs=pl.BlockSpec((1, 8, tn), lambda l, j: (l, 0, j)),
        out_shape=jax.ShapeDtypeStruct((depth, 8, n), F32),
        compiler_params=_cparams(("arbitrary", "arbitrary")),
        name="adaln",
    )(vec8, mod_w, mod_b.reshape(depth, 1, n))


PROJ_TM = 512


def _proj_kernel(x_ref, sh_ref, sc_ref, g_ref, w_ref, cos_ref, sin_ref, qg_ref, kg_ref,
                 *out_refs, with_q):
    h = _norm_mod(x_ref[0], g_ref[...], sh_ref[0], sc_ref[0]).astype(BF16)
    cosf = cos_ref[...]
    sinf = sin_ref[...]

    def rope_heads(p, n_heads, gain, scale, o_ref):
        for hd in range(n_heads):
            t = _rms(p[:, hd * HEAD_DIM_A:(hd + 1) * HEAD_DIM_A], gain)
            t = t * cosf + pltpu.roll(t, HEAD_DIM_A // 2, axis=1) * sinf
            o_ref[0, :, hd * HEAD_DIM_A:(hd + 1) * HEAD_DIM_A] = (t * scale).astype(BF16)

    col = QA_COLS + B_COLS
    if with_q:
        qa_ref, qb_ref, ka_ref, va_ref, kb_ref, vb_ref = out_refs
        rope_heads(_dot(h, w_ref[:, 0:QA_COLS]), HEADS_A, qg_ref[...], HEAD_DIM_A ** -0.5 * LOG2E,
                   qa_ref)
        qb = _dot(h, w_ref[:, QA_COLS:QA_COLS + B_COLS])
        qb_ref[0] = (qb * (HEAD_DIM_B ** -0.5 * LOG2E)).astype(BF16)
    else:
        ka_ref, va_ref, kb_ref, vb_ref = out_refs
    rope_heads(_dot(h, w_ref[:, col:col + KA_COLS]), KV_HEADS_A, kg_ref[...], 1.0, ka_ref)
    col += KA_COLS
    va_ref[0] = _dot(h, w_ref[:, col:col + KA_COLS]).astype(BF16)
    col += KA_COLS
    kb_ref[0] = _dot(h, w_ref[:, col:col + B_COLS]).astype(BF16)
    col += B_COLS
    vb_ref[0] = _dot(h, w_ref[:, col:col + B_COLS]).astype(BF16)


def _proj(x, shift, scale, gain, w, rope, qg, kg, *, with_q, tm, name):
    b, n, d = x.shape
    ncol = w.shape[1]
    widths = ([QA_COLS, B_COLS] if with_q else []) + [KA_COLS, KA_COLS, B_COLS, B_COLS]
    vec = lambda bb, i: (bb, 0, 0)
    return pl.pallas_call(
        functools.partial(_proj_kernel, with_q=with_q),
        grid=(b, n // tm),
        in_specs=[
            pl.BlockSpec((1, tm, d), lambda bb, i: (bb, i, 0)),
            pl.BlockSpec((1, 1, d), vec),
            pl.BlockSpec((1, 1, d), vec),
            pl.BlockSpec((1, d), lambda bb, i: (0, 0)),
            pl.BlockSpec((d, ncol), lambda bb, i: (0, 0)),
            pl.BlockSpec((tm, HEAD_DIM_A), lambda bb, i: (i, 0)),
            pl.BlockSpec((tm, HEAD_DIM_A), lambda bb, i: (i, 0)),
            pl.BlockSpec((1, HEAD_DIM_A), lambda bb, i: (0, 0)),
            pl.BlockSpec((1, HEAD_DIM_A), lambda bb, i: (0, 0)),
        ],
        out_specs=[pl.BlockSpec((1, tm, wd), lambda bb, i: (bb, i, 0)) for wd in widths],
        out_shape=[jax.ShapeDtypeStruct((b, n, wd), BF16) for wd in widths],
        compiler_params=_cparams(("parallel", "parallel")),
        name=name,
    )(x, shift, scale, gain, w, *rope, qg, kg)


GQA_UNROLL = 8


def _flash_update(q, k, v, m_ref, l_ref, acc_ref):
    d = v.shape[1]
    s = _dot_t(q, k)
    m_old = m_ref[...]
    m_new = jnp.maximum(m_old, jnp.max(s, axis=-1, keepdims=True))
    alpha = jnp.exp2(m_old - m_new)
    p = jnp.exp2(s - jnp.tile(m_new, (1, s.shape[1] // m_new.shape[1])))
    o = _dot(p.astype(BF16), jnp.concatenate([v, jnp.ones_like(v)], axis=1))
    acc_ref[...] = alpha * acc_ref[...] + o[:, :d]
    l_ref[...] = alpha * l_ref[...] + o[:, d:]
    m_ref[...] = m_new


def _gqa_kernel(q_ref, k_ref, v_ref, ck_ref, cv_ref, o_ref, m_ref, l_ref, acc_ref, *, tk):
    group = q_ref.shape[2] // HEAD_DIM_A
    m_ref[...] = jnp.full(m_ref.shape, NEG, F32)
    l_ref[...] = jnp.zeros(l_ref.shape, F32)
    acc_ref[...] = jnp.zeros(acc_ref.shape, F32)

    def update(k, v):
        for g in range(group):
            _flash_update(q_ref[0, :, g * HEAD_DIM_A:(g + 1) * HEAD_DIM_A], k, v,
                          m_ref.at[g], l_ref.at[g], acc_ref.at[g])

    def body(i, carry):
        off = pl.multiple_of(i * tk, tk)
        update(k_ref[0, pl.ds(off, tk), :], v_ref[0, pl.ds(off, tk), :])
        return carry

    lax.fori_loop(0, k_ref.shape[1] // tk, body, 0, unroll=GQA_UNROLL)
    update(ck_ref[0], cv_ref[0])
    o_ref[0] = jnp.concatenate([acc_ref[g] / l_ref[g] for g in range(group)], axis=1).astype(BF16)


def _gqa(qa, ka, va, cka, cva, *, tq, tk):
    b, n, _ = qa.shape
    c = cka.shape[1]
    group = HEADS_A // KV_HEADS_A
    gw = group * HEAD_DIM_A
    return pl.pallas_call(
        functools.partial(_gqa_kernel, tk=tk),
        grid=(b, KV_HEADS_A, n // tq),
        in_specs=[
            pl.BlockSpec((1, tq, gw), lambda bb, h, i: (bb, i, h)),
            pl.BlockSpec((1, n, HEAD_DIM_A), lambda bb, h, i: (bb, 0, h)),
            pl.BlockSpec((1, n, HEAD_DIM_A), lambda bb, h, i: (bb, 0, h)),
            pl.BlockSpec((1, c, HEAD_DIM_A), lambda bb, h, i: (bb, 0, h)),
            pl.BlockSpec((1, c, HEAD_DIM_A), lambda bb, h, i: (bb, 0, h)),
        ],
        out_specs=pl.BlockSpec((1, tq, gw), lambda bb, h, i: (bb, i, h)),
        out_shape=jax.ShapeDtypeStruct((b, n, QA_COLS), BF16),
        scratch_shapes=[
            pltpu.VMEM((group, tq, HEAD_DIM_A), F32),
            pltpu.VMEM((group, tq, HEAD_DIM_A), F32),
            pltpu.VMEM((group, tq, HEAD_DIM_A), F32),
        ],
        compiler_params=_cparams(("parallel", "parallel", "arbitrary")),
        name="gqa",
    )(qa, ka, va, cka, cva)


NA_UNROLL = 32


def _na_kernel(q_ref, k_ref, v_ref, ck_ref, cv_ref, bias_ref, o_ref):
    rows = q_ref.shape[1] // GRID_W
    win = NA_ROWS * GRID_W
    lane = lax.broadcasted_iota(jnp.int32, (GRID_W, 2 * HEAD_DIM_B), 1)
    lo = lane < HEAD_DIM_B
    pw = 2 * HEAD_DIM_B

    def with_ones(v):
        return jnp.concatenate([v, jnp.ones_like(v)], axis=1)

    def body(r, carry):
        r0 = jnp.clip(r - NA_ROWS // 2, 0, rows - NA_ROWS)
        q = q_ref[0, pl.ds(pl.multiple_of(r * GRID_W, GRID_W), GRID_W), :]
        zero = jnp.zeros_like(q)
        qq = jnp.concatenate([jnp.where(lo, q, zero), jnp.where(lo, zero, q)], axis=0)
        koff = pl.multiple_of(r0 * GRID_W, GRID_W)
        s = _dot_t(qq, k_ref[0, pl.ds(koff, win), :]) + bias_ref[r - r0]
        sc = _dot_t(qq, ck_ref[0])
        m = jnp.maximum(jnp.max(s, axis=-1, keepdims=True), jnp.max(sc, axis=-1, keepdims=True))
        p = jnp.exp2(s - m).astype(BF16)
        pc = jnp.exp2(sc - m).astype(BF16)
        o = _dot(p, with_ones(v_ref[0, pl.ds(koff, win), :])) + _dot(pc, with_ones(cv_ref[0]))
        o = o[:, :pw] / o[:, pw:]
        res = jnp.where(lo, o[:GRID_W], o[GRID_W:])
        o_ref[0, pl.ds(pl.multiple_of(r * GRID_W, GRID_W), GRID_W), :] = res.astype(BF16)
        return carry

    lax.fori_loop(0, rows, body, 0, unroll=NA_UNROLL)


def _na(qb, kb, vb, ckb, cvb, bias):
    b, n, _ = qb.shape
    c = ckb.shape[1]
    pw = 2 * HEAD_DIM_B
    n_cls = bias.shape[0]
    tok = lambda bb, h: (bb, 0, h)
    return pl.pallas_call(
        _na_kernel,
        grid=(b, HEADS_B // 2),
        in_specs=[
            pl.BlockSpec((1, n, pw), tok),
            pl.BlockSpec((1, n, pw), tok),
            pl.BlockSpec((1, n, pw), tok),
            pl.BlockSpec((1, c, pw), tok),
            pl.BlockSpec((1, c, pw), tok),
            pl.BlockSpec((n_cls, 2 * GRID_W, NA_ROWS * GRID_W), lambda bb, h: (0, h, 0)),
        ],
        out_specs=pl.BlockSpec((1, n, pw), tok),
        out_shape=jax.ShapeDtypeStruct((b, n, B_COLS), BF16),
        compiler_params=_cparams(("parallel", "parallel")),
        name="natten",
    )(qb, kb, vb, ckb, cvb, bias)


def _na_bias_table(rpb):
    edge = GRID_W - NA_COLS
    n_off = rpb.shape[1]
    padded = jnp.pad(rpb, ((0, 0), (0, 1), (edge, 2 * GRID_W - rpb.shape[2] - edge)))
    return pl.pallas_call(
        functools.partial(_na_bias_kernel, edge=edge),
        grid=(NA_ROWS,),
        in_specs=[pl.BlockSpec((HEADS_B, n_off + 1, 2 * GRID_W), lambda s: (0, 0, 0))],
        out_specs=pl.BlockSpec((1, HEADS_B * GRID_W, NA_ROWS * GRID_W), lambda s: (s, 0, 0)),
        out_shape=jax.ShapeDtypeStruct((NA_ROWS, HEADS_B * GRID_W, NA_ROWS * GRID_W), F32),
        compiler_params=_cparams(("parallel",)),
        name="na_bias",
    )(padded)


def _na_bias_kernel(rp_ref, o_ref, *, edge):
    cls = pl.program_id(0)
    shape = (GRID_W, 2 * GRID_W)
    c = lax.broadcasted_iota(jnp.int32, shape, 0)
    lane = lax.broadcasted_iota(jnp.int32, shape, 1)
    lo = lane < GRID_W
    j = jnp.where(lo, lane, lane - GRID_W)
    cs = jnp.clip(c - NA_COLS // 2, 0, edge)
    valid = (j >= cs) & (j < cs + NA_COLS)

    def toeplitz(h, i, shift):
        row = rp_ref[h, pl.ds(i - cls + NA_ROWS - 1, 1), :]
        return pltpu.roll(jnp.broadcast_to(row, shape), shift, 1, stride=1, stride_axis=0)

    for h in range(HEADS_B):
        for pair in range(NA_ROWS // 2):
            t = jnp.where(lo, toeplitz(h, 2 * pair, GRID_W + 1), toeplitz(h, 2 * pair + 1, 1))
            o_ref[0, h * GRID_W:(h + 1) * GRID_W, pair * 2 * GRID_W:(pair + 1) * 2 * GRID_W] = (
                jnp.where(valid, t * LOG2E, NEG))


def _outproj_kernel(x_ref, oa_ref, ob_ref, w_ref, g1_ref, o_ref):
    ka = oa_ref.shape[2]
    y = _dot(oa_ref[0], w_ref[0:ka, :]) + _dot(ob_ref[0], w_ref[ka:, :])
    o_ref[0] = x_ref[0] + g1_ref[0] * y


def _outproj(x, oa, ob, w, g1, *, tm):
    b, n, d = x.shape
    row = lambda bb, i: (bb, i, 0)
    return pl.pallas_call(
        _outproj_kernel,
        grid=(b, n // tm),
        in_specs=[
            pl.BlockSpec((1, tm, d), row),
            pl.BlockSpec((1, tm, oa.shape[2]), row),
            pl.BlockSpec((1, tm, ob.shape[2]), row),
            pl.BlockSpec(w.shape, lambda bb, i: (0, 0)),
            pl.BlockSpec((1, 1, d), lambda bb, i: (bb, 0, 0)),
        ],
        out_specs=pl.BlockSpec((1, tm, d), row),
        out_shape=jax.ShapeDtypeStruct(x.shape, F32),
        compiler_params=_cparams(("parallel", "parallel")),
        name="attn_out",
    )(x, oa, ob, w, g1)


HALO = 8
FFN_CHUNK = 256
FFN_TM = 512
FFN_SLOTS = 3
FFN_ROWS = 32


def _ffn_kernel(x_ref, xp_ref, xn_ref, sh_ref, sc_ref, g2_ref, ng_ref, wup_ref, cw_ref, cb_ref,
                wdn_ref, fg_ref, o_ref, u_ref, a_ref, y_ref, *, tm, d_ff, final_norm):
    i = pl.program_id(1)
    ng, sh, sc = ng_ref[...], sh_ref[0], sc_ref[0]
    d = x_ref.shape[2]
    half = tm // 2
    slabs = FFN_CHUNK // LANES

    hp = jnp.where(i > 0, _norm_mod(xp_ref[0], ng, sh, sc), 0.0)
    hn = jnp.where(i < pl.num_programs(1) - 1, _norm_mod(xn_ref[0], ng, sh, sc), 0.0)
    h = jnp.concatenate([hp, _norm_mod(x_ref[0], ng, sh, sc), hn], axis=0).astype(BF16)

    def project(buf, part, col):
        u = _dot(h, wup_ref[:, col:col + FFN_CHUNK])
        for s in range(slabs):
            u_ref[buf, part, s] = u[:, s * LANES:(s + 1) * LANES]

    def conv_gate(buf, c0):
        for s in range(slabs):
            cols = (c0 + s * LANES, d_ff + c0 + s * LANES)
            taps_w = [[cw_ref[k:k + 1, c:c + LANES] for k in range(3)] for c in cols]
            bias = [cb_ref[:, c:c + LANES] for c in cols]
            for rb in range(half // FFN_ROWS):
                res = []
                for part in range(2):
                    t = [u_ref[buf, part, s, pl.ds(HALO - 1 + k + 2 * rb * FFN_ROWS, FFN_ROWS, stride=2), :]
                         for k in range(4)]
                    w0, w1, w2 = taps_w[part]
                    res.append((t[0] * w0 + t[1] * w1 + t[2] * w2 + bias[part],
                                t[1] * w0 + t[2] * w1 + t[3] * w2 + bias[part]))
                (g_even, g_odd), (v_even, v_odd) = res
                r0 = rb * FFN_ROWS
                col = slice(c0 + s * LANES, c0 + (s + 1) * LANES)
                a_ref[r0:r0 + FFN_ROWS, col] = (g_even * jax.nn.sigmoid(g_even) * v_even).astype(BF16)
                a_ref[half + r0:half + r0 + FFN_ROWS, col] = (
                    g_odd * jax.nn.sigmoid(g_odd) * v_odd).astype(BF16)

    n_chunks = d_ff // FFN_CHUNK
    n_buf = u_ref.shape[0]

    def project_chunk(f):
        project(f % n_buf, 0, f * FFN_CHUNK)
        project(f % n_buf, 1, d_ff + f * FFN_CHUNK)

    for f in range(min(n_buf - 1, n_chunks)):
        project_chunk(f)
    for f in range(n_chunks):
        c0 = f * FFN_CHUNK
        if f + n_buf - 1 < n_chunks:
            project_chunk(f + n_buf - 1)
        conv_gate(f % n_buf, c0)
    acc = _dot(a_ref[...], wdn_ref[...])
    for s in range(d // LANES):
        y_ref[s, pl.ds(0, half, stride=2), :] = acc[:half, s * LANES:(s + 1) * LANES]
        y_ref[s, pl.ds(1, half, stride=2), :] = acc[half:, s * LANES:(s + 1) * LANES]
    ffn = jnp.concatenate([y_ref[s] for s in range(d // LANES)], axis=1)
    y = x_ref[0] + g2_ref[0] * ffn
    if final_norm:
        y = _rms(y, fg_ref[...])
    o_ref[0] = y


def _ffn(x, shift, scale, gate, norm_g, w_up, conv_w, conv_b, w_down, final_g, *, layer, tm,
         final_norm):
    b, n, d = x.shape
    d_ff = w_down.shape[1]
    hb = tm // HALO
    last = n // HALO - 1
    vec = lambda bb, i: (bb, 0, 0)
    const2 = lambda bb, i: (0, 0)
    this_layer = lambda bb, i: (layer, 0, 0)
    resident = dict(pipeline_mode=pl.Buffered(1))
    return pl.pallas_call(
        functools.partial(_ffn_kernel, tm=tm, d_ff=d_ff, final_norm=final_norm),
        grid=(b, n // tm),
        in_specs=[
            pl.BlockSpec((1, tm, d), lambda bb, i: (bb, i, 0)),
            pl.BlockSpec((1, HALO, d), lambda bb, i: (bb, jnp.maximum(i * hb - 1, 0), 0)),
            pl.BlockSpec((1, HALO, d), lambda bb, i: (bb, jnp.minimum((i + 1) * hb, last), 0)),
            pl.BlockSpec((1, 1, d), vec),
            pl.BlockSpec((1, 1, d), vec),
            pl.BlockSpec((1, 1, d), vec),
            pl.BlockSpec((1, d), const2),
            pl.BlockSpec((None,) + w_up.shape[1:], this_layer, **resident),
            pl.BlockSpec(conv_w.shape, const2),
            pl.BlockSpec((1, 2 * d_ff), const2),
            pl.BlockSpec((None,) + w_down.shape[1:], this_layer, **resident),
            pl.BlockSpec((1, d), const2),
        ],
        out_specs=pl.BlockSpec((1, tm, d), lambda bb, i: (bb, i, 0)),
        out_shape=jax.ShapeDtypeStruct(x.shape, F32),
        scratch_shapes=[
            pltpu.VMEM((FFN_SLOTS, 2, FFN_CHUNK // LANES, tm + 2 * HALO, LANES), F32),
            pltpu.VMEM((tm, d_ff), BF16),
            pltpu.VMEM((d // LANES, tm, LANES), F32),
        ],
        compiler_params=_cparams(("parallel", "parallel")),
        name="conv_ffn",
    )(x, x, x, shift, scale, gate, norm_g, w_up, conv_w, conv_b.reshape(1, -1), w_down, final_g)


F1_NC = 8
F2_NK = 16


def _dft_tables(n_tok, cg):
    w = GRID_W
    idx = np.arange(w, dtype=np.float64)
    a64 = 2.0 * np.pi * np.outer(idx, idx) / w
    fc, fs = np.cos(a64), np.sin(a64)
    step1 = np.block([[fc, fs], [-fs, fc]])
    eye = np.eye(F2_NK)
    step3 = (np.kron(fc[:w // 2 + 1], eye), np.kron(fs[:w // 2 + 1], eye))
    atw = 2.0 * np.pi * np.outer(idx, idx) / n_tok
    tw_c = np.repeat(np.cos(atw)[:, :, None], 128, axis=2)
    tw_s = np.repeat(np.sin(atw)[:, :, None], 128, axis=2)
    ch = np.arange(cg, dtype=np.float64)
    ach = 2.0 * np.pi * np.outer(ch, ch) / cg
    chan = np.concatenate([np.cos(ach), -np.sin(ach)], axis=1)
    f = lambda t: jnp.asarray(t, F32)
    return f(chan), f(step1), f(step3[0]), f(step3[1]), f(tw_c), f(tw_s)


def _f1_kernel(x_ref, sh_ref, sc_ref, g_ref, chan_ref, s1_ref, twc_ref, tws_ref, ar_ref, ai_ref,
               u_ref, *, nc, d):
    w = GRID_W
    cg = d // FOURIER_GROUPS
    tiles = d // LANES
    gt = cg // LANES
    x = x_ref[0].reshape(w * nc, d)
    h = _norm_mod(x, g_ref[...], sh_ref[0], sc_ref[0]).astype(BF16)
    for g in range(FOURIER_GROUPS):
        u = _dot(h[:, g * cg:(g + 1) * cg], chan_ref[...])
        for t in range(2 * gt):
            part, tt = divmod(t, gt)
            u_ref[part * tiles + g * gt + tt] = u[:, t * LANES:(t + 1) * LANES]
    for c in range(nc):
        rows = pl.ds(c, w, stride=nc)
        ur = jnp.concatenate([u_ref[s, rows, :] for s in range(tiles)], axis=1)
        ui = jnp.concatenate([u_ref[tiles + s, rows, :] for s in range(tiles)], axis=1)
        a = _dot(s1_ref[...], jnp.concatenate([ur, ui], axis=0).astype(BF16))
        a_r, a_i = a[:w], a[w:]
        tc = jnp.tile(twc_ref[c], (1, tiles))
        ts = jnp.tile(tws_ref[c], (1, tiles))
        ar_ref[0, c] = (a_r * tc + a_i * ts).astype(BF16)
        ai_ref[0, c] = (a_i * tc - a_r * ts).astype(BF16)


def _f2_kernel(ar_ref, ai_ref, x_ref, g1_ref, kc_ref, ks_ref, w_ref, o_ref, *, nk, d, inv_norm):
    w = GRID_W
    a_r = ar_ref[0].reshape(w * nk, d)
    a_i = ai_ref[0].reshape(w * nk, d)
    p = _dot(kc_ref[...], a_r)
    q = _dot(ks_ref[...], a_i)
    diff = p - q
    mirrored = [diff[(w - k2) * nk:(w - k2 + 1) * nk] for k2 in range(w // 2 + 1, w)]
    z = jnp.concatenate([p + q] + mirrored, axis=0)
    y = _dot((z * inv_norm).astype(BF16), w_ref[...])
    o_ref[0] = x_ref[0] + g1_ref[0] * y.reshape(w, nk, d)


def _fourier(x, shift, scale, gate, norm_g, w_out):
    b, n, d = x.shape
    w = GRID_W
    assert n == w * w
    nc, nk = F1_NC, F2_NK
    cg = d // FOURIER_GROUPS
    chan, step1, kc, ks, tw_c, tw_s = _dft_tables(n, cg)
    chan, step1, kc, ks = (t.astype(BF16) for t in (chan, step1, kc, ks))
    x4 = x.reshape(b, w, w, d)
    vec = lambda bb, j: (bb, 0, 0)
    const2 = lambda bb, j: (0, 0)
    resident = dict(pipeline_mode=pl.Buffered(1))
    a_r, a_i = pl.pallas_call(
        functools.partial(_f1_kernel, nc=nc, d=d),
        grid=(b, w // nc),
        in_specs=[
            pl.BlockSpec((1, w, nc, d), lambda bb, j: (bb, 0, j, 0)),
            pl.BlockSpec((1, 1, d), vec),
            pl.BlockSpec((1, 1, d), vec),
            pl.BlockSpec((1, d), const2),
            pl.BlockSpec(chan.shape, const2),
            pl.BlockSpec(step1.shape, const2),
            pl.BlockSpec((nc, w, LANES), lambda bb, j: (j, 0, 0)),
            pl.BlockSpec((nc, w, LANES), lambda bb, j: (j, 0, 0)),
        ],
        out_specs=[pl.BlockSpec((1, nc, w, d), lambda bb, j: (bb, j, 0, 0))] * 2,
        out_shape=[jax.ShapeDtypeStruct((b, w, w, d), BF16)] * 2,
        scratch_shapes=[pltpu.VMEM((2 * d // LANES, w * nc, LANES), F32)],
        compiler_params=_cparams(("parallel", "parallel")),
        name="fourier_rows",
    )(x4, shift, scale, norm_g, chan, step1, tw_c, tw_s)
    out = pl.pallas_call(
        functools.partial(_f2_kernel, nk=nk, d=d, inv_norm=1.0 / float(np.sqrt(n * cg))),
        grid=(b, w // nk),
        in_specs=[
            pl.BlockSpec((1, w, nk, d), lambda bb, j: (bb, 0, j, 0)),
            pl.BlockSpec((1, w, nk, d), lambda bb, j: (bb, 0, j, 0)),
            pl.BlockSpec((1, w, nk, d), lambda bb, j: (bb, 0, j, 0)),
            pl.BlockSpec((1, 1, d), vec),
            pl.BlockSpec(kc.shape, const2, **resident),
            pl.BlockSpec(ks.shape, const2, **resident),
            pl.BlockSpec(w_out.shape, const2, **resident),
        ],
        out_specs=pl.BlockSpec((1, w, nk, d), lambda bb, j: (bb, 0, j, 0)),
        out_shape=jax.ShapeDtypeStruct((b, w, w, d), F32),
        compiler_params=_cparams(("parallel", "parallel")),
        name="fourier_cols",
    )(a_r, a_i, x4, gate, kc, ks, w_out)
    return out.reshape(b, n, d)


def _rope_tables(n):
    t = jnp.arange(n)
    row = (t // GRID_W).astype(F32)
    col = (t % GRID_W).astype(F32)
    pairs = HEAD_DIM_A // 4
    freqs = ROPE_THETA ** (-jnp.arange(pairs, dtype=F32) / pairs)
    ang = jnp.concatenate([row[:, None] * freqs, col[:, None] * freqs], axis=-1)
    cos, sin = jnp.cos(ang), jnp.sin(ang)
    return jnp.concatenate([cos, cos], axis=-1), jnp.concatenate([-sin, sin], axis=-1)


def kernel(x, c, ctx, c_ctx, mod_w, mod_b, norm1_g, norm2_g, attn_w_in, attn_w_out, q_norm_g,
           k_norm_g, na_rpb, fourier_w_out, ffn_w_up, ffn_conv_w, ffn_conv_b, ffn_w_down, final_g):
    b, n, d = x.shape
    depth = mod_w.shape[0]
    assert depth == 2 and b + 1 <= 8

    vec8 = jnp.zeros((8, d), F32).at[:b].set(c).at[b].set(c_ctx)
    mod = _adaln(vec8, mod_w, mod_b)
    term = lambda l, k: mod[l, :b, k * d:(k + 1) * d].reshape(b, 1, d)
    cterm = lambda l, k: jnp.broadcast_to(mod[l, b, k * d:(k + 1) * d].reshape(1, 1, d), (b, 1, d))

    perm = np.concatenate([np.arange(0, HEAD_DIM_A, 2), np.arange(1, HEAD_DIM_A, 2)])
    cols = np.arange(attn_w_in.shape[2])
    for base, nh in ((0, HEADS_A), (QA_COLS + B_COLS, KV_HEADS_A)):
        for hd in range(nh):
            s0 = base + hd * HEAD_DIM_A
            cols[s0:s0 + HEAD_DIM_A] = s0 + perm
    w_in = attn_w_in[0][:, cols].astype(BF16)
    w_up = _to_bf16(ffn_w_up)
    w_down = _to_bf16(ffn_w_down)
    qg = q_norm_g[0][perm].reshape(1, -1)
    kg = k_norm_g[0][perm].reshape(1, -1)
    n_ctx = ctx.shape[1]
    no_rope = (jnp.ones((n_ctx, HEAD_DIM_A), F32), jnp.zeros((n_ctx, HEAD_DIM_A), F32))
    g1row = norm1_g[0].reshape(1, d)

    qa, qb, ka, va, kb, vb = _proj(x, term(0, 0), term(0, 1), g1row, w_in, _rope_tables(n), qg, kg,
                                   with_q=True, tm=PROJ_TM, name="proj_latent")
    cka, cva, ckb, cvb = _proj(ctx, cterm(0, 0), cterm(0, 1), g1row, w_in, no_rope, qg, kg,
                               with_q=False, tm=n_ctx, name="proj_context")
    oa = _gqa(qa, ka, va, cka, cva, tq=512, tk=512)
    ob = _na(qb, kb, vb, ckb, cvb, _na_bias_table(na_rpb[0]))
    x = _outproj(x, oa, ob, _to_bf16(attn_w_out)[0], term(0, 2), tm=1024)
    x = _ffn(x, term(0, 3), term(0, 4), term(0, 5), norm2_g[0].reshape(1, d), w_up,
             ffn_conv_w[0], ffn_conv_b[0], w_down, final_g.reshape(1, d),
             layer=0, tm=FFN_TM, final_norm=False)

    x = _fourier(x, term(1, 0), term(1, 1), term(1, 2), norm1_g[1].reshape(1, d),
                 _to_bf16(fourier_w_out)[0])
    x = _ffn(x, term(1, 3), term(1, 4), term(1, 5), norm2_g[1].reshape(1, d), w_up,
             ffn_conv_w[1], ffn_conv_b[1], w_down, final_g.reshape(1, d),
             layer=1, tm=FFN_TM, final_norm=True)
    return x
```

```python
import functools

import numpy as np
import jax
import jax.numpy as jnp
from jax import lax
from jax.experimental import pallas as pl
from jax.experimental.pallas import tpu as pltpu

F32 = jnp.float32
BF16 = jnp.bfloat16

GRID_W = 64
HEAD_DIM_A = 128
HEADS_A = 4
KV_HEADS_A = 2
HEAD_DIM_B = 64
HEADS_B = 8
NA_ROWS = 8
NA_COLS = 16
FOURIER_GROUPS = 4
ROPE_THETA = 10000.0
EPS = 1e-6
QA_COLS = HEADS_A * HEAD_DIM_A
KA_COLS = KV_HEADS_A * HEAD_DIM_A
B_COLS = HEADS_B * HEAD_DIM_B
NEG = -1e30
LOG2E = 1.4426950408889634
LANES = 128

VMEM_LIMIT = 56 * 1024 * 1024


def _cparams(sem):
    return pltpu.CompilerParams(dimension_semantics=sem, vmem_limit_bytes=VMEM_LIMIT)


def _rms(x, g):
    ms = jnp.mean(x * x, axis=-1, keepdims=True)
    return x * lax.rsqrt(ms + EPS) * g


def _norm_mod(x, g, shift, scale):
    return _rms(x, g) * (1.0 + scale) + shift


def _dot(a, b):
    return jnp.dot(a, b, preferred_element_type=F32)


def _dot_t(a, b):
    return lax.dot_general(a, b, (((1,), (1,)), ((), ())), preferred_element_type=F32)


CAST_ROW_BLOCKS = 4


def _cast_kernel(w_ref, o_ref):
    o_ref[...] = w_ref[...].astype(o_ref.dtype)


def _to_bf16(w):
    layers, rows, cols = w.shape
    rb = rows // CAST_ROW_BLOCKS
    spec = pl.BlockSpec((1, rb, cols), lambda l, r: (l, r, 0))
    return pl.pallas_call(
        _cast_kernel,
        grid=(layers, CAST_ROW_BLOCKS),
        in_specs=[spec],
        out_specs=spec,
        out_shape=jax.ShapeDtypeStruct(w.shape, BF16),
        compiler_params=_cparams(("parallel", "parallel")),
        name="cast_bf16",
    )(w)


def _w_in_kernel(w_ref, p_ref, o_ref):
    j = pl.program_id(0)
    q_blocks = (QA_COLS + B_COLS) // HEAD_DIM_A
    rotary = (j < HEADS_A) | ((j >= q_blocks) & (j < q_blocks + KV_HEADS_A))
    w = w_ref[...].astype(BF16)

    @pl.when(rotary)
    def _():
        o_ref[...] = _dot(w, p_ref[...]).astype(BF16)

    @pl.when(jnp.logical_not(rotary))
    def _():
        o_ref[...] = w


def _prep_w_in(attn_w_in, perm):
    _, d, ncol = attn_w_in.shape
    p = np.zeros((HEAD_DIM_A, HEAD_DIM_A), np.float32)
    p[perm, np.arange(HEAD_DIM_A)] = 1.0
    return pl.pallas_call(
        _w_in_kernel,
        grid=(ncol // HEAD_DIM_A,),
        in_specs=[
            pl.BlockSpec((None, d, HEAD_DIM_A), lambda j: (0, 0, j)),
            pl.BlockSpec((HEAD_DIM_A, HEAD_DIM_A), lambda j: (0, 0)),
        ],
        out_specs=pl.BlockSpec((d, HEAD_DIM_A), lambda j: (0, j)),
        out_shape=jax.ShapeDtypeStruct((d, ncol), BF16),
        compiler_params=_cparams(("parallel",)),
        name="prep_w_in",
    )(attn_w_in, jnp.asarray(p).astype(BF16))


def _adaln_kernel(v_ref, w_ref, b_ref, o_ref):
    v = v_ref[...]
    s = v * jax.nn.sigmoid(v)
    o_ref[0] = jnp.dot(s, w_ref[0], preferred_element_type=F32,
                       precision=lax.Precision.HIGHEST) + b_ref[0]


def _adaln(vec8, mod_w, mod_b):
    depth, d, n = mod_w.shape
    tn = 1536
    return pl.pallas_call(
        _adaln_kernel,
        grid=(depth, n // tn),
        in_specs=[
            pl.BlockSpec((8, d), lambda l, j: (0, 0)),
            pl.BlockSpec((1, d, tn), lambda l, j: (l, 0, j)),
            pl.BlockSpec((1, 1, tn), lambda l, j: (l, 0, j)),
        ],
        out_specs=pl.BlockSpec((1, 8, tn), lambda l, j: (l, 0, j)),
        out_shape=jax.ShapeDtypeStruct((depth, 8, n), F32),
        compiler_params=_cparams(("arbitrary", "arbitrary")),
        name="adaln",
    )(vec8, mod_w, mod_b.reshape(depth, 1, n))


PROJ_TM = 512


def _proj_kernel(x_ref, sh_ref, sc_ref, g_ref, w_ref, cos_ref, sin_ref, qg_ref, kg_ref,
                 *out_refs, with_q):
    h = _norm_mod(x_ref[0], g_ref[...], sh_ref[0], sc_ref[0]).astype(BF16)
    cosf = cos_ref[...]
    sinf = sin_ref[...]

    def rope_heads(p, n_heads, gain, scale, o_ref):
        for hd in range(n_heads):
            t = _rms(p[:, hd * HEAD_DIM_A:(hd + 1) * HEAD_DIM_A], gain)
            t = t * cosf + pltpu.roll(t, HEAD_DIM_A // 2, axis=1) * sinf
            o_ref[0, :, hd * HEAD_DIM_A:(hd + 1) * HEAD_DIM_A] = (t * scale).astype(BF16)

    col = QA_COLS + B_COLS
    if with_q:
        qa_ref, qb_ref, ka_ref, va_ref, kb_ref, vb_ref = out_refs
        rope_heads(_dot(h, w_ref[:, 0:QA_COLS]), HEADS_A, qg_ref[...], HEAD_DIM_A ** -0.5 * LOG2E,
                   qa_ref)
        qb = _dot(h, w_ref[:, QA_COLS:QA_COLS + B_COLS])
        qb_ref[0] = (qb * (HEAD_DIM_B ** -0.5 * LOG2E)).astype(BF16)
    else:
        ka_ref, va_ref, kb_ref, vb_ref = out_refs
    rope_heads(_dot(h, w_ref[:, col:col + KA_COLS]), KV_HEADS_A, kg_ref[...], 1.0, ka_ref)
    col += KA_COLS
    va_ref[0] = _dot(h, w_ref[:, col:col + KA_COLS]).astype(BF16)
    col += KA_COLS
    kb_ref[0] = _dot(h, w_ref[:, col:col + B_COLS]).astype(BF16)
    col += B_COLS
    vb_ref[0] = _dot(h, w_ref[:, col:col + B_COLS]).astype(BF16)


def _proj(x, shift, scale, gain, w, rope, qg, kg, *, with_q, tm, name):
    b, n, d = x.shape
    ncol = w.shape[1]
    widths = ([QA_COLS, B_COLS] if with_q else []) + [KA_COLS, KA_COLS, B_COLS, B_COLS]
    vec = lambda bb, i: (bb, 0, 0)
    return pl.pallas_call(
        functools.partial(_proj_kernel, with_q=with_q),
        grid=(b, n // tm),
        in_specs=[
            pl.BlockSpec((1, tm, d), lambda bb, i: (bb, i, 0)),
            pl.BlockSpec((1, 1, d), vec),
            pl.BlockSpec((1, 1, d), vec),
            pl.BlockSpec((1, d), lambda bb, i: (0, 0)),
            pl.BlockSpec((d, ncol), lambda bb, i: (0, 0)),
            pl.BlockSpec((tm, HEAD_DIM_A), lambda bb, i: (i, 0)),
            pl.BlockSpec((tm, HEAD_DIM_A), lambda bb, i: (i, 0)),
            pl.BlockSpec((1, HEAD_DIM_A), lambda bb, i: (0, 0)),
            pl.BlockSpec((1, HEAD_DIM_A), lambda bb, i: (0, 0)),
        ],
        out_specs=[pl.BlockSpec((1, tm, wd), lambda bb, i: (bb, i, 0)) for wd in widths],
        out_shape=[jax.ShapeDtypeStruct((b, n, wd), BF16) for wd in widths],
        compiler_params=_cparams(("parallel", "parallel")),
        name=name,
    )(x, shift, scale, gain, w, *rope, qg, kg)


GQA_UNROLL = 8


def _flash_update(q, k, v, m_ref, l_ref, acc_ref):
    d = v.shape[1]
    s = _dot_t(q, k)
    m_old = m_ref[...]
    m_new = jnp.maximum(m_old, jnp.max(s, axis=-1, keepdims=True))
    alpha = jnp.exp2(m_old - m_new)
    p = jnp.exp2(s - jnp.tile(m_new, (1, s.shape[1] // m_new.shape[1])))
    o = _dot(p.astype(BF16), jnp.concatenate([v, jnp.ones_like(v)], axis=1))
    acc_ref[...] = alpha * acc_ref[...] + o[:, :d]
    l_ref[...] = alpha * l_ref[...] + o[:, d:]
    m_ref[...] = m_new


def _gqa_kernel(q_ref, k_ref, v_ref, ck_ref, cv_ref, o_ref, m_ref, l_ref, acc_ref, *, tk):
    group = q_ref.shape[2] // HEAD_DIM_A
    m_ref[...] = jnp.full(m_ref.shape, NEG, F32)
    l_ref[...] = jnp.zeros(l_ref.shape, F32)
    acc_ref[...] = jnp.zeros(acc_ref.shape, F32)

    def update(k, v):
        for g in range(group):
            _flash_update(q_ref[0, :, g * HEAD_DIM_A:(g + 1) * HEAD_DIM_A], k, v,
                          m_ref.at[g], l_ref.at[g], acc_ref.at[g])

    def body(i, carry):
        off = pl.multiple_of(i * tk, tk)
        update(k_ref[0, pl.ds(off, tk), :], v_ref[0, pl.ds(off, tk), :])
        return carry

    lax.fori_loop(0, k_ref.shape[1] // tk, body, 0, unroll=GQA_UNROLL)
    update(ck_ref[0], cv_ref[0])
    o_ref[0] = jnp.concatenate([acc_ref[g] / l_ref[g] for g in range(group)], axis=1).astype(BF16)


def _gqa(qa, ka, va, cka, cva, *, tq, tk):
    b, n, _ = qa.shape
    c = cka.shape[1]
    group = HEADS_A // KV_HEADS_A
    gw = group * HEAD_DIM_A
    return pl.pallas_call(
        functools.partial(_gqa_kernel, tk=tk),
        grid=(b, KV_HEADS_A, n // tq),
        in_specs=[
            pl.BlockSpec((1, tq, gw), lambda bb, h, i: (bb, i, h)),
            pl.BlockSpec((1, n, HEAD_DIM_A), lambda bb, h, i: (bb, 0, h)),
            pl.BlockSpec((1, n, HEAD_DIM_A), lambda bb, h, i: (bb, 0, h)),
            pl.BlockSpec((1, c, HEAD_DIM_A), lambda bb, h, i: (bb, 0, h)),
            pl.BlockSpec((1, c, HEAD_DIM_A), lambda bb, h, i: (bb, 0, h)),
        ],
        out_specs=pl.BlockSpec((1, tq, gw), lambda bb, h, i: (bb, i, h)),
        out_shape=jax.ShapeDtypeStruct((b, n, QA_COLS), BF16),
        scratch_shapes=[
            pltpu.VMEM((group, tq, HEAD_DIM_A), F32),
            pltpu.VMEM((group, tq, HEAD_DIM_A), F32),
            pltpu.VMEM((group, tq, HEAD_DIM_A), F32),
        ],
        compiler_params=_cparams(("parallel", "parallel", "arbitrary")),
        name="gqa",
    )(qa, ka, va, cka, cva)


NA_UNROLL = 32


def _na_kernel(q_ref, k_ref, v_ref, ck_ref, cv_ref, bias_ref, o_ref):
    rows = q_ref.shape[1] // GRID_W
    win = NA_ROWS * GRID_W
    lane = lax.broadcasted_iota(jnp.int32, (GRID_W, 2 * HEAD_DIM_B), 1)
    lo = lane < HEAD_DIM_B
    pw = 2 * HEAD_DIM_B

    def with_ones(v):
        return jnp.concatenate([v, jnp.ones_like(v)], axis=1)

    def body(r, carry):
        r0 = jnp.clip(r - NA_ROWS // 2, 0, rows - NA_ROWS)
        q = q_ref[0, pl.ds(pl.multiple_of(r * GRID_W, GRID_W), GRID_W), :]
        zero = jnp.zeros_like(q)
        qq = jnp.concatenate([jnp.where(lo, q, zero), jnp.where(lo, zero, q)], axis=0)
        koff = pl.multiple_of(r0 * GRID_W, GRID_W)
        s = _dot_t(qq, k_ref[0, pl.ds(koff, win), :]) + bias_ref[r - r0]
        sc = _dot_t(qq, ck_ref[0])
        m = jnp.maximum(jnp.max(s, axis=-1, keepdims=True), jnp.max(sc, axis=-1, keepdims=True))
        p = jnp.exp2(s - m).astype(BF16)
        pc = jnp.exp2(sc - m).astype(BF16)
        o = _dot(p, with_ones(v_ref[0, pl.ds(koff, win), :])) + _dot(pc, with_ones(cv_ref[0]))
        o = o[:, :pw] / o[:, pw:]
        res = jnp.where(lo, o[:GRID_W], o[GRID_W:])
        o_ref[0, pl.ds(pl.multiple_of(r * GRID_W, GRID_W), GRID_W), :] = res.astype(BF16)
        return carry

    lax.fori_loop(0, rows, body, 0, unroll=NA_UNROLL)


def _na(qb, kb, vb, ckb, cvb, bias):
    b, n, _ = qb.shape
    c = ckb.shape[1]
    pw = 2 * HEAD_DIM_B
    n_cls = bias.shape[0]
    tok = lambda bb, h: (bb, 0, h)
    return pl.pallas_call(
        _na_kernel,
        grid=(b, HEADS_B // 2),
        in_specs=[
            pl.BlockSpec((1, n, pw), tok),
            pl.BlockSpec((1, n, pw), tok),
            pl.BlockSpec((1, n, pw), tok),
            pl.BlockSpec((1, c, pw), tok),
            pl.BlockSpec((1, c, pw), tok),
            pl.BlockSpec((n_cls, 2 * GRID_W, NA_ROWS * GRID_W), lambda bb, h: (0, h, 0)),
        ],
        out_specs=pl.BlockSpec((1, n, pw), tok),
        out_shape=jax.ShapeDtypeStruct((b, n, B_COLS), BF16),
        compiler_params=_cparams(("parallel", "parallel")),
        name="natten",
    )(qb, kb, vb, ckb, cvb, bias)


def _na_bias_table(rpb):
    edge = GRID_W - NA_COLS
    n_off = rpb.shape[1]
    padded = jnp.pad(rpb, ((0, 0), (0, 1), (edge, 2 * GRID_W - rpb.shape[2] - edge)))
    return pl.pallas_call(
        functools.partial(_na_bias_kernel, edge=edge),
        grid=(NA_ROWS,),
        in_specs=[pl.BlockSpec((HEADS_B, n_off + 1, 2 * GRID_W), lambda s: (0, 0, 0))],
        out_specs=pl.BlockSpec((1, HEADS_B * GRID_W, NA_ROWS * GRID_W), lambda s: (s, 0, 0)),
        out_shape=jax.ShapeDtypeStruct((NA_ROWS, HEADS_B * GRID_W, NA_ROWS * GRID_W), F32),
        compiler_params=_cparams(("parallel",)),
        name="na_bias",
    )(padded)


def _na_bias_kernel(rp_ref, o_ref, *, edge):
    cls = pl.program_id(0)
    shape = (GRID_W, 2 * GRID_W)
    c = lax.broadcasted_iota(jnp.int32, shape, 0)
    lane = lax.broadcasted_iota(jnp.int32, shape, 1)
    lo = lane < GRID_W
    j = jnp.where(lo, lane, lane - GRID_W)
    cs = jnp.clip(c - NA_COLS // 2, 0, edge)
    valid = (j >= cs) & (j < cs + NA_COLS)

    def toeplitz(h, i, shift):
        row = rp_ref[h, pl.ds(i - cls + NA_ROWS - 1, 1), :]
        return pltpu.roll(jnp.broadcast_to(row, shape), shift, 1, stride=1, stride_axis=0)

    for h in range(HEADS_B):
        for pair in range(NA_ROWS // 2):
            t = jnp.where(lo, toeplitz(h, 2 * pair, GRID_W + 1), toeplitz(h, 2 * pair + 1, 1))
            o_ref[0, h * GRID_W:(h + 1) * GRID_W, pair * 2 * GRID_W:(pair + 1) * 2 * GRID_W] = (
                jnp.where(valid, t * LOG2E, NEG))


HALO = 8
MIX_HALO = 16
FFN_CHUNK = 256
FFN_TM = 512
FFN_SLOTS = 3
FFN_ROWS = 32


def _ffn_kernel(x_ref, xp_ref, xn_ref, sh_ref, sc_ref, g2_ref, ng_ref, wup_ref, cw_ref, cb_ref,
                wdn_ref, fg_ref, *rest, tm, d_ff, final_norm, mixer):
    if mixer:
        (oa_ref, oap_ref, oan_ref, ob_ref, obp_ref, obn_ref, wo_ref, g1_ref,
         o_ref, u_ref, a_ref, y_ref) = rest
        ka = oa_ref.shape[2]
        rows = lambda main, prev, nxt: jnp.concatenate([main[0], prev[0], nxt[0]], axis=0)
        mix = (_dot(rows(oa_ref, oap_ref, oan_ref), wo_ref[0:ka, :])
               + _dot(rows(ob_ref, obp_ref, obn_ref), wo_ref[ka:, :]))
        g1 = g1_ref[0]
        lo = tm + MIX_HALO - HALO
        x_tile = x_ref[0] + g1 * mix[:tm]
        x_prev = xp_ref[0] + g1 * mix[lo:lo + HALO]
        x_next = xn_ref[0] + g1 * mix[tm + MIX_HALO:tm + MIX_HALO + HALO]
        o_ref[0] = x_tile
    else:
        o_ref, u_ref, a_ref, y_ref = rest
        x_tile, x_prev, x_next = x_ref[0], xp_ref[0], xn_ref[0]
    i = pl.program_id(1)
    ng, sh, sc = ng_ref[...], sh_ref[0], sc_ref[0]
    d = x_ref.shape[2]
    half = tm // 2
    slabs = FFN_CHUNK // LANES

    hp = jnp.where(i > 0, _norm_mod(x_prev, ng, sh, sc), 0.0)
    hn = jnp.where(i < pl.num_programs(1) - 1, _norm_mod(x_next, ng, sh, sc), 0.0)
    h = jnp.concatenate([hp, _norm_mod(x_tile, ng, sh, sc), hn], axis=0).astype(BF16)

    def project(buf, part, col):
        u = _dot(h, wup_ref[:, col:col + FFN_CHUNK])
        for s in range(slabs):
            u_ref[buf, part, s] = u[:, s * LANES:(s + 1) * LANES]

    def conv_gate(buf, c0):
        for s in range(slabs):
            cols = (c0 + s * LANES, d_ff + c0 + s * LANES)
            taps_w = [[cw_ref[k:k + 1, c:c + LANES] for k in range(3)] for c in cols]
            bias = [cb_ref[:, c:c + LANES] for c in cols]
            for rb in range(half // FFN_ROWS):
                res = []
                for part in range(2):
                    t = [u_ref[buf, part, s, pl.ds(HALO - 1 + k + 2 * rb * FFN_ROWS, FFN_ROWS, stride=2), :]
                         for k in range(4)]
                    w0, w1, w2 = taps_w[part]
                    res.append((t[0] * w0 + t[1] * w1 + t[2] * w2 + bias[part],
                                t[1] * w0 + t[2] * w1 + t[3] * w2 + bias[part]))
                (g_even, g_odd), (v_even, v_odd) = res
                r0 = rb * FFN_ROWS
                col = slice(c0 + s * LANES, c0 + (s + 1) * LANES)
                a_ref[r0:r0 + FFN_ROWS, col] = (g_even * jax.nn.sigmoid(g_even) * v_even).astype(BF16)
                a_ref[half + r0:half + r0 + FFN_ROWS, col] = (
                    g_odd * jax.nn.sigmoid(g_odd) * v_odd).astype(BF16)

    n_chunks = d_ff // FFN_CHUNK
    n_buf = u_ref.shape[0]

    def project_chunk(f):
        project(f % n_buf, 0, f * FFN_CHUNK)
        project(f % n_buf, 1, d_ff + f * FFN_CHUNK)

    for f in range(min(n_buf - 1, n_chunks)):
        project_chunk(f)
    for f in range(n_chunks):
        c0 = f * FFN_CHUNK
        if f + n_buf - 1 < n_chunks:
            project_chunk(f + n_buf - 1)
        conv_gate(f % n_buf, c0)
    acc = _dot(a_ref[...], wdn_ref[...])
    for s in range(d // LANES):
        y_ref[s, pl.ds(0, half, stride=2), :] = acc[:half, s * LANES:(s + 1) * LANES]
        y_ref[s, pl.ds(1, half, stride=2), :] = acc[half:, s * LANES:(s + 1) * LANES]
    ffn = jnp.concatenate([y_ref[s] for s in range(d // LANES)], axis=1)
    y = (o_ref[0] if mixer else x_ref[0]) + g2_ref[0] * ffn
    if final_norm:
        y = _rms(y, fg_ref[...])
    o_ref[0] = y


def _halo_specs(width, tm, n, rows):
    hb = tm // rows
    last = n // rows - 1
    return [
        pl.BlockSpec((1, tm, width), lambda bb, i: (bb, i, 0)),
        pl.BlockSpec((1, rows, width), lambda bb, i: (bb, jnp.maximum(i * hb - 1, 0), 0)),
        pl.BlockSpec((1, rows, width), lambda bb, i: (bb, jnp.minimum((i + 1) * hb, last), 0)),
    ]


def _ffn(x, shift, scale, gate, norm_g, w_up, conv_w, conv_b, w_down, final_g, *, layer, tm,
         final_norm, mixer=None):
    b, n, d = x.shape
    d_ff = w_down.shape[1]
    vec = lambda bb, i: (bb, 0, 0)
    const2 = lambda bb, i: (0, 0)
    this_layer = lambda bb, i: (layer, 0, 0)
    resident = dict(pipeline_mode=pl.Buffered(1))
    mix_specs, mix_args = [], []
    if mixer is not None:
        oa, ob, w_out, g1 = mixer
        for o in (oa, ob):
            mix_specs += _halo_specs(o.shape[2], tm, n, MIX_HALO)
            mix_args += [o, o, o]
        mix_specs += [pl.BlockSpec(w_out.shape, const2, **resident), pl.BlockSpec((1, 1, d), vec)]
        mix_args += [w_out, g1]
    return pl.pallas_call(
        functools.partial(_ffn_kernel, tm=tm, d_ff=d_ff, final_norm=final_norm,
                          mixer=mixer is not None),
        grid=(b, n // tm),
        in_specs=_halo_specs(d, tm, n, HALO) + [
            pl.BlockSpec((1, 1, d), vec),
            pl.BlockSpec((1, 1, d), vec),
            pl.BlockSpec((1, 1, d), vec),
            pl.BlockSpec((1, d), const2),
            pl.BlockSpec((None,) + w_up.shape[1:], this_layer, **resident),
            pl.BlockSpec(conv_w.shape, const2),
            pl.BlockSpec((1, 2 * d_ff), const2),
            pl.BlockSpec((None,) + w_down.shape[1:], this_layer, **resident),
            pl.BlockSpec((1, d), const2),
        ] + mix_specs,
        out_specs=pl.BlockSpec((1, tm, d), lambda bb, i: (bb, i, 0)),
        out_shape=jax.ShapeDtypeStruct(x.shape, F32),
        scratch_shapes=[
            pltpu.VMEM((FFN_SLOTS, 2, FFN_CHUNK // LANES, tm + 2 * HALO, LANES), F32),
            pltpu.VMEM((tm, d_ff), BF16),
            pltpu.VMEM((d // LANES, tm, LANES), F32),
        ],
        compiler_params=_cparams(("parallel", "parallel")),
        name="conv_ffn",
    )(x, x, x, shift, scale, gate, norm_g, w_up, conv_w, conv_b.reshape(1, -1), w_down, final_g,
      *mix_args)


F1_NC = 8
F2_NK = 16


def _dft_tables(n_tok, cg):
    w = GRID_W
    idx = np.arange(w, dtype=np.float64)
    a64 = 2.0 * np.pi * np.outer(idx, idx) / w
    fc, fs = np.cos(a64), np.sin(a64)
    step1 = np.block([[fc, fs], [-fs, fc]])
    eye = np.eye(F2_NK)
    step3 = (np.kron(fc[:w // 2 + 1], eye), np.kron(fs[:w // 2 + 1], eye))
    atw = 2.0 * np.pi * np.outer(idx, idx) / n_tok
    tw_c = np.repeat(np.cos(atw)[:, :, None], 128, axis=2)
    tw_s = np.repeat(np.sin(atw)[:, :, None], 128, axis=2)
    ch = np.arange(cg, dtype=np.float64)
    ach = 2.0 * np.pi * np.outer(ch, ch) / cg
    chan = np.concatenate([np.cos(ach), -np.sin(ach)], axis=1)
    f = lambda t: jnp.asarray(t, F32)
    return f(chan), f(step1), f(step3[0]), f(step3[1]), f(tw_c), f(tw_s)


def _f1_kernel(x_ref, sh_ref, sc_ref, g_ref, chan_ref, s1_ref, twc_ref, tws_ref, ar_ref, ai_ref,
               u_ref, *, nc, d):
    w = GRID_W
    cg = d // FOURIER_GROUPS
    tiles = d // LANES
    gt = cg // LANES
    x = x_ref[0].reshape(w * nc, d)
    h = _norm_mod(x, g_ref[...], sh_ref[0], sc_ref[0]).astype(BF16)
    for g in range(FOURIER_GROUPS):
        u = _dot(h[:, g * cg:(g + 1) * cg], chan_ref[...])
        for t in range(2 * gt):
            part, tt = divmod(t, gt)
            u_ref[part * tiles + g * gt + tt] = u[:, t * LANES:(t + 1) * LANES]
    for c in range(nc):
        rows = pl.ds(c, w, stride=nc)
        ur = jnp.concatenate([u_ref[s, rows, :] for s in range(tiles)], axis=1)
        ui = jnp.concatenate([u_ref[tiles + s, rows, :] for s in range(tiles)], axis=1)
        a = _dot(s1_ref[...], jnp.concatenate([ur, ui], axis=0).astype(BF16))
        a_r, a_i = a[:w], a[w:]
        tc = jnp.tile(twc_ref[c], (1, tiles))
        ts = jnp.tile(tws_ref[c], (1, tiles))
        ar_ref[0, c] = (a_r * tc + a_i * ts).astype(BF16)
        ai_ref[0, c] = (a_i * tc - a_r * ts).astype(BF16)


def _f2_kernel(ar_ref, ai_ref, x_ref, g1_ref, kc_ref, ks_ref, w_ref, o_ref, *, nk, d, inv_norm):
    w = GRID_W
    a_r = ar_ref[0].reshape(w * nk, d)
    a_i = ai_ref[0].reshape(w * nk, d)
    p = _dot(kc_ref[...], a_r)
    q = _dot(ks_ref[...], a_i)
    diff = p - q
    mirrored = [diff[(w - k2) * nk:(w - k2 + 1) * nk] for k2 in range(w // 2 + 1, w)]
    z = jnp.concatenate([p + q] + mirrored, axis=0)
    y = _dot((z * inv_norm).astype(BF16), w_ref[...])
    o_ref[0] = x_ref[0] + g1_ref[0] * y.reshape(w, nk, d)


def _fourier(x, shift, scale, gate, norm_g, w_out):
    b, n, d = x.shape
    w = GRID_W
    assert n == w * w
    nc, nk = F1_NC, F2_NK
    cg = d // FOURIER_GROUPS
    chan, step1, kc, ks, tw_c, tw_s = _dft_tables(n, cg)
    chan, step1, kc, ks = (t.astype(BF16) for t in (chan, step1, kc, ks))
    x4 = x.reshape(b, w, w, d)
    vec = lambda bb, j: (bb, 0, 0)
    const2 = lambda bb, j: (0, 0)
    resident = dict(pipeline_mode=pl.Buffered(1))
    a_r, a_i = pl.pallas_call(
        functools.partial(_f1_kernel, nc=nc, d=d),
        grid=(b, w // nc),
        in_specs=[
            pl.BlockSpec((1, w, nc, d), lambda bb, j: (bb, 0, j, 0)),
            pl.BlockSpec((1, 1, d), vec),
            pl.BlockSpec((1, 1, d), vec),
            pl.BlockSpec((1, d), const2),
            pl.BlockSpec(chan.shape, const2),
            pl.BlockSpec(step1.shape, const2),
            pl.BlockSpec((nc, w, LANES), lambda bb, j: (j, 0, 0)),
            pl.BlockSpec((nc, w, LANES), lambda bb, j: (j, 0, 0)),
        ],
        out_specs=[pl.BlockSpec((1, nc, w, d), lambda bb, j: (bb, j, 0, 0))] * 2,
        out_shape=[jax.ShapeDtypeStruct((b, w, w, d), BF16)] * 2,
        scratch_shapes=[pltpu.VMEM((2 * d // LANES, w * nc, LANES), F32)],
        compiler_params=_cparams(("parallel", "parallel")),
        name="fourier_rows",
    )(x4, shift, scale, norm_g, chan, step1, tw_c, tw_s)
    out = pl.pallas_call(
        functools.partial(_f2_kernel, nk=nk, d=d, inv_norm=1.0 / float(np.sqrt(n * cg))),
        grid=(b, w // nk),
        in_specs=[
            pl.BlockSpec((1, w, nk, d), lambda bb, j: (bb, 0, j, 0)),
            pl.BlockSpec((1, w, nk, d), lambda bb, j: (bb, 0, j, 0)),
            pl.BlockSpec((1, w, nk, d), lambda bb, j: (bb, 0, j, 0)),
            pl.BlockSpec((1, 1, d), vec),
            pl.BlockSpec(kc.shape, const2, **resident),
            pl.BlockSpec(ks.shape, const2, **resident),
            pl.BlockSpec(w_out.shape, const2, **resident),
        ],
        out_specs=pl.BlockSpec((1, w, nk, d), lambda bb, j: (bb, 0, j, 0)),
        out_shape=jax.ShapeDtypeStruct((b, w, w, d), F32),
        compiler_params=_cparams(("parallel", "parallel")),
        name="fourier_cols",
    )(a_r, a_i, x4, gate, kc, ks, w_out)
    return out.reshape(b, n, d)


def _rope_tables(n):
    t = jnp.arange(n)
    row = (t // GRID_W).astype(F32)
    col = (t % GRID_W).astype(F32)
    pairs = HEAD_DIM_A // 4
    freqs = ROPE_THETA ** (-jnp.arange(pairs, dtype=F32) / pairs)
    ang = jnp.concatenate([row[:, None] * freqs, col[:, None] * freqs], axis=-1)
    cos, sin = jnp.cos(ang), jnp.sin(ang)
    return jnp.concatenate([cos, cos], axis=-1), jnp.concatenate([-sin, sin], axis=-1)


def kernel(x, c, ctx, c_ctx, mod_w, mod_b, norm1_g, norm2_g, attn_w_in, attn_w_out, q_norm_g,
           k_norm_g, na_rpb, fourier_w_out, ffn_w_up, ffn_conv_w, ffn_conv_b, ffn_w_down, final_g):
    b, n, d = x.shape
    depth = mod_w.shape[0]
    assert depth == 2 and b + 1 <= 8

    vec8 = jnp.zeros((8, d), F32).at[:b].set(c).at[b].set(c_ctx)
    mod = _adaln(vec8, mod_w, mod_b)
    term = lambda l, k: mod[l, :b, k * d:(k + 1) * d].reshape(b, 1, d)
    cterm = lambda l, k: jnp.broadcast_to(mod[l, b, k * d:(k + 1) * d].reshape(1, 1, d), (b, 1, d))

    perm = np.concatenate([np.arange(0, HEAD_DIM_A, 2), np.arange(1, HEAD_DIM_A, 2)])
    w_in = _prep_w_in(attn_w_in, perm)
    w_up = _to_bf16(ffn_w_up)
    w_down = _to_bf16(ffn_w_down)
    qg = q_norm_g[0][perm].reshape(1, -1)
    kg = k_norm_g[0][perm].reshape(1, -1)
    n_ctx = ctx.shape[1]
    no_rope = (jnp.ones((n_ctx, HEAD_DIM_A), F32), jnp.zeros((n_ctx, HEAD_DIM_A), F32))
    g1row = norm1_g[0].reshape(1, d)

    qa, qb, ka, va, kb, vb = _proj(x, term(0, 0), term(0, 1), g1row, w_in, _rope_tables(n), qg, kg,
                                   with_q=True, tm=PROJ_TM, name="proj_latent")
    cka, cva, ckb, cvb = _proj(ctx, cterm(0, 0), cterm(0, 1), g1row, w_in, no_rope, qg, kg,
                               with_q=False, tm=n_ctx, name="proj_context")
    oa = _gqa(qa, ka, va, cka, cva, tq=1024, tk=512)
    ob = _na(qb, kb, vb, ckb, cvb, _na_bias_table(na_rpb[0]))
    x = _ffn(x, term(0, 3), term(0, 4), term(0, 5), norm2_g[0].reshape(1, d), w_up,
             ffn_conv_w[0], ffn_conv_b[0], w_down, final_g.reshape(1, d),
             layer=0, tm=FFN_TM, final_norm=False,
             mixer=(oa, ob, _to_bf16(attn_w_out)[0], term(0, 2)))

    x = _fourier(x, term(1, 0), term(1, 1), term(1, 2), norm1_g[1].reshape(1, d),
                 _to_bf16(fourier_w_out)[0])
    x = _ffn(x, term(1, 3), term(1, 4), term(1, 5), norm2_g[1].reshape(1, d), w_up,
             ffn_conv_w[1], ffn_conv_b[1], w_down, final_g.reshape(1, d),
             layer=1, tm=FFN_TM, final_norm=True)
    return x
```

```python
import functools

import numpy as np
import jax
import jax.numpy as jnp
from jax import lax
from jax.experimental import pallas as pl
from jax.experimental.pallas import tpu as pltpu

F32 = jnp.float32
BF16 = jnp.bfloat16

GRID_W = 64
HEAD_DIM_A = 128
HEADS_A = 4
KV_HEADS_A = 2
HEAD_DIM_B = 64
HEADS_B = 8
NA_ROWS = 8
NA_COLS = 16
FOURIER_GROUPS = 4
ROPE_THETA = 10000.0
EPS = 1e-6
QA_COLS = HEADS_A * HEAD_DIM_A
KA_COLS = KV_HEADS_A * HEAD_DIM_A
B_COLS = HEADS_B * HEAD_DIM_B
NEG = -1e30
LOG2E = 1.4426950408889634
LANES = 128

VMEM_LIMIT = 56 * 1024 * 1024


def _cparams(sem):
    return pltpu.CompilerParams(dimension_semantics=sem, vmem_limit_bytes=VMEM_LIMIT)


def _rms(x, g):
    ms = jnp.mean(x * x, axis=-1, keepdims=True)
    return x * lax.rsqrt(ms + EPS) * g


def _norm_mod(x, g, shift, scale):
    return _rms(x, g) * (1.0 + scale) + shift


def _dot(a, b):
    return jnp.dot(a, b, preferred_element_type=F32)


def _dot_t(a, b):
    return lax.dot_general(a, b, (((1,), (1,)), ((), ())), preferred_element_type=F32)


def _side_cast_plan(weights, n_steps, step_of):
    args = [w.reshape(-1, w.shape[-1]) for w in weights]
    specs = [pl.BlockSpec((a.shape[0] // n_steps, a.shape[1]), lambda *g: (step_of(*g), 0))
             for a in args]
    shapes = [jax.ShapeDtypeStruct(a.shape, BF16) for a in args]
    return args, specs, shapes


def _side_cast(in_refs, out_refs):
    for src, dst in zip(in_refs, out_refs):
        dst[...] = src[...].astype(dst.dtype)


def _w_in_kernel(w_ref, p_ref, o_ref):
    j = pl.program_id(0)
    q_blocks = (QA_COLS + B_COLS) // HEAD_DIM_A
    rotary = (j < HEADS_A) | ((j >= q_blocks) & (j < q_blocks + KV_HEADS_A))
    w = w_ref[...].astype(BF16)

    @pl.when(rotary)
    def _():
        o_ref[...] = _dot(w, p_ref[...]).astype(BF16)

    @pl.when(jnp.logical_not(rotary))
    def _():
        o_ref[...] = w


def _prep_w_in(attn_w_in, perm):
    _, d, ncol = attn_w_in.shape
    p = np.zeros((HEAD_DIM_A, HEAD_DIM_A), np.float32)
    p[perm, np.arange(HEAD_DIM_A)] = 1.0
    return pl.pallas_call(
        _w_in_kernel,
        grid=(ncol // HEAD_DIM_A,),
        in_specs=[
            pl.BlockSpec((None, d, HEAD_DIM_A), lambda j: (0, 0, j)),
            pl.BlockSpec((HEAD_DIM_A, HEAD_DIM_A), lambda j: (0, 0)),
        ],
        out_specs=pl.BlockSpec((d, HEAD_DIM_A), lambda j: (0, j)),
        out_shape=jax.ShapeDtypeStruct((d, ncol), BF16),
        compiler_params=_cparams(("parallel",)),
        name="prep_w_in",
    )(attn_w_in, jnp.asarray(p).astype(BF16))


def _adaln_kernel(v_ref, w_ref, b_ref, o_ref):
    v = v_ref[...]
    s = v * jax.nn.sigmoid(v)
    w = w_ref[0]
    s_hi = s.astype(BF16)
    s_lo = (s - s_hi.astype(F32)).astype(BF16)
    w_hi = w.astype(BF16)
    w_lo = (w - w_hi.astype(F32)).astype(BF16)
    rows = s.shape[0]
    t = _dot(jnp.concatenate([s_hi, s_lo], axis=0), w_hi)
    o_ref[0] = t[:rows] + t[rows:] + _dot(s_hi, w_lo) + b_ref[0]


def _adaln(vec8, mod_w, mod_b):
    depth, d, n = mod_w.shape
    tn = 1536
    return pl.pallas_call(
        _adaln_kernel,
        grid=(depth, n // tn),
        in_specs=[
            pl.BlockSpec((8, d), lambda l, j: (0, 0)),
            pl.BlockSpec((1, d, tn), lambda l, j: (l, 0, j)),
            pl.BlockSpec((1, 1, tn), lambda l, j: (l, 0, j)),
        ],
        out_specs=pl.BlockSpec((1, 8, tn), lambda l, j: (l, 0, j)),
        out_shape=jax.ShapeDtypeStruct((depth, 8, n), F32),
        compiler_params=_cparams(("arbitrary", "arbitrary")),
        name="adaln",
    )(vec8, mod_w, mod_b.reshape(depth, 1, n))


PROJ_TM = 512


def _proj_kernel(x_ref, sh_ref, sc_ref, g_ref, w_ref, cos_ref, sin_ref, qg_ref, kg_ref,
                 *out_refs, with_q):
    h = _norm_mod(x_ref[0], g_ref[...], sh_ref[0], sc_ref[0]).astype(BF16)
    cosf = cos_ref[...]
    sinf = sin_ref[...]

    def rope_heads(p, n_heads, gain, scale, o_ref):
        for hd in range(n_heads):
            t = _rms(p[:, hd * HEAD_DIM_A:(hd + 1) * HEAD_DIM_A], gain)
            t = t * cosf + pltpu.roll(t, HEAD_DIM_A // 2, axis=1) * sinf
            o_ref[0, :, hd * HEAD_DIM_A:(hd + 1) * HEAD_DIM_A] = (t * scale).astype(BF16)

    col = QA_COLS + B_COLS
    if with_q:
        qa_ref, qb_ref, ka_ref, va_ref, kb_ref, vb_ref = out_refs
        rope_heads(_dot(h, w_ref[:, 0:QA_COLS]), HEADS_A, qg_ref[...], HEAD_DIM_A ** -0.5 * LOG2E,
                   qa_ref)
        qb = _dot(h, w_ref[:, QA_COLS:QA_COLS + B_COLS])
        qb_ref[0] = (qb * (HEAD_DIM_B ** -0.5 * LOG2E)).astype(BF16)
    else:
        ka_ref, va_ref, kb_ref, vb_ref = out_refs
    rope_heads(_dot(h, w_ref[:, col:col + KA_COLS]), KV_HEADS_A, kg_ref[...], 1.0, ka_ref)
    col += KA_COLS
    va_ref[0] = _dot(h, w_ref[:, col:col + KA_COLS]).astype(BF16)
    col += KA_COLS
    kb_ref[0] = _dot(h, w_ref[:, col:col + B_COLS]).astype(BF16)
    col += B_COLS
    vb_ref[0] = _dot(h, w_ref[:, col:col + B_COLS]).astype(BF16)


def _proj(x, shift, scale, gain, w, rope, qg, kg, *, with_q, tm, name):
    b, n, d = x.shape
    ncol = w.shape[1]
    widths = ([QA_COLS, B_COLS] if with_q else []) + [KA_COLS, KA_COLS, B_COLS, B_COLS]
    vec = lambda bb, i: (bb, 0, 0)
    return pl.pallas_call(
        functools.partial(_proj_kernel, with_q=with_q),
        grid=(b, n // tm),
        in_specs=[
            pl.BlockSpec((1, tm, d), lambda bb, i: (bb, i, 0)),
            pl.BlockSpec((1, 1, d), vec),
            pl.BlockSpec((1, 1, d), vec),
            pl.BlockSpec((1, d), lambda bb, i: (0, 0)),
            pl.BlockSpec((d, ncol), lambda bb, i: (0, 0)),
            pl.BlockSpec((tm, HEAD_DIM_A), lambda bb, i: (i, 0)),
            pl.BlockSpec((tm, HEAD_DIM_A), lambda bb, i: (i, 0)),
            pl.BlockSpec((1, HEAD_DIM_A), lambda bb, i: (0, 0)),
            pl.BlockSpec((1, HEAD_DIM_A), lambda bb, i: (0, 0)),
        ],
        out_specs=[pl.BlockSpec((1, tm, wd), lambda bb, i: (bb, i, 0)) for wd in widths],
        out_shape=[jax.ShapeDtypeStruct((b, n, wd), BF16) for wd in widths],
        compiler_params=_cparams(("parallel", "parallel")),
        name=name,
    )(x, shift, scale, gain, w, *rope, qg, kg)


GQA_UNROLL = 8


def _flash_update(q, k, v, m_ref, l_ref, acc_ref):
    d = v.shape[1]
    s = _dot_t(q, k)
    m_old = m_ref[...]
    m_new = jnp.maximum(m_old, jnp.max(s, axis=-1, keepdims=True))
    alpha = jnp.exp2(m_old - m_new)
    p = jnp.exp2(s - jnp.tile(m_new, (1, s.shape[1] // m_new.shape[1])))
    o = _dot(p.astype(BF16), jnp.concatenate([v, jnp.ones_like(v)], axis=1))
    acc_ref[...] = alpha * acc_ref[...] + o[:, :d]
    l_ref[...] = alpha * l_ref[...] + o[:, d:]
    m_ref[...] = m_new


def _gqa_kernel(q_ref, k_ref, v_ref, ck_ref, cv_ref, *rest, tk, n_side):
    side_in, (o_ref, *side_out), (m_ref, l_ref, acc_ref) = (
        rest[:n_side], rest[n_side:2 * n_side + 1], rest[2 * n_side + 1:])
    _side_cast(side_in, side_out)
    group = q_ref.shape[2] // HEAD_DIM_A
    m_ref[...] = jnp.full(m_ref.shape, NEG, F32)
    l_ref[...] = jnp.zeros(l_ref.shape, F32)
    acc_ref[...] = jnp.zeros(acc_ref.shape, F32)

    def update(k, v):
        for g in range(group):
            _flash_update(q_ref[0, :, g * HEAD_DIM_A:(g + 1) * HEAD_DIM_A], k, v,
                          m_ref.at[g], l_ref.at[g], acc_ref.at[g])

    def body(i, carry):
        off = pl.multiple_of(i * tk, tk)
        update(k_ref[0, pl.ds(off, tk), :], v_ref[0, pl.ds(off, tk), :])
        return carry

    lax.fori_loop(0, k_ref.shape[1] // tk, body, 0, unroll=GQA_UNROLL)
    update(ck_ref[0], cv_ref[0])
    o_ref[0] = jnp.concatenate([acc_ref[g] / l_ref[g] for g in range(group)], axis=1).astype(BF16)


def _gqa(qa, ka, va, cka, cva, side, *, tq, tk):
    b, n, _ = qa.shape
    c = cka.shape[1]
    group = HEADS_A // KV_HEADS_A
    gw = group * HEAD_DIM_A
    nq = n // tq
    side_args, side_specs, side_shapes = _side_cast_plan(
        side, b * KV_HEADS_A * nq, lambda bb, h, i: (bb * KV_HEADS_A + h) * nq + i)
    out = pl.pallas_call(
        functools.partial(_gqa_kernel, tk=tk, n_side=len(side)),
        grid=(b, KV_HEADS_A, nq),
        in_specs=[
            pl.BlockSpec((1, tq, gw), lambda bb, h, i: (bb, i, h)),
            pl.BlockSpec((1, n, HEAD_DIM_A), lambda bb, h, i: (bb, 0, h)),
            pl.BlockSpec((1, n, HEAD_DIM_A), lambda bb, h, i: (bb, 0, h)),
            pl.BlockSpec((1, c, HEAD_DIM_A), lambda bb, h, i: (bb, 0, h)),
            pl.BlockSpec((1, c, HEAD_DIM_A), lambda bb, h, i: (bb, 0, h)),
        ] + side_specs,
        out_specs=[pl.BlockSpec((1, tq, gw), lambda bb, h, i: (bb, i, h))] + side_specs,
        out_shape=[jax.ShapeDtypeStruct((b, n, QA_COLS), BF16)] + side_shapes,
        scratch_shapes=[
            pltpu.VMEM((group, tq, HEAD_DIM_A), F32),
            pltpu.VMEM((group, tq, HEAD_DIM_A), F32),
            pltpu.VMEM((group, tq, HEAD_DIM_A), F32),
        ],
        compiler_params=_cparams(("parallel", "parallel", "arbitrary")),
        name="gqa",
    )(qa, ka, va, cka, cva, *side_args)
    return out[0], [o.reshape(w.shape) for o, w in zip(out[1:], side)]


NA_UNROLL = 32


def _na_kernel(q_ref, k_ref, v_ref, ck_ref, cv_ref, bias_ref, *rest, n_side):
    o_ref = rest[n_side]
    _side_cast(rest[:n_side], rest[n_side + 1:])
    rows = q_ref.shape[1] // GRID_W
    win = NA_ROWS * GRID_W
    lane = lax.broadcasted_iota(jnp.int32, (GRID_W, 2 * HEAD_DIM_B), 1)
    lo = lane < HEAD_DIM_B
    pw = 2 * HEAD_DIM_B

    def with_ones(v):
        return jnp.concatenate([v, jnp.ones_like(v)], axis=1)

    def body(r, carry):
        r0 = jnp.clip(r - NA_ROWS // 2, 0, rows - NA_ROWS)
        q = q_ref[0, pl.ds(pl.multiple_of(r * GRID_W, GRID_W), GRID_W), :]
        zero = jnp.zeros_like(q)
        qq = jnp.concatenate([jnp.where(lo, q, zero), jnp.where(lo, zero, q)], axis=0)
        koff = pl.multiple_of(r0 * GRID_W, GRID_W)
        s = _dot_t(qq, k_ref[0, pl.ds(koff, win), :]) + bias_ref[r - r0]
        sc = _dot_t(qq, ck_ref[0])
        m = jnp.maximum(jnp.max(s, axis=-1, keepdims=True), jnp.max(sc, axis=-1, keepdims=True))
        p = jnp.exp2(s - m).astype(BF16)
        pc = jnp.exp2(sc - m).astype(BF16)
        o = _dot(p, with_ones(v_ref[0, pl.ds(koff, win), :])) + _dot(pc, with_ones(cv_ref[0]))
        o = o[:, :pw] / o[:, pw:]
        res = jnp.where(lo, o[:GRID_W], o[GRID_W:])
        o_ref[0, pl.ds(pl.multiple_of(r * GRID_W, GRID_W), GRID_W), :] = res.astype(BF16)
        return carry

    lax.fori_loop(0, rows, body, 0, unroll=NA_UNROLL)


def _na(qb, kb, vb, ckb, cvb, bias, side):
    b, n, _ = qb.shape
    c = ckb.shape[1]
    pw = 2 * HEAD_DIM_B
    n_cls = bias.shape[0]
    tok = lambda bb, h: (bb, 0, h)
    pairs = HEADS_B // 2
    side_args, side_specs, side_shapes = _side_cast_plan(
        side, b * pairs, lambda bb, h: bb * pairs + h)
    out = pl.pallas_call(
        functools.partial(_na_kernel, n_side=len(side)),
        grid=(b, pairs),
        in_specs=[
            pl.BlockSpec((1, n, pw), tok),
            pl.BlockSpec((1, n, pw), tok),
            pl.BlockSpec((1, n, pw), tok),
            pl.BlockSpec((1, c, pw), tok),
            pl.BlockSpec((1, c, pw), tok),
            pl.BlockSpec((n_cls, 2 * GRID_W, NA_ROWS * GRID_W), lambda bb, h: (0, h, 0)),
        ] + side_specs,
        out_specs=[pl.BlockSpec((1, n, pw), tok)] + side_specs,
        out_shape=[jax.ShapeDtypeStruct((b, n, B_COLS), BF16)] + side_shapes,
        compiler_params=_cparams(("parallel", "parallel")),
        name="natten",
    )(qb, kb, vb, ckb, cvb, bias, *side_args)
    return out[0], [o.reshape(w.shape) for o, w in zip(out[1:], side)]


def _na_bias_table(rpb):
    edge = GRID_W - NA_COLS
    n_off = rpb.shape[1]
    padded = jnp.pad(rpb, ((0, 0), (0, 1), (edge, 2 * GRID_W - rpb.shape[2] - edge)))
    return pl.pallas_call(
        functools.partial(_na_bias_kernel, edge=edge),
        grid=(NA_ROWS,),
        in_specs=[pl.BlockSpec((HEADS_B, n_off + 1, 2 * GRID_W), lambda s: (0, 0, 0))],
        out_specs=pl.BlockSpec((1, HEADS_B * GRID_W, NA_ROWS * GRID_W), lambda s: (s, 0, 0)),
        out_shape=jax.ShapeDtypeStruct((NA_ROWS, HEADS_B * GRID_W, NA_ROWS * GRID_W), F32),
        compiler_params=_cparams(("parallel",)),
        name="na_bias",
    )(padded)


def _na_bias_kernel(rp_ref, o_ref, *, edge):
    cls = pl.program_id(0)
    shape = (GRID_W, 2 * GRID_W)
    c = lax.broadcasted_iota(jnp.int32, shape, 0)
    lane = lax.broadcasted_iota(jnp.int32, shape, 1)
    lo = lane < GRID_W
    j = jnp.where(lo, lane, lane - GRID_W)
    cs = jnp.clip(c - NA_COLS // 2, 0, edge)
    valid = (j >= cs) & (j < cs + NA_COLS)

    def toeplitz(h, i, shift):
        row = rp_ref[h, pl.ds(i - cls + NA_ROWS - 1, 1), :]
        return pltpu.roll(jnp.broadcast_to(row, shape), shift, 1, stride=1, stride_axis=0)

    for h in range(HEADS_B):
        for pair in range(NA_ROWS // 2):
            t = jnp.where(lo, toeplitz(h, 2 * pair, GRID_W + 1), toeplitz(h, 2 * pair + 1, 1))
            o_ref[0, h * GRID_W:(h + 1) * GRID_W, pair * 2 * GRID_W:(pair + 1) * 2 * GRID_W] = (
                jnp.where(valid, t * LOG2E, NEG))


HALO = 8
MIX_HALO = 16
FFN_CHUNK = 256
FFN_TM = 512
FFN_SLOTS = 4
FFN_ROWS = 32


def _ffn_kernel(x_ref, xp_ref, xn_ref, sh_ref, sc_ref, g2_ref, ng_ref, wup_ref, cw_ref, cb_ref,
                wdn_ref, fg_ref, *rest, tm, d_ff, final_norm, mixer):
    if mixer:
        (oa_ref, oap_ref, oan_ref, ob_ref, obp_ref, obn_ref, wo_ref, g1_ref,
         o_ref, u_ref, a_ref, y_ref) = rest
        ka = oa_ref.shape[2]
        rows = lambda main, prev, nxt: jnp.concatenate([main[0], prev[0], nxt[0]], axis=0)
        mix = (_dot(rows(oa_ref, oap_ref, oan_ref), wo_ref[0:ka, :])
               + _dot(rows(ob_ref, obp_ref, obn_ref), wo_ref[ka:, :]))
        g1 = g1_ref[0]
        lo = tm + MIX_HALO - HALO
        x_tile = x_ref[0] + g1 * mix[:tm]
        x_prev = xp_ref[0] + g1 * mix[lo:lo + HALO]
        x_next = xn_ref[0] + g1 * mix[tm + MIX_HALO:tm + MIX_HALO + HALO]
        o_ref[0] = x_tile
    else:
        o_ref, u_ref, a_ref, y_ref = rest
        x_tile, x_prev, x_next = x_ref[0], xp_ref[0], xn_ref[0]
    i = pl.program_id(1)
    ng, sh, sc = ng_ref[...], sh_ref[0], sc_ref[0]
    d = x_ref.shape[2]
    half = tm // 2
    slabs = FFN_CHUNK // LANES

    hp = jnp.where(i > 0, _norm_mod(x_prev, ng, sh, sc), 0.0)
    hn = jnp.where(i < pl.num_programs(1) - 1, _norm_mod(x_next, ng, sh, sc), 0.0)
    h = jnp.concatenate([hp, _norm_mod(x_tile, ng, sh, sc), hn], axis=0).astype(BF16)

    def project(buf, part, col):
        u = _dot(h, wup_ref[:, col:col + FFN_CHUNK])
        for s in range(slabs):
            u_ref[buf, part, s] = u[:, s * LANES:(s + 1) * LANES]

    def conv_gate(buf, c0):
        for s in range(slabs):
            cols = (c0 + s * LANES, d_ff + c0 + s * LANES)
            taps_w = [[cw_ref[k:k + 1, c:c + LANES] for k in range(3)] for c in cols]
            bias = [cb_ref[:, c:c + LANES] for c in cols]
            for rb in range(half // FFN_ROWS):
                res = []
                for part in range(2):
                    t = [u_ref[buf, part, s, pl.ds(HALO - 1 + k + 2 * rb * FFN_ROWS, FFN_ROWS, stride=2), :]
                         for k in range(4)]
                    w0, w1, w2 = taps_w[part]
                    res.append((t[0] * w0 + t[1] * w1 + t[2] * w2 + bias[part],
                                t[1] * w0 + t[2] * w1 + t[3] * w2 + bias[part]))
                (g_even, g_odd), (v_even, v_odd) = res
                r0 = rb * FFN_ROWS
                col = slice(c0 + s * LANES, c0 + (s + 1) * LANES)
                a_ref[r0:r0 + FFN_ROWS, col] = (g_even * jax.nn.sigmoid(g_even) * v_even).astype(BF16)
                a_ref[half + r0:half + r0 + FFN_ROWS, col] = (
                    g_odd * jax.nn.sigmoid(g_odd) * v_odd).astype(BF16)

    n_chunks = d_ff // FFN_CHUNK
    n_buf = u_ref.shape[0]

    def project_chunk(f):
        project(f % n_buf, 0, f * FFN_CHUNK)
        project(f % n_buf, 1, d_ff + f * FFN_CHUNK)

    for f in range(min(n_buf - 1, n_chunks)):
        project_chunk(f)
    for f in range(n_chunks):
        c0 = f * FFN_CHUNK
        if f + n_buf - 1 < n_chunks:
            project_chunk(f + n_buf - 1)
        conv_gate(f % n_buf, c0)
    acc = _dot(a_ref[...], wdn_ref[...])
    for s in range(d // LANES):
        y_ref[s, pl.ds(0, half, stride=2), :] = acc[:half, s * LANES:(s + 1) * LANES]
        y_ref[s, pl.ds(1, half, stride=2), :] = acc[half:, s * LANES:(s + 1) * LANES]
    ffn = jnp.concatenate([y_ref[s] for s in range(d // LANES)], axis=1)
    y = (o_ref[0] if mixer else x_ref[0]) + g2_ref[0] * ffn
    if final_norm:
        y = _rms(y, fg_ref[...])
    o_ref[0] = y


def _halo_specs(width, tm, n, rows):
    hb = tm // rows
    last = n // rows - 1
    return [
        pl.BlockSpec((1, tm, width), lambda bb, i: (bb, i, 0)),
        pl.BlockSpec((1, rows, width), lambda bb, i: (bb, jnp.maximum(i * hb - 1, 0), 0)),
        pl.BlockSpec((1, rows, width), lambda bb, i: (bb, jnp.minimum((i + 1) * hb, last), 0)),
    ]


def _ffn(x, shift, scale, gate, norm_g, w_up, conv_w, conv_b, w_down, final_g, *, layer, tm,
         final_norm, mixer=None):
    b, n, d = x.shape
    d_ff = w_down.shape[1]
    vec = lambda bb, i: (bb, 0, 0)
    const2 = lambda bb, i: (0, 0)
    this_layer = lambda bb, i: (layer, 0, 0)
    resident = dict(pipeline_mode=pl.Buffered(1))
    mix_specs, mix_args = [], []
    if mixer is not None:
        oa, ob, w_out, g1 = mixer
        for o in (oa, ob):
            mix_specs += _halo_specs(o.shape[2], tm, n, MIX_HALO)
            mix_args += [o, o, o]
        mix_specs += [pl.BlockSpec(w_out.shape, const2, **resident), pl.BlockSpec((1, 1, d), vec)]
        mix_args += [w_out, g1]
    return pl.pallas_call(
        functools.partial(_ffn_kernel, tm=tm, d_ff=d_ff, final_norm=final_norm,
                          mixer=mixer is not None),
        grid=(b, n // tm),
        in_specs=_halo_specs(d, tm, n, HALO) + [
            pl.BlockSpec((1, 1, d), vec),
            pl.BlockSpec((1, 1, d), vec),
            pl.BlockSpec((1, 1, d), vec),
            pl.BlockSpec((1, d), const2),
            pl.BlockSpec((None,) + w_up.shape[1:], this_layer, **resident),
            pl.BlockSpec(conv_w.shape, const2),
            pl.BlockSpec((1, 2 * d_ff), const2),
            pl.BlockSpec((None,) + w_down.shape[1:], this_layer, **resident),
            pl.BlockSpec((1, d), const2),
        ] + mix_specs,
        out_specs=pl.BlockSpec((1, tm, d), lambda bb, i: (bb, i, 0)),
        out_shape=jax.ShapeDtypeStruct(x.shape, F32),
        scratch_shapes=[
            pltpu.VMEM((FFN_SLOTS, 2, FFN_CHUNK // LANES, tm + 2 * HALO, LANES), F32),
            pltpu.VMEM((tm, d_ff), BF16),
            pltpu.VMEM((d // LANES, tm, LANES), F32),
        ],
        compiler_params=_cparams(("parallel", "parallel")),
        name="conv_ffn",
    )(x, x, x, shift, scale, gate, norm_g, w_up, conv_w, conv_b.reshape(1, -1), w_down, final_g,
      *mix_args)


F1_NC = 8
F2_NK = 16


def _dft_tables(n_tok, cg):
    w = GRID_W
    idx = np.arange(w, dtype=np.float64)
    a64 = 2.0 * np.pi * np.outer(idx, idx) / w
    fc, fs = np.cos(a64), np.sin(a64)
    a1 = 2.0 * np.pi * idx[None, :, None] * (w * idx[None, None, :] + idx[:, None, None]) / n_tok
    step1 = np.block([[np.cos(a1), np.sin(a1)], [-np.sin(a1), np.cos(a1)]])
    eye = np.eye(F2_NK)
    step3 = (np.kron(fc[:w // 2 + 1], eye), np.kron(fs[:w // 2 + 1], eye))
    ch = np.arange(cg, dtype=np.float64)
    ach = 2.0 * np.pi * np.outer(ch, ch) / cg
    chan = np.concatenate([np.cos(ach), -np.sin(ach)], axis=1)
    f = lambda t: jnp.asarray(t, F32)
    return f(chan), f(step1), f(step3[0]), f(step3[1])


def _f1_kernel(x_ref, sh_ref, sc_ref, g_ref, chan_ref, s1_ref, ar_ref, ai_ref, u_ref, *, nc, d):
    w = GRID_W
    cg = d // FOURIER_GROUPS
    tiles = d // LANES
    gt = cg // LANES
    x = x_ref[0].reshape(w * nc, d)
    h = _norm_mod(x, g_ref[...], sh_ref[0], sc_ref[0]).astype(BF16)
    for g in range(FOURIER_GROUPS):
        u = _dot(h[:, g * cg:(g + 1) * cg], chan_ref[...])
        for t in range(2 * gt):
            part, tt = divmod(t, gt)
            u_ref[part * tiles + g * gt + tt] = u[:, t * LANES:(t + 1) * LANES]
    for c in range(nc):
        rows = pl.ds(c, w, stride=nc)
        ur = jnp.concatenate([u_ref[s, rows, :] for s in range(tiles)], axis=1)
        ui = jnp.concatenate([u_ref[tiles + s, rows, :] for s in range(tiles)], axis=1)
        a = _dot(s1_ref[c], jnp.concatenate([ur, ui], axis=0).astype(BF16))
        ar_ref[0, c] = a[:w].astype(BF16)
        ai_ref[0, c] = a[w:].astype(BF16)


def _f2_kernel(ar_ref, ai_ref, x_ref, g1_ref, kc_ref, ks_ref, w_ref, o_ref, *, nk, d, inv_norm):
    w = GRID_W
    a_r = ar_ref[0].reshape(w * nk, d)
    a_i = ai_ref[0].reshape(w * nk, d)
    p = _dot(kc_ref[...], a_r)
    q = _dot(ks_ref[...], a_i)
    diff = p - q
    mirrored = [diff[(w - k2) * nk:(w - k2 + 1) * nk] for k2 in range(w // 2 + 1, w)]
    z = jnp.concatenate([p + q] + mirrored, axis=0)
    y = _dot((z * inv_norm).astype(BF16), w_ref[...])
    o_ref[0] = x_ref[0] + g1_ref[0] * y.reshape(w, nk, d)


def _fourier(x, shift, scale, gate, norm_g, w_out):
    b, n, d = x.shape
    w = GRID_W
    assert n == w * w
    nc, nk = F1_NC, F2_NK
    cg = d // FOURIER_GROUPS
    chan, step1, kc, ks = _dft_tables(n, cg)
    chan, step1, kc, ks = (t.astype(BF16) for t in (chan, step1, kc, ks))
    x4 = x.reshape(b, w, w, d)
    vec = lambda bb, j: (bb, 0, 0)
    const2 = lambda bb, j: (0, 0)
    resident = dict(pipeline_mode=pl.Buffered(1))
    a_r, a_i = pl.pallas_call(
        functools.partial(_f1_kernel, nc=nc, d=d),
        grid=(b, w // nc),
        in_specs=[
            pl.BlockSpec((1, w, nc, d), lambda bb, j: (bb, 0, j, 0)),
            pl.BlockSpec((1, 1, d), vec),
            pl.BlockSpec((1, 1, d), vec),
            pl.BlockSpec((1, d), const2),
            pl.BlockSpec(chan.shape, const2),
            pl.BlockSpec((nc,) + step1.shape[1:], lambda bb, j: (j, 0, 0)),
        ],
        out_specs=[pl.BlockSpec((1, nc, w, d), lambda bb, j: (bb, j, 0, 0))] * 2,
        out_shape=[jax.ShapeDtypeStruct((b, w, w, d), BF16)] * 2,
        scratch_shapes=[pltpu.VMEM((2 * d // LANES, w * nc, LANES), F32)],
        compiler_params=_cparams(("parallel", "parallel")),
        name="fourier_rows",
    )(x4, shift, scale, norm_g, chan, step1)
    out = pl.pallas_call(
        functools.partial(_f2_kernel, nk=nk, d=d, inv_norm=1.0 / float(np.sqrt(n * cg))),
        grid=(b, w // nk),
        in_specs=[
            pl.BlockSpec((1, w, nk, d), lambda bb, j: (bb, 0, j, 0)),
            pl.BlockSpec((1, w, nk, d), lambda bb, j: (bb, 0, j, 0)),
            pl.BlockSpec((1, w, nk, d), lambda bb, j: (bb, 0, j, 0)),
            pl.BlockSpec((1, 1, d), vec),
            pl.BlockSpec(kc.shape, const2, **resident),
            pl.BlockSpec(ks.shape, const2, **resident),
            pl.BlockSpec(w_out.shape, const2, **resident),
        ],
        out_specs=pl.BlockSpec((1, w, nk, d), lambda bb, j: (bb, 0, j, 0)),
        out_shape=jax.ShapeDtypeStruct((b, w, w, d), F32),
        compiler_params=_cparams(("parallel", "parallel")),
        name="fourier_cols",
    )(a_r, a_i, x4, gate, kc, ks, w_out)
    return out.reshape(b, n, d)


def _rope_tables(n):
    t = jnp.arange(n)
    row = (t // GRID_W).astype(F32)
    col = (t % GRID_W).astype(F32)
    pairs = HEAD_DIM_A // 4
    freqs = ROPE_THETA ** (-jnp.arange(pairs, dtype=F32) / pairs)
    ang = jnp.concatenate([row[:, None] * freqs, col[:, None] * freqs], axis=-1)
    cos, sin = jnp.cos(ang), jnp.sin(ang)
    return jnp.concatenate([cos, cos], axis=-1), jnp.concatenate([-sin, sin], axis=-1)


def kernel(x, c, ctx, c_ctx, mod_w, mod_b, norm1_g, norm2_g, attn_w_in, attn_w_out, q_norm_g,
           k_norm_g, na_rpb, fourier_w_out, ffn_w_up, ffn_conv_w, ffn_conv_b, ffn_w_down, final_g):
    b, n, d = x.shape
    depth = mod_w.shape[0]
    assert depth == 2 and b + 1 <= 8

    vec8 = jnp.zeros((8, d), F32).at[:b].set(c).at[b].set(c_ctx)
    mod = _adaln(vec8, mod_w, mod_b)
    term = lambda l, k: mod[l, :b, k * d:(k + 1) * d].reshape(b, 1, d)
    cterm = lambda l, k: jnp.broadcast_to(mod[l, b, k * d:(k + 1) * d].reshape(1, 1, d), (b, 1, d))

    perm = np.concatenate([np.arange(0, HEAD_DIM_A, 2), np.arange(1, HEAD_DIM_A, 2)])
    w_in = _prep_w_in(attn_w_in, perm)
    qg = q_norm_g[0][perm].reshape(1, -1)
    kg = k_norm_g[0][perm].reshape(1, -1)
    n_ctx = ctx.shape[1]
    no_rope = (jnp.ones((n_ctx, HEAD_DIM_A), F32), jnp.zeros((n_ctx, HEAD_DIM_A), F32))
    g1row = norm1_g[0].reshape(1, d)

    qa, qb, ka, va, kb, vb = _proj(x, term(0, 0), term(0, 1), g1row, w_in, _rope_tables(n), qg, kg,
                                   with_q=True, tm=PROJ_TM, name="proj_latent")
    cka, cva, ckb, cvb = _proj(ctx, cterm(0, 0), cterm(0, 1), g1row, w_in, no_rope, qg, kg,
                               with_q=False, tm=n_ctx, name="proj_context")
    oa, (w_up, w_down) = _gqa(qa, ka, va, cka, cva, (ffn_w_up, ffn_w_down), tq=1024, tk=512)
    ob, (w_attn_out, w_fourier_out) = _na(qb, kb, vb, ckb, cvb, _na_bias_table(na_rpb[0]),
                                          (attn_w_out, fourier_w_out))
    x = _ffn(x, term(0, 3), term(0, 4), term(0, 5), norm2_g[0].reshape(1, d), w_up,
             ffn_conv_w[0], ffn_conv_b[0], w_down, final_g.reshape(1, d),
             layer=0, tm=FFN_TM, final_norm=False,
             mixer=(oa, ob, w_attn_out[0], term(0, 2)))

    x = _fourier(x, term(1, 0), term(1, 1), term(1, 2), norm1_g[1].reshape(1, d),
                 w_fourier_out[0])
    x = _ffn(x, term(1, 3), term(1, 4), term(1, 5), norm2_g[1].reshape(1, d), w_up,
             ffn_conv_w[1], ffn_conv_b[1], w_down, final_g.reshape(1, d),
             layer=1, tm=FFN_TM, final_norm=True)
    return x
```

```python
import functools

import numpy as np
import jax
import jax.numpy as jnp
from jax import lax
from jax.experimental import pallas as pl
from jax.experimental.pallas import tpu as pltpu

F32 = jnp.float32
BF16 = jnp.bfloat16

GRID_W = 64
HEAD_DIM_A = 128
HEADS_A = 4
KV_HEADS_A = 2
HEAD_DIM_B = 64
HEADS_B = 8
NA_ROWS = 8
NA_COLS = 16
FOURIER_GROUPS = 4
ROPE_THETA = 10000.0
EPS = 1e-6
QA_COLS = HEADS_A * HEAD_DIM_A
KA_COLS = KV_HEADS_A * HEAD_DIM_A
B_COLS = HEADS_B * HEAD_DIM_B
NEG = -1e30
LOG2E = 1.4426950408889634
LANES = 128

VMEM_LIMIT = 56 * 1024 * 1024


def _cparams(sem):
    return pltpu.CompilerParams(dimension_semantics=sem, vmem_limit_bytes=VMEM_LIMIT)


def _rms(x, g):
    ms = jnp.mean(x * x, axis=-1, keepdims=True)
    return x * lax.rsqrt(ms + EPS) * g


def _norm_mod(x, g, shift, scale):
    return _rms(x, g) * (1.0 + scale) + shift


def _dot(a, b):
    return jnp.dot(a, b, preferred_element_type=F32)


def _dot_t(a, b):
    return lax.dot_general(a, b, (((1,), (1,)), ((), ())), preferred_element_type=F32)


def _side_cast_plan(weights, n_steps, step_of):
    args = [w.reshape(-1, w.shape[-1]) for w in weights]
    specs = [pl.BlockSpec((a.shape[0] // n_steps, a.shape[1]), lambda *g: (step_of(*g), 0))
             for a in args]
    shapes = [jax.ShapeDtypeStruct(a.shape, BF16) for a in args]
    return args, specs, shapes


def _side_cast(in_refs, out_refs):
    for src, dst in zip(in_refs, out_refs):
        dst[...] = src[...].astype(dst.dtype)


W_IN_HEADS = 4


def _w_in_kernel(w_ref, p_ref, o_ref):
    q_blocks = (QA_COLS + B_COLS) // HEAD_DIM_A
    for k in range(W_IN_HEADS):
        head = pl.program_id(0) * W_IN_HEADS + k
        rotary = (head < HEADS_A) | ((head >= q_blocks) & (head < q_blocks + KV_HEADS_A))
        cols = slice(k * HEAD_DIM_A, (k + 1) * HEAD_DIM_A)
        w = w_ref[:, cols].astype(BF16)

        @pl.when(rotary)
        def _():
            o_ref[:, cols] = _dot(w, p_ref[...]).astype(BF16)

        @pl.when(jnp.logical_not(rotary))
        def _():
            o_ref[:, cols] = w


def _prep_w_in(attn_w_in, perm):
    _, d, ncol = attn_w_in.shape
    p = np.zeros((HEAD_DIM_A, HEAD_DIM_A), np.float32)
    p[perm, np.arange(HEAD_DIM_A)] = 1.0
    return pl.pallas_call(
        _w_in_kernel,
        grid=(ncol // (W_IN_HEADS * HEAD_DIM_A),),
        in_specs=[
            pl.BlockSpec((None, d, W_IN_HEADS * HEAD_DIM_A), lambda j: (0, 0, j)),
            pl.BlockSpec((HEAD_DIM_A, HEAD_DIM_A), lambda j: (0, 0)),
        ],
        out_specs=pl.BlockSpec((d, W_IN_HEADS * HEAD_DIM_A), lambda j: (0, j)),
        out_shape=jax.ShapeDtypeStruct((d, ncol), BF16),
        compiler_params=_cparams(("parallel",)),
        name="prep_w_in",
    )(attn_w_in, jnp.asarray(p).astype(BF16))


def _adaln_kernel(v_ref, w_ref, b_ref, o_ref):
    v = v_ref[...]
    s = v * jax.nn.sigmoid(v)
    w = w_ref[0]
    s_hi = s.astype(BF16)
    s_lo = (s - s_hi.astype(F32)).astype(BF16)
    w_hi = w.astype(BF16)
    w_lo = (w - w_hi.astype(F32)).astype(BF16)
    rows = s.shape[0]
    t = _dot(jnp.concatenate([s_hi, s_lo], axis=0), w_hi)
    o_ref[0] = t[:rows] + t[rows:] + _dot(s_hi, w_lo) + b_ref[0]


def _adaln(vec8, mod_w, mod_b):
    depth, d, n = mod_w.shape
    tn = 1536
    return pl.pallas_call(
        _adaln_kernel,
        grid=(depth, n // tn),
        in_specs=[
            pl.BlockSpec((8, d), lambda l, j: (0, 0)),
            pl.BlockSpec((1, d, tn), lambda l, j: (l, 0, j)),
            pl.BlockSpec((1, 1, tn), lambda l, j: (l, 0, j)),
        ],
        out_specs=pl.BlockSpec((1, 8, tn), lambda l, j: (l, 0, j)),
        out_shape=jax.ShapeDtypeStruct((depth, 8, n), F32),
        compiler_params=_cparams(("arbitrary", "arbitrary")),
        name="adaln",
    )(vec8, mod_w, mod_b.reshape(depth, 1, n))


PROJ_TM = 512


def _proj_kernel(x_ref, sh_ref, sc_ref, g_ref, w_ref, cos_ref, sin_ref, qg_ref, kg_ref,
                 *out_refs, with_q):
    h = _norm_mod(x_ref[0], g_ref[...], sh_ref[0], sc_ref[0]).astype(BF16)
    cosf = cos_ref[...]
    sinf = sin_ref[...]

    def rope_heads(p, n_heads, gain, scale, o_ref):
        for hd in range(n_heads):
            t = _rms(p[:, hd * HEAD_DIM_A:(hd + 1) * HEAD_DIM_A], gain)
            t = t * cosf + pltpu.roll(t, HEAD_DIM_A // 2, axis=1) * sinf
            o_ref[0, :, hd * HEAD_DIM_A:(hd + 1) * HEAD_DIM_A] = (t * scale).astype(BF16)

    col = QA_COLS + B_COLS
    if with_q:
        qa_ref, qb_ref, ka_ref, va_ref, kb_ref, vb_ref = out_refs
        rope_heads(_dot(h, w_ref[:, 0:QA_COLS]), HEADS_A, qg_ref[...], HEAD_DIM_A ** -0.5 * LOG2E,
                   qa_ref)
        qb = _dot(h, w_ref[:, QA_COLS:QA_COLS + B_COLS])
        qb_ref[0] = (qb * (HEAD_DIM_B ** -0.5 * LOG2E)).astype(BF16)
    else:
        ka_ref, va_ref, kb_ref, vb_ref = out_refs
    rope_heads(_dot(h, w_ref[:, col:col + KA_COLS]), KV_HEADS_A, kg_ref[...], 1.0, ka_ref)
    col += KA_COLS
    va_ref[0] = _dot(h, w_ref[:, col:col + KA_COLS]).astype(BF16)
    col += KA_COLS
    kb_ref[0] = _dot(h, w_ref[:, col:col + B_COLS]).astype(BF16)
    col += B_COLS
    vb_ref[0] = _dot(h, w_ref[:, col:col + B_COLS]).astype(BF16)


def _proj(x, shift, scale, gain, w, rope, qg, kg, *, with_q, tm, name):
    b, n, d = x.shape
    ncol = w.shape[1]
    widths = ([QA_COLS, B_COLS] if with_q else []) + [KA_COLS, KA_COLS, B_COLS, B_COLS]
    vec = lambda bb, i: (bb, 0, 0)
    return pl.pallas_call(
        functools.partial(_proj_kernel, with_q=with_q),
        grid=(b, n // tm),
        in_specs=[
            pl.BlockSpec((1, tm, d), lambda bb, i: (bb, i, 0)),
            pl.BlockSpec((1, 1, d), vec),
            pl.BlockSpec((1, 1, d), vec),
            pl.BlockSpec((1, d), lambda bb, i: (0, 0)),
            pl.BlockSpec((d, ncol), lambda bb, i: (0, 0)),
            pl.BlockSpec((tm, HEAD_DIM_A), lambda bb, i: (i, 0)),
            pl.BlockSpec((tm, HEAD_DIM_A), lambda bb, i: (i, 0)),
            pl.BlockSpec((1, HEAD_DIM_A), lambda bb, i: (0, 0)),
            pl.BlockSpec((1, HEAD_DIM_A), lambda bb, i: (0, 0)),
        ],
        out_specs=[pl.BlockSpec((1, tm, wd), lambda bb, i: (bb, i, 0)) for wd in widths],
        out_shape=[jax.ShapeDtypeStruct((b, n, wd), BF16) for wd in widths],
        compiler_params=_cparams(("parallel", "parallel")),
        name=name,
    )(x, shift, scale, gain, w, *rope, qg, kg)


GQA_UNROLL = 8


def _flash_update(q, k, v, m_ref, l_ref, acc_ref):
    d = v.shape[1]
    s = _dot_t(q, k)
    m_old = m_ref[...]
    m_new = jnp.maximum(m_old, jnp.max(s, axis=-1, keepdims=True))
    alpha = jnp.exp2(m_old - m_new)
    p = jnp.exp2(s - jnp.tile(m_new, (1, s.shape[1] // m_new.shape[1])))
    o = _dot(p.astype(BF16), jnp.concatenate([v, jnp.ones_like(v)], axis=1))
    acc_ref[...] = alpha * acc_ref[...] + o[:, :d]
    l_ref[...] = alpha * l_ref[...] + o[:, d:]
    m_ref[...] = m_new


def _gqa_kernel(q_ref, k_ref, v_ref, ck_ref, cv_ref, *rest, tk, n_side):
    side_in, (o_ref, *side_out), (m_ref, l_ref, acc_ref) = (
        rest[:n_side], rest[n_side:2 * n_side + 1], rest[2 * n_side + 1:])
    _side_cast(side_in, side_out)
    group = q_ref.shape[2] // HEAD_DIM_A
    m_ref[...] = jnp.full(m_ref.shape, NEG, F32)
    l_ref[...] = jnp.zeros(l_ref.shape, F32)
    acc_ref[...] = jnp.zeros(acc_ref.shape, F32)

    def update(k, v):
        for g in range(group):
            _flash_update(q_ref[0, :, g * HEAD_DIM_A:(g + 1) * HEAD_DIM_A], k, v,
                          m_ref.at[g], l_ref.at[g], acc_ref.at[g])

    def body(i, carry):
        off = pl.multiple_of(i * tk, tk)
        update(k_ref[0, pl.ds(off, tk), :], v_ref[0, pl.ds(off, tk), :])
        return carry

    lax.fori_loop(0, k_ref.shape[1] // tk, body, 0, unroll=GQA_UNROLL)
    update(ck_ref[0], cv_ref[0])
    o_ref[0] = jnp.concatenate([acc_ref[g] / l_ref[g] for g in range(group)], axis=1).astype(BF16)


def _gqa(qa, ka, va, cka, cva, side, *, tq, tk):
    b, n, _ = qa.shape
    c = cka.shape[1]
    group = HEADS_A // KV_HEADS_A
    gw = group * HEAD_DIM_A
    nq = n // tq
    side_args, side_specs, side_shapes = _side_cast_plan(
        side, b * KV_HEADS_A * nq, lambda bb, h, i: (bb * KV_HEADS_A + h) * nq + i)
    out = pl.pallas_call(
        functools.partial(_gqa_kernel, tk=tk, n_side=len(side)),
        grid=(b, KV_HEADS_A, nq),
        in_specs=[
            pl.BlockSpec((1, tq, gw), lambda bb, h, i: (bb, i, h)),
            pl.BlockSpec((1, n, HEAD_DIM_A), lambda bb, h, i: (bb, 0, h)),
            pl.BlockSpec((1, n, HEAD_DIM_A), lambda bb, h, i: (bb, 0, h)),
            pl.BlockSpec((1, c, HEAD_DIM_A), lambda bb, h, i: (bb, 0, h)),
            pl.BlockSpec((1, c, HEAD_DIM_A), lambda bb, h, i: (bb, 0, h)),
        ] + side_specs,
        out_specs=[pl.BlockSpec((1, tq, gw), lambda bb, h, i: (bb, i, h))] + side_specs,
        out_shape=[jax.ShapeDtypeStruct((b, n, QA_COLS), BF16)] + side_shapes,
        scratch_shapes=[
            pltpu.VMEM((group, tq, HEAD_DIM_A), F32),
            pltpu.VMEM((group, tq, HEAD_DIM_A), F32),
            pltpu.VMEM((group, tq, HEAD_DIM_A), F32),
        ],
        compiler_params=_cparams(("parallel", "parallel", "arbitrary")),
        name="gqa",
    )(qa, ka, va, cka, cva, *side_args)
    return out[0], [o.reshape(w.shape) for o, w in zip(out[1:], side)]


NA_UNROLL = 32


def _na_kernel(q_ref, k_ref, v_ref, ck_ref, cv_ref, bias_ref, o_ref):
    rows = q_ref.shape[1] // GRID_W
    win = NA_ROWS * GRID_W
    lane = lax.broadcasted_iota(jnp.int32, (GRID_W, 2 * HEAD_DIM_B), 1)
    lo = lane < HEAD_DIM_B
    pw = 2 * HEAD_DIM_B

    def with_ones(v):
        return jnp.concatenate([v, jnp.ones_like(v)], axis=1)

    def body(r, carry):
        r0 = jnp.clip(r - NA_ROWS // 2, 0, rows - NA_ROWS)
        q = q_ref[0, pl.ds(pl.multiple_of(r * GRID_W, GRID_W), GRID_W), :]
        zero = jnp.zeros_like(q)
        qq = jnp.concatenate([jnp.where(lo, q, zero), jnp.where(lo, zero, q)], axis=0)
        koff = pl.multiple_of(r0 * GRID_W, GRID_W)
        s = _dot_t(qq, k_ref[0, pl.ds(koff, win), :]) + bias_ref[r - r0]
        sc = _dot_t(qq, ck_ref[0])
        m = jnp.maximum(jnp.max(s, axis=-1, keepdims=True), jnp.max(sc, axis=-1, keepdims=True))
        p = jnp.exp2(s - m).astype(BF16)
        pc = jnp.exp2(sc - m).astype(BF16)
        o = _dot(p, with_ones(v_ref[0, pl.ds(koff, win), :])) + _dot(pc, with_ones(cv_ref[0]))
        o = o[:, :pw] / o[:, pw:]
        res = jnp.where(lo, o[:GRID_W], o[GRID_W:])
        o_ref[0, pl.ds(pl.multiple_of(r * GRID_W, GRID_W), GRID_W), :] = res.astype(BF16)
        return carry

    lax.fori_loop(0, rows, body, 0, unroll=NA_UNROLL)


def _na(qb, kb, vb, ckb, cvb, bias):
    b, n, _ = qb.shape
    c = ckb.shape[1]
    pw = 2 * HEAD_DIM_B
    n_cls = bias.shape[0]
    tok = lambda bb, h: (bb, 0, h)
    return pl.pallas_call(
        _na_kernel,
        grid=(b, HEADS_B // 2),
        in_specs=[
            pl.BlockSpec((1, n, pw), tok),
            pl.BlockSpec((1, n, pw), tok),
            pl.BlockSpec((1, n, pw), tok),
            pl.BlockSpec((1, c, pw), tok),
            pl.BlockSpec((1, c, pw), tok),
            pl.BlockSpec((n_cls, 2 * GRID_W, NA_ROWS * GRID_W), lambda bb, h: (0, h, 0)),
        ],
        out_specs=pl.BlockSpec((1, n, pw), tok),
        out_shape=jax.ShapeDtypeStruct((b, n, B_COLS), BF16),
        compiler_params=_cparams(("parallel", "parallel")),
        name="natten",
    )(qb, kb, vb, ckb, cvb, bias)


def _na_bias_table(rpb):
    edge = GRID_W - NA_COLS
    n_off = rpb.shape[1]
    padded = jnp.pad(rpb, ((0, 0), (0, 1), (edge, 2 * GRID_W - rpb.shape[2] - edge)))
    return pl.pallas_call(
        functools.partial(_na_bias_kernel, edge=edge),
        grid=(NA_ROWS,),
        in_specs=[pl.BlockSpec((HEADS_B, n_off + 1, 2 * GRID_W), lambda s: (0, 0, 0))],
        out_specs=pl.BlockSpec((1, HEADS_B * GRID_W, NA_ROWS * GRID_W), lambda s: (s, 0, 0)),
        out_shape=jax.ShapeDtypeStruct((NA_ROWS, HEADS_B * GRID_W, NA_ROWS * GRID_W), F32),
        compiler_params=_cparams(("parallel",)),
        name="na_bias",
    )(padded)


def _na_bias_kernel(rp_ref, o_ref, *, edge):
    cls = pl.program_id(0)
    shape = (GRID_W, 2 * GRID_W)
    c = lax.broadcasted_iota(jnp.int32, shape, 0)
    lane = lax.broadcasted_iota(jnp.int32, shape, 1)
    lo = lane < GRID_W
    j = jnp.where(lo, lane, lane - GRID_W)
    cs = jnp.clip(c - NA_COLS // 2, 0, edge)
    valid = (j >= cs) & (j < cs + NA_COLS)

    def toeplitz(h, i, shift):
        row = rp_ref[h, pl.ds(i - cls + NA_ROWS - 1, 1), :]
        return pltpu.roll(jnp.broadcast_to(row, shape), shift, 1, stride=1, stride_axis=0)

    for h in range(HEADS_B):
        for pair in range(NA_ROWS // 2):
            t = jnp.where(lo, toeplitz(h, 2 * pair, GRID_W + 1), toeplitz(h, 2 * pair + 1, 1))
            o_ref[0, h * GRID_W:(h + 1) * GRID_W, pair * 2 * GRID_W:(pair + 1) * 2 * GRID_W] = (
                jnp.where(valid, t * LOG2E, NEG))


HALO = 8
MIX_HALO = 16
FFN_CHUNK = 256
FFN_TM = 512
FFN_SLOTS = 4
FFN_ROWS = 32


def _ffn_kernel(x_ref, xp_ref, xn_ref, sh_ref, sc_ref, g2_ref, ng_ref, wup_ref, cw_ref, cb_ref,
                wdn_ref, fg_ref, *rest, tm, d_ff, final_norm, mixer):
    if mixer:
        (oa_ref, oap_ref, oan_ref, ob_ref, obp_ref, obn_ref, wo_ref, g1_ref,
         o_ref, u_ref, a_ref, y_ref) = rest
        ka = oa_ref.shape[2]
        rows = lambda main, prev, nxt: jnp.concatenate([main[0], prev[0], nxt[0]], axis=0)
        mix = (_dot(rows(oa_ref, oap_ref, oan_ref), wo_ref[0:ka, :])
               + _dot(rows(ob_ref, obp_ref, obn_ref), wo_ref[ka:, :]))
        g1 = g1_ref[0]
        lo = tm + MIX_HALO - HALO
        x_tile = x_ref[0] + g1 * mix[:tm]
        x_prev = xp_ref[0] + g1 * mix[lo:lo + HALO]
        x_next = xn_ref[0] + g1 * mix[tm + MIX_HALO:tm + MIX_HALO + HALO]
        o_ref[0] = x_tile
    else:
        o_ref, u_ref, a_ref, y_ref = rest
        x_tile, x_prev, x_next = x_ref[0], xp_ref[0], xn_ref[0]
    i = pl.program_id(1)
    ng, sh, sc = ng_ref[...], sh_ref[0], sc_ref[0]
    d = x_ref.shape[2]
    half = tm // 2
    slabs = FFN_CHUNK // LANES

    hp = jnp.where(i > 0, _norm_mod(x_prev, ng, sh, sc), 0.0)
    hn = jnp.where(i < pl.num_programs(1) - 1, _norm_mod(x_next, ng, sh, sc), 0.0)
    h = jnp.concatenate([hp, _norm_mod(x_tile, ng, sh, sc), hn], axis=0).astype(BF16)

    def project(buf, part, col):
        u = _dot(h, wup_ref[:, col:col + FFN_CHUNK])
        for s in range(slabs):
            u_ref[buf, part, s] = u[:, s * LANES:(s + 1) * LANES]

    def conv_gate(buf, c0):
        for s in range(slabs):
            cols = (c0 + s * LANES, d_ff + c0 + s * LANES)
            taps_w = [[cw_ref[k:k + 1, c:c + LANES] for k in range(3)] for c in cols]
            bias = [cb_ref[:, c:c + LANES] for c in cols]
            for rb in range(half // FFN_ROWS):
                res = []
                for part in range(2):
                    t = [u_ref[buf, part, s, pl.ds(HALO - 1 + k + 2 * rb * FFN_ROWS, FFN_ROWS, stride=2), :]
                         for k in range(4)]
                    w0, w1, w2 = taps_w[part]
                    res.append((t[0] * w0 + t[1] * w1 + t[2] * w2 + bias[part],
                                t[1] * w0 + t[2] * w1 + t[3] * w2 + bias[part]))
                (g_even, g_odd), (v_even, v_odd) = res
                r0 = rb * FFN_ROWS
                col = slice(c0 + s * LANES, c0 + (s + 1) * LANES)
                a_ref[r0:r0 + FFN_ROWS, col] = (g_even * jax.nn.sigmoid(g_even) * v_even).astype(BF16)
                a_ref[half + r0:half + r0 + FFN_ROWS, col] = (
                    g_odd * jax.nn.sigmoid(g_odd) * v_odd).astype(BF16)

    n_chunks = d_ff // FFN_CHUNK
    n_buf = u_ref.shape[0]

    def project_chunk(f):
        project(f % n_buf, 0, f * FFN_CHUNK)
        project(f % n_buf, 1, d_ff + f * FFN_CHUNK)

    for f in range(min(n_buf - 1, n_chunks)):
        project_chunk(f)
    for f in range(n_chunks):
        c0 = f * FFN_CHUNK
        if f + n_buf - 1 < n_chunks:
            project_chunk(f + n_buf - 1)
        conv_gate(f % n_buf, c0)
    acc = _dot(a_ref[...], wdn_ref[...])
    for s in range(d // LANES):
        y_ref[s, pl.ds(0, half, stride=2), :] = acc[:half, s * LANES:(s + 1) * LANES]
        y_ref[s, pl.ds(1, half, stride=2), :] = acc[half:, s * LANES:(s + 1) * LANES]
    ffn = jnp.concatenate([y_ref[s] for s in range(d // LANES)], axis=1)
    y = (o_ref[0] if mixer else x_ref[0]) + g2_ref[0] * ffn
    if final_norm:
        y = _rms(y, fg_ref[...])
    o_ref[0] = y


def _halo_specs(width, tm, n, rows):
    hb = tm // rows
    last = n // rows - 1
    return [
        pl.BlockSpec((1, tm, width), lambda bb, i: (bb, i, 0)),
        pl.BlockSpec((1, rows, width), lambda bb, i: (bb, jnp.maximum(i * hb - 1, 0), 0)),
        pl.BlockSpec((1, rows, width), lambda bb, i: (bb, jnp.minimum((i + 1) * hb, last), 0)),
    ]


def _ffn(x, shift, scale, gate, norm_g, w_up, conv_w, conv_b, w_down, final_g, *, layer, tm,
         final_norm, mixer=None):
    b, n, d = x.shape
    d_ff = w_down.shape[1]
    vec = lambda bb, i: (bb, 0, 0)
    const2 = lambda bb, i: (0, 0)
    this_layer = lambda bb, i: (layer, 0, 0)
    resident = dict(pipeline_mode=pl.Buffered(1))
    mix_specs, mix_args = [], []
    if mixer is not None:
        oa, ob, w_out, g1 = mixer
        for o in (oa, ob):
            mix_specs += _halo_specs(o.shape[2], tm, n, MIX_HALO)
            mix_args += [o, o, o]
        mix_specs += [pl.BlockSpec(w_out.shape, const2, **resident), pl.BlockSpec((1, 1, d), vec)]
        mix_args += [w_out, g1]
    return pl.pallas_call(
        functools.partial(_ffn_kernel, tm=tm, d_ff=d_ff, final_norm=final_norm,
                          mixer=mixer is not None),
        grid=(b, n // tm),
        in_specs=_halo_specs(d, tm, n, HALO) + [
            pl.BlockSpec((1, 1, d), vec),
            pl.BlockSpec((1, 1, d), vec),
            pl.BlockSpec((1, 1, d), vec),
            pl.BlockSpec((1, d), const2),
            pl.BlockSpec((None,) + w_up.shape[1:], this_layer, **resident),
            pl.BlockSpec(conv_w.shape, const2),
            pl.BlockSpec((1, 2 * d_ff), const2),
            pl.BlockSpec((None,) + w_down.shape[1:], this_layer, **resident),
            pl.BlockSpec((1, d), const2),
        ] + mix_specs,
        out_specs=pl.BlockSpec((1, tm, d), lambda bb, i: (bb, i, 0)),
        out_shape=jax.ShapeDtypeStruct(x.shape, F32),
        scratch_shapes=[
            pltpu.VMEM((FFN_SLOTS, 2, FFN_CHUNK // LANES, tm + 2 * HALO, LANES), F32),
            pltpu.VMEM((tm, d_ff), BF16),
            pltpu.VMEM((d // LANES, tm, LANES), F32),
        ],
        compiler_params=_cparams(("parallel", "parallel")),
        name="conv_ffn",
    )(x, x, x, shift, scale, gate, norm_g, w_up, conv_w, conv_b.reshape(1, -1), w_down, final_g,
      *mix_args)


F1_NC = 8
F2_NK = 16


def _dft_tables(n_tok, cg):
    w = GRID_W
    idx = np.arange(w, dtype=np.float64)
    a64 = 2.0 * np.pi * np.outer(idx, idx) / w
    fc, fs = np.cos(a64), np.sin(a64)
    a1 = 2.0 * np.pi * idx[None, :, None] * (w * idx[None, None, :] + idx[:, None, None]) / n_tok
    step1 = np.block([[np.cos(a1), np.sin(a1)], [-np.sin(a1), np.cos(a1)]])
    eye = np.eye(F2_NK)
    step3 = (np.kron(fc[:w // 2 + 1], eye), np.kron(fs[:w // 2 + 1], eye))
    ch = np.arange(cg, dtype=np.float64)
    ach = 2.0 * np.pi * np.outer(ch, ch) / cg
    chan = np.concatenate([np.cos(ach), -np.sin(ach)], axis=1)
    f = lambda t: jnp.asarray(t, F32)
    return f(chan), f(step1), f(step3[0]), f(step3[1])


def _f1_kernel(x_ref, sh_ref, sc_ref, g_ref, chan_ref, s1_ref, ar_ref, ai_ref, u_ref, *, nc, d):
    w = GRID_W
    cg = d // FOURIER_GROUPS
    tiles = d // LANES
    gt = cg // LANES
    x = x_ref[0].reshape(w * nc, d)
    h = _norm_mod(x, g_ref[...], sh_ref[0], sc_ref[0]).astype(BF16)
    for g in range(FOURIER_GROUPS):
        u = _dot(h[:, g * cg:(g + 1) * cg], chan_ref[...])
        for t in range(2 * gt):
            part, tt = divmod(t, gt)
            u_ref[part * tiles + g * gt + tt] = u[:, t * LANES:(t + 1) * LANES]
    for c in range(nc):
        rows = pl.ds(c, w, stride=nc)
        ur = jnp.concatenate([u_ref[s, rows, :] for s in range(tiles)], axis=1)
        ui = jnp.concatenate([u_ref[tiles + s, rows, :] for s in range(tiles)], axis=1)
        a = _dot(s1_ref[c], jnp.concatenate([ur, ui], axis=0).astype(BF16))
        ar_ref[0, c] = a[:w].astype(BF16)
        ai_ref[0, c] = a[w:].astype(BF16)


def _f2_kernel(ar_ref, ai_ref, x_ref, g1_ref, kc_ref, ks_ref, w_ref, o_ref, *, nk, d, inv_norm):
    w = GRID_W
    a_r = ar_ref[0].reshape(w * nk, d)
    a_i = ai_ref[0].reshape(w * nk, d)
    p = _dot(kc_ref[...], a_r)
    q = _dot(ks_ref[...], a_i)
    diff = p - q
    mirrored = [diff[(w - k2) * nk:(w - k2 + 1) * nk] for k2 in range(w // 2 + 1, w)]
    z = jnp.concatenate([p + q] + mirrored, axis=0)
    y = _dot((z * inv_norm).astype(BF16), w_ref[...])
    o_ref[0] = x_ref[0] + g1_ref[0] * y.reshape(w, nk, d)


def _fourier(x, shift, scale, gate, norm_g, w_out):
    b, n, d = x.shape
    w = GRID_W
    assert n == w * w
    nc, nk = F1_NC, F2_NK
    cg = d // FOURIER_GROUPS
    chan, step1, kc, ks = _dft_tables(n, cg)
    chan, step1, kc, ks = (t.astype(BF16) for t in (chan, step1, kc, ks))
    x4 = x.reshape(b, w, w, d)
    vec = lambda bb, j: (bb, 0, 0)
    const2 = lambda bb, j: (0, 0)
    resident = dict(pipeline_mode=pl.Buffered(1))
    a_r, a_i = pl.pallas_call(
        functools.partial(_f1_kernel, nc=nc, d=d),
        grid=(b, w // nc),
        in_specs=[
            pl.BlockSpec((1, w, nc, d), lambda bb, j: (bb, 0, j, 0)),
            pl.BlockSpec((1, 1, d), vec),
            pl.BlockSpec((1, 1, d), vec),
            pl.BlockSpec((1, d), const2),
            pl.BlockSpec(chan.shape, const2),
            pl.BlockSpec((nc,) + step1.shape[1:], lambda bb, j: (j, 0, 0)),
        ],
        out_specs=[pl.BlockSpec((1, nc, w, d), lambda bb, j: (bb, j, 0, 0))] * 2,
        out_shape=[jax.ShapeDtypeStruct((b, w, w, d), BF16)] * 2,
        scratch_shapes=[pltpu.VMEM((2 * d // LANES, w * nc, LANES), F32)],
        compiler_params=_cparams(("parallel", "parallel")),
        name="fourier_rows",
    )(x4, shift, scale, norm_g, chan, step1)
    out = pl.pallas_call(
        functools.partial(_f2_kernel, nk=nk, d=d, inv_norm=1.0 / float(np.sqrt(n * cg))),
        grid=(b, w // nk),
        in_specs=[
            pl.BlockSpec((1, w, nk, d), lambda bb, j: (bb, 0, j, 0)),
            pl.BlockSpec((1, w, nk, d), lambda bb, j: (bb, 0, j, 0)),
            pl.BlockSpec((1, w, nk, d), lambda bb, j: (bb, 0, j, 0)),
            pl.BlockSpec((1, 1, d), vec),
            pl.BlockSpec(kc.shape, const2, **resident),
            pl.BlockSpec(ks.shape, const2, **resident),
            pl.BlockSpec(w_out.shape, const2, **resident),
        ],
        out_specs=pl.BlockSpec((1, w, nk, d), lambda bb, j: (bb, 0, j, 0)),
        out_shape=jax.ShapeDtypeStruct((b, w, w, d), F32),
        compiler_params=_cparams(("parallel", "parallel")),
        name="fourier_cols",
    )(a_r, a_i, x4, gate, kc, ks, w_out)
    return out.reshape(b, n, d)


def _rope_tables(n):
    t = np.arange(n)
    row = (t // GRID_W).astype(np.float64)
    col = (t % GRID_W).astype(np.float64)
    pairs = HEAD_DIM_A // 4
    freqs = ROPE_THETA ** (-np.arange(pairs, dtype=np.float64) / pairs)
    ang = np.concatenate([row[:, None] * freqs, col[:, None] * freqs], axis=-1)
    cos, sin = np.cos(ang), np.sin(ang)
    return (jnp.asarray(np.concatenate([cos, cos], axis=-1), F32),
            jnp.asarray(np.concatenate([-sin, sin], axis=-1), F32))


def kernel(x, c, ctx, c_ctx, mod_w, mod_b, norm1_g, norm2_g, attn_w_in, attn_w_out, q_norm_g,
           k_norm_g, na_rpb, fourier_w_out, ffn_w_up, ffn_conv_w, ffn_conv_b, ffn_w_down, final_g):
    b, n, d = x.shape
    depth = mod_w.shape[0]
    assert depth == 2 and b + 1 <= 8

    vec8 = jnp.zeros((8, d), F32).at[:b].set(c).at[b].set(c_ctx)
    mod = _adaln(vec8, mod_w, mod_b)
    term = lambda l, k: mod[l, :b, k * d:(k + 1) * d].reshape(b, 1, d)
    cterm = lambda l, k: jnp.broadcast_to(mod[l, b, k * d:(k + 1) * d].reshape(1, 1, d), (b, 1, d))

    perm = np.concatenate([np.arange(0, HEAD_DIM_A, 2), np.arange(1, HEAD_DIM_A, 2)])
    w_in = _prep_w_in(attn_w_in, perm)
    qg = q_norm_g[0][perm].reshape(1, -1)
    kg = k_norm_g[0][perm].reshape(1, -1)
    n_ctx = ctx.shape[1]
    no_rope = (jnp.ones((n_ctx, HEAD_DIM_A), F32), jnp.zeros((n_ctx, HEAD_DIM_A), F32))
    g1row = norm1_g[0].reshape(1, d)

    qa, qb, ka, va, kb, vb = _proj(x, term(0, 0), term(0, 1), g1row, w_in, _rope_tables(n), qg, kg,
                                   with_q=True, tm=PROJ_TM, name="proj_latent")
    cka, cva, ckb, cvb = _proj(ctx, cterm(0, 0), cterm(0, 1), g1row, w_in, no_rope, qg, kg,
                               with_q=False, tm=n_ctx, name="proj_context")
    oa, (w_up, w_down, w_attn_out, w_fourier_out) = _gqa(
        qa, ka, va, cka, cva, (ffn_w_up, ffn_w_down, attn_w_out, fourier_w_out), tq=1024, tk=512)
    ob = _na(qb, kb, vb, ckb, cvb, _na_bias_table(na_rpb[0]))
    x = _ffn(x, term(0, 3), term(0, 4), term(0, 5), norm2_g[0].reshape(1, d), w_up,
             ffn_conv_w[0], ffn_conv_b[0], w_down, final_g.reshape(1, d),
             layer=0, tm=FFN_TM, final_norm=False,
             mixer=(oa, ob, w_attn_out[0], term(0, 2)))

    x = _fourier(x, term(1, 0), term(1, 1), term(1, 2), norm1_g[1].reshape(1, d),
                 w_fourier_out[0])
    x = _ffn(x, term(1, 3), term(1, 4), term(1, 5), norm2_g[1].reshape(1, d), w_up,
             ffn_conv_w[1], ffn_conv_b[1], w_down, final_g.reshape(1, d),
             layer=1, tm=FFN_TM, final_norm=True)
    return x
```

```python
import functools

import numpy as np
import jax
import jax.numpy as jnp
from jax import lax
from jax.experimental import pallas as pl
from jax.experimental.pallas import tpu as pltpu

F32 = jnp.float32
BF16 = jnp.bfloat16

GRID_W = 64
HEAD_DIM_A = 128
HEADS_A = 4
KV_HEADS_A = 2
HEAD_DIM_B = 64
HEADS_B = 8
NA_ROWS = 8
NA_COLS = 16
FOURIER_GROUPS = 4
ROPE_THETA = 10000.0
EPS = 1e-6
QA_COLS = HEADS_A * HEAD_DIM_A
KA_COLS = KV_HEADS_A * HEAD_DIM_A
B_COLS = HEADS_B * HEAD_DIM_B
NEG = -1e30
LOG2E = 1.4426950408889634
LANES = 128

VMEM_LIMIT = 56 * 1024 * 1024


def _cparams(sem):
    return pltpu.CompilerParams(dimension_semantics=sem, vmem_limit_bytes=VMEM_LIMIT)


def _rms(x, g):
    ms = jnp.mean(x * x, axis=-1, keepdims=True)
    return x * lax.rsqrt(ms + EPS) * g


def _norm_mod(x, g, shift, scale):
    return _rms(x, g) * (1.0 + scale) + shift


def _dot(a, b):
    return jnp.dot(a, b, preferred_element_type=F32)


def _dot_t(a, b):
    return lax.dot_general(a, b, (((1,), (1,)), ((), ())), preferred_element_type=F32)


def _side_cast_plan(weights, n_steps, step_of):
    args = [w.reshape(-1, w.shape[-1]) for w in weights]
    specs = [pl.BlockSpec((a.shape[0] // n_steps, a.shape[1]), lambda *g: (step_of(*g), 0))
             for a in args]
    shapes = [jax.ShapeDtypeStruct(a.shape, BF16) for a in args]
    return args, specs, shapes


def _side_cast(in_refs, out_refs):
    for src, dst in zip(in_refs, out_refs):
        dst[...] = src[...].astype(dst.dtype)


W_IN_HEADS = 4


def _w_in_kernel(w_ref, p_ref, o_ref):
    q_blocks = (QA_COLS + B_COLS) // HEAD_DIM_A
    for k in range(W_IN_HEADS):
        head = pl.program_id(0) * W_IN_HEADS + k
        rotary = (head < HEADS_A) | ((head >= q_blocks) & (head < q_blocks + KV_HEADS_A))
        cols = slice(k * HEAD_DIM_A, (k + 1) * HEAD_DIM_A)
        w = w_ref[:, cols].astype(BF16)

        @pl.when(rotary)
        def _():
            o_ref[:, cols] = _dot(w, p_ref[...]).astype(BF16)

        @pl.when(jnp.logical_not(rotary))
        def _():
            o_ref[:, cols] = w


def _prep_w_in(attn_w_in, perm):
    _, d, ncol = attn_w_in.shape
    p = np.zeros((HEAD_DIM_A, HEAD_DIM_A), np.float32)
    p[perm, np.arange(HEAD_DIM_A)] = 1.0
    return pl.pallas_call(
        _w_in_kernel,
        grid=(ncol // (W_IN_HEADS * HEAD_DIM_A),),
        in_specs=[
            pl.BlockSpec((None, d, W_IN_HEADS * HEAD_DIM_A), lambda j: (0, 0, j)),
            pl.BlockSpec((HEAD_DIM_A, HEAD_DIM_A), lambda j: (0, 0)),
        ],
        out_specs=pl.BlockSpec((d, W_IN_HEADS * HEAD_DIM_A), lambda j: (0, j)),
        out_shape=jax.ShapeDtypeStruct((d, ncol), BF16),
        compiler_params=_cparams(("parallel",)),
        name="prep_w_in",
    )(attn_w_in, jnp.asarray(p).astype(BF16))


def _adaln_kernel(v_ref, w_ref, b_ref, o_ref):
    v = v_ref[...]
    s = v * jax.nn.sigmoid(v)
    w = w_ref[0]
    s_hi = s.astype(BF16)
    s_lo = (s - s_hi.astype(F32)).astype(BF16)
    w_hi = w.astype(BF16)
    w_lo = (w - w_hi.astype(F32)).astype(BF16)
    rows = s.shape[0]
    t = _dot(jnp.concatenate([s_hi, s_lo], axis=0), w_hi)
    o_ref[0] = t[:rows] + t[rows:] + _dot(s_hi, w_lo) + b_ref[0]


def _adaln(vec8, mod_w, mod_b):
    depth, d, n = mod_w.shape
    tn = 1536
    return pl.pallas_call(
        _adaln_kernel,
        grid=(depth, n // tn),
        in_specs=[
            pl.BlockSpec((8, d), lambda l, j: (0, 0)),
            pl.BlockSpec((1, d, tn), lambda l, j: (l, 0, j)),
            pl.BlockSpec((1, 1, tn), lambda l, j: (l, 0, j)),
        ],
        out_specs=pl.BlockSpec((1, 8, tn), lambda l, j: (l, 0, j)),
        out_shape=jax.ShapeDtypeStruct((depth, 8, n), F32),
        compiler_params=_cparams(("arbitrary", "arbitrary")),
        name="adaln",
    )(vec8, mod_w, mod_b.reshape(depth, 1, n))


PROJ_TM = 512


def _proj_kernel(x_ref, sh_ref, sc_ref, g_ref, w_ref, cos_ref, sin_ref, qg_ref, kg_ref,
                 *out_refs, with_q):
    h = _norm_mod(x_ref[0], g_ref[...], sh_ref[0], sc_ref[0]).astype(BF16)
    cosf = cos_ref[...]
    sinf = sin_ref[...]

    def rope_heads(p, n_heads, gain, scale, o_ref):
        for hd in range(n_heads):
            t = _rms(p[:, hd * HEAD_DIM_A:(hd + 1) * HEAD_DIM_A], gain)
            t = t * cosf + pltpu.roll(t, HEAD_DIM_A // 2, axis=1) * sinf
            o_ref[0, :, hd * HEAD_DIM_A:(hd + 1) * HEAD_DIM_A] = (t * scale).astype(BF16)

    col = QA_COLS + B_COLS
    if with_q:
        qa_ref, qb_ref, ka_ref, va_ref, kb_ref, vb_ref = out_refs
        rope_heads(_dot(h, w_ref[:, 0:QA_COLS]), HEADS_A, qg_ref[...], HEAD_DIM_A ** -0.5 * LOG2E,
                   qa_ref)
    else:
        ka_ref, va_ref, kb_ref, vb_ref = out_refs
    rope_heads(_dot(h, w_ref[:, col:col + KA_COLS]), KV_HEADS_A, kg_ref[...], 1.0, ka_ref)
    if with_q:
        qb = _dot(h, w_ref[:, QA_COLS:QA_COLS + B_COLS])
        qb_ref[0] = (qb * (HEAD_DIM_B ** -0.5 * LOG2E)).astype(BF16)
    col += KA_COLS
    va_ref[0] = _dot(h, w_ref[:, col:col + KA_COLS]).astype(BF16)
    col += KA_COLS
    kb_ref[0] = _dot(h, w_ref[:, col:col + B_COLS]).astype(BF16)
    col += B_COLS
    vb_ref[0] = _dot(h, w_ref[:, col:col + B_COLS]).astype(BF16)


def _proj(x, shift, scale, gain, w, rope, qg, kg, *, with_q, tm, name):
    b, n, d = x.shape
    ncol = w.shape[1]
    widths = ([QA_COLS, B_COLS] if with_q else []) + [KA_COLS, KA_COLS, B_COLS, B_COLS]
    vec = lambda bb, i: (bb, 0, 0)
    return pl.pallas_call(
        functools.partial(_proj_kernel, with_q=with_q),
        grid=(b, n // tm),
        in_specs=[
            pl.BlockSpec((1, tm, d), lambda bb, i: (bb, i, 0)),
            pl.BlockSpec((1, 1, d), vec),
            pl.BlockSpec((1, 1, d), vec),
            pl.BlockSpec((1, d), lambda bb, i: (0, 0)),
            pl.BlockSpec((d, ncol), lambda bb, i: (0, 0)),
            pl.BlockSpec((tm, HEAD_DIM_A), lambda bb, i: (i, 0)),
            pl.BlockSpec((tm, HEAD_DIM_A), lambda bb, i: (i, 0)),
            pl.BlockSpec((1, HEAD_DIM_A), lambda bb, i: (0, 0)),
            pl.BlockSpec((1, HEAD_DIM_A), lambda bb, i: (0, 0)),
        ],
        out_specs=[pl.BlockSpec((1, tm, wd), lambda bb, i: (bb, i, 0)) for wd in widths],
        out_shape=[jax.ShapeDtypeStruct((b, n, wd), BF16) for wd in widths],
        compiler_params=_cparams(("parallel", "parallel")),
        name=name,
    )(x, shift, scale, gain, w, *rope, qg, kg)


GQA_UNROLL = 8


def _flash_update(q, k, v, m_ref, l_ref, acc_ref):
    d = v.shape[1]
    s = _dot_t(q, k)
    m_old = m_ref[...]
    m_new = jnp.maximum(m_old, jnp.max(s, axis=-1, keepdims=True))
    alpha = jnp.exp2(m_old - m_new)
    p = jnp.exp2(s - jnp.tile(m_new, (1, s.shape[1] // m_new.shape[1])))
    o = _dot(p.astype(BF16), jnp.concatenate([v, jnp.ones_like(v)], axis=1))
    acc_ref[...] = alpha * acc_ref[...] + o[:, :d]
    l_ref[...] = alpha * l_ref[...] + o[:, d:]
    m_ref[...] = m_new


def _gqa_kernel(q_ref, k_ref, v_ref, ck_ref, cv_ref, *rest, tk, n_side):
    side_in, (o_ref, *side_out), (m_ref, l_ref, acc_ref) = (
        rest[:n_side], rest[n_side:2 * n_side + 1], rest[2 * n_side + 1:])
    _side_cast(side_in, side_out)
    group = q_ref.shape[2] // HEAD_DIM_A
    m_ref[...] = jnp.full(m_ref.shape, NEG, F32)
    l_ref[...] = jnp.zeros(l_ref.shape, F32)
    acc_ref[...] = jnp.zeros(acc_ref.shape, F32)

    def update(k, v):
        for g in range(group):
            _flash_update(q_ref[0, :, g * HEAD_DIM_A:(g + 1) * HEAD_DIM_A], k, v,
                          m_ref.at[g], l_ref.at[g], acc_ref.at[g])

    def body(i, carry):
        off = pl.multiple_of(i * tk, tk)
        update(k_ref[0, pl.ds(off, tk), :], v_ref[0, pl.ds(off, tk), :])
        return carry

    lax.fori_loop(0, k_ref.shape[1] // tk, body, 0, unroll=GQA_UNROLL)
    update(ck_ref[0], cv_ref[0])
    o_ref[0] = jnp.concatenate([acc_ref[g] / l_ref[g] for g in range(group)], axis=1).astype(BF16)


def _gqa(qa, ka, va, cka, cva, side, *, tq, tk):
    b, n, _ = qa.shape
    c = cka.shape[1]
    group = HEADS_A // KV_HEADS_A
    gw = group * HEAD_DIM_A
    nq = n // tq
    side_args, side_specs, side_shapes = _side_cast_plan(
        side, b * KV_HEADS_A * nq, lambda bb, h, i: (bb * KV_HEADS_A + h) * nq + i)
    out = pl.pallas_call(
        functools.partial(_gqa_kernel, tk=tk, n_side=len(side)),
        grid=(b, KV_HEADS_A, nq),
        in_specs=[
            pl.BlockSpec((1, tq, gw), lambda bb, h, i: (bb, i, h)),
            pl.BlockSpec((1, n, HEAD_DIM_A), lambda bb, h, i: (bb, 0, h)),
            pl.BlockSpec((1, n, HEAD_DIM_A), lambda bb, h, i: (bb, 0, h)),
            pl.BlockSpec((1, c, HEAD_DIM_A), lambda bb, h, i: (bb, 0, h)),
            pl.BlockSpec((1, c, HEAD_DIM_A), lambda bb, h, i: (bb, 0, h)),
        ] + side_specs,
        out_specs=[pl.BlockSpec((1, tq, gw), lambda bb, h, i: (bb, i, h))] + side_specs,
        out_shape=[jax.ShapeDtypeStruct((b, n, QA_COLS), BF16)] + side_shapes,
        scratch_shapes=[
            pltpu.VMEM((group, tq, HEAD_DIM_A), F32),
            pltpu.VMEM((group, tq, HEAD_DIM_A), F32),
            pltpu.VMEM((group, tq, HEAD_DIM_A), F32),
        ],
        compiler_params=_cparams(("parallel", "parallel", "arbitrary")),
        name="gqa",
    )(qa, ka, va, cka, cva, *side_args)
    return out[0], [o.reshape(w.shape) for o, w in zip(out[1:], side)]


NA_UNROLL = 32


def _na_kernel(q_ref, k_ref, v_ref, ckt_ref, cv_ref, bias_ref, o_ref):
    rows = q_ref.shape[1] // GRID_W
    win = NA_ROWS * GRID_W
    lane = lax.broadcasted_iota(jnp.int32, (GRID_W, 2 * HEAD_DIM_B), 1)
    lo = lane < HEAD_DIM_B
    pw = 2 * HEAD_DIM_B

    def with_ones(v):
        return jnp.concatenate([v, jnp.ones_like(v)], axis=1)

    def body(r, carry):
        r0 = jnp.clip(r - NA_ROWS // 2, 0, rows - NA_ROWS)
        q = q_ref[0, pl.ds(pl.multiple_of(r * GRID_W, GRID_W), GRID_W), :]
        zero = jnp.zeros_like(q)
        qq = jnp.concatenate([jnp.where(lo, q, zero), jnp.where(lo, zero, q)], axis=0)
        koff = pl.multiple_of(r0 * GRID_W, GRID_W)
        s = _dot_t(qq, k_ref[0, pl.ds(koff, win), :]) + bias_ref[r - r0]
        sc = _dot(qq, ckt_ref[0])
        m = jnp.maximum(jnp.max(s, axis=-1, keepdims=True), jnp.max(sc, axis=-1, keepdims=True))
        p = jnp.exp2(s - m).astype(BF16)
        pc = jnp.exp2(sc - m).astype(BF16)
        o = _dot(p, with_ones(v_ref[0, pl.ds(koff, win), :])) + _dot(pc, with_ones(cv_ref[0]))
        o = o[:, :pw] / o[:, pw:]
        res = jnp.where(lo, o[:GRID_W], o[GRID_W:])
        o_ref[0, pl.ds(pl.multiple_of(r * GRID_W, GRID_W), GRID_W), :] = res.astype(BF16)
        return carry

    lax.fori_loop(0, rows, body, 0, unroll=NA_UNROLL)


def _na(qb, kb, vb, ckb_t, cvb, bias):
    b, n, _ = qb.shape
    c = ckb_t.shape[2]
    pw = 2 * HEAD_DIM_B
    n_cls = bias.shape[0]
    tok = lambda bb, h: (bb, 0, h)
    return pl.pallas_call(
        _na_kernel,
        grid=(b, HEADS_B // 2),
        in_specs=[
            pl.BlockSpec((1, n, pw), tok),
            pl.BlockSpec((1, n, pw), tok),
            pl.BlockSpec((1, n, pw), tok),
            pl.BlockSpec((1, pw, c), lambda bb, h: (bb, h, 0)),
            pl.BlockSpec((1, c, pw), tok),
            pl.BlockSpec((n_cls, 2 * GRID_W, NA_ROWS * GRID_W), lambda bb, h: (0, h, 0)),
        ],
        out_specs=pl.BlockSpec((1, n, pw), tok),
        out_shape=jax.ShapeDtypeStruct((b, n, B_COLS), BF16),
        compiler_params=_cparams(("parallel", "parallel")),
        name="natten",
    )(qb, kb, vb, ckb_t, cvb, bias)


def _na_bias_table(rpb):
    edge = GRID_W - NA_COLS
    n_off = rpb.shape[1]
    padded = jnp.pad(rpb, ((0, 0), (0, 1), (edge, 2 * GRID_W - rpb.shape[2] - edge)))
    return pl.pallas_call(
        functools.partial(_na_bias_kernel, edge=edge),
        grid=(NA_ROWS,),
        in_specs=[pl.BlockSpec((HEADS_B, n_off + 1, 2 * GRID_W), lambda s: (0, 0, 0))],
        out_specs=pl.BlockSpec((1, HEADS_B * GRID_W, NA_ROWS * GRID_W), lambda s: (s, 0, 0)),
        out_shape=jax.ShapeDtypeStruct((NA_ROWS, HEADS_B * GRID_W, NA_ROWS * GRID_W), F32),
        compiler_params=_cparams(("parallel",)),
        name="na_bias",
    )(padded)


def _na_bias_kernel(rp_ref, o_ref, *, edge):
    cls = pl.program_id(0)
    shape = (GRID_W, 2 * GRID_W)
    c = lax.broadcasted_iota(jnp.int32, shape, 0)
    lane = lax.broadcasted_iota(jnp.int32, shape, 1)
    lo = lane < GRID_W
    j = jnp.where(lo, lane, lane - GRID_W)
    cs = jnp.clip(c - NA_COLS // 2, 0, edge)
    valid = (j >= cs) & (j < cs + NA_COLS)

    def toeplitz(h, i, shift):
        row = rp_ref[h, pl.ds(i - cls + NA_ROWS - 1, 1), :]
        return pltpu.roll(jnp.broadcast_to(row, shape), shift, 1, stride=1, stride_axis=0)

    for h in range(HEADS_B):
        for pair in range(NA_ROWS // 2):
            t = jnp.where(lo, toeplitz(h, 2 * pair, GRID_W + 1), toeplitz(h, 2 * pair + 1, 1))
            o_ref[0, h * GRID_W:(h + 1) * GRID_W, pair * 2 * GRID_W:(pair + 1) * 2 * GRID_W] = (
                jnp.where(valid, t * LOG2E, NEG))


HALO = 8
MIX_HALO = 16
FFN_CHUNK = 256
FFN_TM = 512
FFN_SLOTS = 4
FFN_ROWS = 32


def _ffn_kernel(x_ref, xp_ref, xn_ref, sh_ref, sc_ref, g2_ref, ng_ref, wup_ref, cw_ref, cb_ref,
                wdn_ref, fg_ref, *rest, tm, d_ff, final_norm, mixer):
    if mixer:
        (oa_ref, oap_ref, oan_ref, ob_ref, obp_ref, obn_ref, wo_ref, g1_ref,
         o_ref, u_ref, a_ref, y_ref) = rest
        ka = oa_ref.shape[2]
        rows = lambda main, prev, nxt: jnp.concatenate([main[0], prev[0], nxt[0]], axis=0)
        mix = (_dot(rows(oa_ref, oap_ref, oan_ref), wo_ref[0:ka, :])
               + _dot(rows(ob_ref, obp_ref, obn_ref), wo_ref[ka:, :]))
        g1 = g1_ref[0]
        lo = tm + MIX_HALO - HALO
        x_tile = x_ref[0] + g1 * mix[:tm]
        x_prev = xp_ref[0] + g1 * mix[lo:lo + HALO]
        x_next = xn_ref[0] + g1 * mix[tm + MIX_HALO:tm + MIX_HALO + HALO]
        o_ref[0] = x_tile
    else:
        o_ref, u_ref, a_ref, y_ref = rest
        x_tile, x_prev, x_next = x_ref[0], xp_ref[0], xn_ref[0]
    i = pl.program_id(1)
    ng, sh, sc = ng_ref[...], sh_ref[0], sc_ref[0]
    d = x_ref.shape[2]
    half = tm // 2
    slabs = FFN_CHUNK // LANES

    hp = jnp.where(i > 0, _norm_mod(x_prev, ng, sh, sc), 0.0)
    hn = jnp.where(i < pl.num_programs(1) - 1, _norm_mod(x_next, ng, sh, sc), 0.0)
    h = jnp.concatenate([hp, _norm_mod(x_tile, ng, sh, sc), hn], axis=0).astype(BF16)

    def project(buf, part, col):
        u = _dot(h, wup_ref[:, col:col + FFN_CHUNK])
        for s in range(slabs):
            u_ref[buf, part, s] = u[:, s * LANES:(s + 1) * LANES]

    def conv_gate(buf, c0):
        for s in range(slabs):
            cols = (c0 + s * LANES, d_ff + c0 + s * LANES)
            taps_w = [[cw_ref[k:k + 1, c:c + LANES] for k in range(3)] for c in cols]
            bias = [cb_ref[:, c:c + LANES] for c in cols]
            for rb in range(half // FFN_ROWS):
                res = []
                for part in range(2):
                    t = [u_ref[buf, part, s, pl.ds(HALO - 1 + k + 2 * rb * FFN_ROWS, FFN_ROWS, stride=2), :]
                         for k in range(4)]
                    w0, w1, w2 = taps_w[part]
                    res.append((t[0] * w0 + t[1] * w1 + t[2] * w2 + bias[part],
                                t[1] * w0 + t[2] * w1 + t[3] * w2 + bias[part]))
                (g_even, g_odd), (v_even, v_odd) = res
                r0 = rb * FFN_ROWS
                col = slice(c0 + s * LANES, c0 + (s + 1) * LANES)
                a_ref[r0:r0 + FFN_ROWS, col] = (g_even * jax.nn.sigmoid(g_even) * v_even).astype(BF16)
                a_ref[half + r0:half + r0 + FFN_ROWS, col] = (
                    g_odd * jax.nn.sigmoid(g_odd) * v_odd).astype(BF16)

    n_chunks = d_ff // FFN_CHUNK
    n_buf = u_ref.shape[0]

    def project_chunk(f):
        project(f % n_buf, 0, f * FFN_CHUNK)
        project(f % n_buf, 1, d_ff + f * FFN_CHUNK)

    for f in range(min(n_buf - 1, n_chunks)):
        project_chunk(f)
    for f in range(n_chunks):
        c0 = f * FFN_CHUNK
        if f + n_buf - 1 < n_chunks:
            project_chunk(f + n_buf - 1)
        conv_gate(f % n_buf, c0)
    acc = _dot(a_ref[...], wdn_ref[...])
    for s in range(d // LANES):
        y_ref[s, pl.ds(0, half, stride=2), :] = acc[:half, s * LANES:(s + 1) * LANES]
        y_ref[s, pl.ds(1, half, stride=2), :] = acc[half:, s * LANES:(s + 1) * LANES]
    ffn = jnp.concatenate([y_ref[s] for s in range(d // LANES)], axis=1)
    y = (o_ref[0] if mixer else x_ref[0]) + g2_ref[0] * ffn
    if final_norm:
        y = _rms(y, fg_ref[...])
    o_ref[0] = y


def _halo_specs(width, tm, n, rows):
    hb = tm // rows
    last = n // rows - 1
    return [
        pl.BlockSpec((1, tm, width), lambda bb, i: (bb, i, 0)),
        pl.BlockSpec((1, rows, width), lambda bb, i: (bb, jnp.maximum(i * hb - 1, 0), 0)),
        pl.BlockSpec((1, rows, width), lambda bb, i: (bb, jnp.minimum((i + 1) * hb, last), 0)),
    ]


def _ffn(x, shift, scale, gate, norm_g, w_up, conv_w, conv_b, w_down, final_g, *, layer, tm,
         final_norm, mixer=None):
    b, n, d = x.shape
    d_ff = w_down.shape[1]
    vec = lambda bb, i: (bb, 0, 0)
    const2 = lambda bb, i: (0, 0)
    this_layer = lambda bb, i: (layer, 0, 0)
    resident = dict(pipeline_mode=pl.Buffered(1))
    mix_specs, mix_args = [], []
    if mixer is not None:
        oa, ob, w_out, g1 = mixer
        for o in (oa, ob):
            mix_specs += _halo_specs(o.shape[2], tm, n, MIX_HALO)
            mix_args += [o, o, o]
        mix_specs += [pl.BlockSpec(w_out.shape, const2, **resident), pl.BlockSpec((1, 1, d), vec)]
        mix_args += [w_out, g1]
    return pl.pallas_call(
        functools.partial(_ffn_kernel, tm=tm, d_ff=d_ff, final_norm=final_norm,
                          mixer=mixer is not None),
        grid=(b, n // tm),
        in_specs=_halo_specs(d, tm, n, HALO) + [
            pl.BlockSpec((1, 1, d), vec),
            pl.BlockSpec((1, 1, d), vec),
            pl.BlockSpec((1, 1, d), vec),
            pl.BlockSpec((1, d), const2),
            pl.BlockSpec((None,) + w_up.shape[1:], this_layer, **resident),
            pl.BlockSpec(conv_w.shape, const2),
            pl.BlockSpec((1, 2 * d_ff), const2),
            pl.BlockSpec((None,) + w_down.shape[1:], this_layer, **resident),
            pl.BlockSpec((1, d), const2),
        ] + mix_specs,
        out_specs=pl.BlockSpec((1, tm, d), lambda bb, i: (bb, i, 0)),
        out_shape=jax.ShapeDtypeStruct(x.shape, F32),
        scratch_shapes=[
            pltpu.VMEM((FFN_SLOTS, 2, FFN_CHUNK // LANES, tm + 2 * HALO, LANES), F32),
            pltpu.VMEM((tm, d_ff), BF16),
            pltpu.VMEM((d // LANES, tm, LANES), F32),
        ],
        compiler_params=_cparams(("parallel", "parallel")),
        name="conv_ffn",
    )(x, x, x, shift, scale, gate, norm_g, w_up, conv_w, conv_b.reshape(1, -1), w_down, final_g,
      *mix_args)


F1_NC = 8
F2_NK = 16


def _dft_tables(n_tok, cg):
    w = GRID_W
    idx = np.arange(w, dtype=np.float64)
    a64 = 2.0 * np.pi * np.outer(idx, idx) / w
    fc, fs = np.cos(a64), np.sin(a64)
    a1 = 2.0 * np.pi * idx[None, :, None] * (w * idx[None, None, :] + idx[:, None, None]) / n_tok
    step1 = np.block([[np.cos(a1), np.sin(a1)], [-np.sin(a1), np.cos(a1)]])
    eye = np.eye(F2_NK)
    step3 = (np.kron(fc[:w // 2 + 1], eye), np.kron(fs[:w // 2 + 1], eye))
    ch = np.arange(cg, dtype=np.float64)
    ach = 2.0 * np.pi * np.outer(ch, ch) / cg
    chan = np.concatenate([np.cos(ach), -np.sin(ach)], axis=1)
    f = lambda t: jnp.asarray(t, F32)
    return f(chan), f(step1), f(step3[0]), f(step3[1])


def _f1_kernel(x_ref, sh_ref, sc_ref, g_ref, chan_ref, s1_ref, ar_ref, ai_ref, u_ref, *, nc, d):
    w = GRID_W
    cg = d // FOURIER_GROUPS
    tiles = d // LANES
    gt = cg // LANES
    x = x_ref[0].reshape(w * nc, d)
    h = _norm_mod(x, g_ref[...], sh_ref[0], sc_ref[0]).astype(BF16)
    for g in range(FOURIER_GROUPS):
        u = _dot(h[:, g * cg:(g + 1) * cg], chan_ref[...])
        for t in range(2 * gt):
            part, tt = divmod(t, gt)
            u_ref[part * tiles + g * gt + tt] = u[:, t * LANES:(t + 1) * LANES]
    for c in range(nc):
        rows = pl.ds(c, w, stride=nc)
        ur = jnp.concatenate([u_ref[s, rows, :] for s in range(tiles)], axis=1)
        ui = jnp.concatenate([u_ref[tiles + s, rows, :] for s in range(tiles)], axis=1)
        a = _dot(s1_ref[c], jnp.concatenate([ur, ui], axis=0).astype(BF16))
        ar_ref[0, c] = a[:w].astype(BF16)
        ai_ref[0, c] = a[w:].astype(BF16)


def _f2_kernel(ar_ref, ai_ref, x_ref, g1_ref, kc_ref, ks_ref, w_ref, o_ref, *, nk, d, inv_norm):
    w = GRID_W
    a_r = ar_ref[0].reshape(w * nk, d)
    a_i = ai_ref[0].reshape(w * nk, d)
    p = _dot(kc_ref[...], a_r)
    q = _dot(ks_ref[...], a_i)
    diff = p - q
    mirrored = [diff[(w - k2) * nk:(w - k2 + 1) * nk] for k2 in range(w // 2 + 1, w)]
    z = jnp.concatenate([p + q] + mirrored, axis=0)
    y = _dot((z * inv_norm).astype(BF16), w_ref[...])
    o_ref[0] = x_ref[0] + g1_ref[0] * y.reshape(w, nk, d)


def _fourier(x, shift, scale, gate, norm_g, w_out):
    b, n, d = x.shape
    w = GRID_W
    assert n == w * w
    nc, nk = F1_NC, F2_NK
    cg = d // FOURIER_GROUPS
    chan, step1, kc, ks = _dft_tables(n, cg)
    chan, step1, kc, ks = (t.astype(BF16) for t in (chan, step1, kc, ks))
    x4 = x.reshape(b, w, w, d)
    vec = lambda bb, j: (bb, 0, 0)
    const2 = lambda bb, j: (0, 0)
    resident = dict(pipeline_mode=pl.Buffered(1))
    a_r, a_i = pl.pallas_call(
        functools.partial(_f1_kernel, nc=nc, d=d),
        grid=(b, w // nc),
        in_specs=[
            pl.BlockSpec((1, w, nc, d), lambda bb, j: (bb, 0, j, 0)),
            pl.BlockSpec((1, 1, d), vec),
            pl.BlockSpec((1, 1, d), vec),
            pl.BlockSpec((1, d), const2),
            pl.BlockSpec(chan.shape, const2),
            pl.BlockSpec((nc,) + step1.shape[1:], lambda bb, j: (j, 0, 0)),
        ],
        out_specs=[pl.BlockSpec((1, nc, w, d), lambda bb, j: (bb, j, 0, 0))] * 2,
        out_shape=[jax.ShapeDtypeStruct((b, w, w, d), BF16)] * 2,
        scratch_shapes=[pltpu.VMEM((2 * d // LANES, w * nc, LANES), F32)],
        compiler_params=_cparams(("parallel", "parallel")),
        name="fourier_rows",
    )(x4, shift, scale, norm_g, chan, step1)
    out = pl.pallas_call(
        functools.partial(_f2_kernel, nk=nk, d=d, inv_norm=1.0 / float(np.sqrt(n * cg))),
        grid=(b, w // nk),
        in_specs=[
            pl.BlockSpec((1, w, nk, d), lambda bb, j: (bb, 0, j, 0)),
            pl.BlockSpec((1, w, nk, d), lambda bb, j: (bb, 0, j, 0)),
            pl.BlockSpec((1, w, nk, d), lambda bb, j: (bb, 0, j, 0)),
            pl.BlockSpec((1, 1, d), vec),
            pl.BlockSpec(kc.shape, const2, **resident),
            pl.BlockSpec(ks.shape, const2, **resident),
            pl.BlockSpec(w_out.shape, const2, **resident),
        ],
        out_specs=pl.BlockSpec((1, w, nk, d), lambda bb, j: (bb, 0, j, 0)),
        out_shape=jax.ShapeDtypeStruct((b, w, w, d), F32),
        compiler_params=_cparams(("parallel", "parallel")),
        name="fourier_cols",
    )(a_r, a_i, x4, gate, kc, ks, w_out)
    return out.reshape(b, n, d)


def _rope_tables(n):
    t = np.arange(n)
    row = (t // GRID_W).astype(np.float64)
    col = (t % GRID_W).astype(np.float64)
    pairs = HEAD_DIM_A // 4
    freqs = ROPE_THETA ** (-np.arange(pairs, dtype=np.float64) / pairs)
    ang = np.concatenate([row[:, None] * freqs, col[:, None] * freqs], axis=-1)
    cos, sin = np.cos(ang), np.sin(ang)
    return (jnp.asarray(np.concatenate([cos, cos], axis=-1), F32),
            jnp.asarray(np.concatenate([-sin, sin], axis=-1), F32))


def kernel(x, c, ctx, c_ctx, mod_w, mod_b, norm1_g, norm2_g, attn_w_in, attn_w_out, q_norm_g,
           k_norm_g, na_rpb, fourier_w_out, ffn_w_up, ffn_conv_w, ffn_conv_b, ffn_w_down, final_g):
    b, n, d = x.shape
    depth = mod_w.shape[0]
    assert depth == 2 and b + 1 <= 8

    vec8 = jnp.zeros((8, d), F32).at[:b].set(c).at[b].set(c_ctx)
    mod = _adaln(vec8, mod_w, mod_b)
    term = lambda l, k: mod[l, :b, k * d:(k + 1) * d].reshape(b, 1, d)
    cterm = lambda l, k: jnp.broadcast_to(mod[l, b, k * d:(k + 1) * d].reshape(1, 1, d), (b, 1, d))

    perm = np.concatenate([np.arange(0, HEAD_DIM_A, 2), np.arange(1, HEAD_DIM_A, 2)])
    w_in = _prep_w_in(attn_w_in, perm)
    qg = q_norm_g[0][perm].reshape(1, -1)
    kg = k_norm_g[0][perm].reshape(1, -1)
    n_ctx = ctx.shape[1]
    no_rope = (jnp.ones((n_ctx, HEAD_DIM_A), F32), jnp.zeros((n_ctx, HEAD_DIM_A), F32))
    g1row = norm1_g[0].reshape(1, d)

    qa, qb, ka, va, kb, vb = _proj(x, term(0, 0), term(0, 1), g1row, w_in, _rope_tables(n), qg, kg,
                                   with_q=True, tm=PROJ_TM, name="proj_latent")
    cka, cva, ckb, cvb = _proj(ctx, cterm(0, 0), cterm(0, 1), g1row, w_in, no_rope, qg, kg,
                               with_q=False, tm=n_ctx, name="proj_context")
    oa, (w_up, w_down, w_attn_out, w_fourier_out) = _gqa(
        qa, ka, va, cka, cva, (ffn_w_up, ffn_w_down, attn_w_out, fourier_w_out), tq=1024, tk=512)
    ob = _na(qb, kb, vb, jnp.swapaxes(ckb, 1, 2), cvb, _na_bias_table(na_rpb[0]))
    x = _ffn(x, term(0, 3), term(0, 4), term(0, 5), norm2_g[0].reshape(1, d), w_up,
             ffn_conv_w[0], ffn_conv_b[0], w_down, final_g.reshape(1, d),
             layer=0, tm=FFN_TM, final_norm=False,
             mixer=(oa, ob, w_attn_out[0], term(0, 2)))

    x = _fourier(x, term(1, 0), term(1, 1), term(1, 2), norm1_g[1].reshape(1, d),
                 w_fourier_out[0])
    x = _ffn(x, term(1, 3), term(1, 4), term(1, 5), norm2_g[1].reshape(1, d), w_up,
             ffn_conv_w[1], ffn_conv_b[1], w_down, final_g.reshape(1, d),
             layer=1, tm=FFN_TM, final_norm=True)
    return x
```

```python
import functools

import numpy as np
import jax
import jax.numpy as jnp
from jax import lax
from jax.experimental import pallas as pl
from jax.experimental.pallas import tpu as pltpu

F32 = jnp.float32
BF16 = jnp.bfloat16

GRID_W = 64
HEAD_DIM_A = 128
HEADS_A = 4
KV_HEADS_A = 2
HEAD_DIM_B = 64
HEADS_B = 8
NA_ROWS = 8
NA_COLS = 16
FOURIER_GROUPS = 4
ROPE_THETA = 10000.0
EPS = 1e-6
QA_COLS = HEADS_A * HEAD_DIM_A
KA_COLS = KV_HEADS_A * HEAD_DIM_A
B_COLS = HEADS_B * HEAD_DIM_B
NEG = -1e30
LOG2E = 1.4426950408889634
LANES = 128

VMEM_LIMIT = 56 * 1024 * 1024


def _cparams(sem):
    return pltpu.CompilerParams(dimension_semantics=sem, vmem_limit_bytes=VMEM_LIMIT)


def _rms(x, g):
    ms = jnp.mean(x * x, axis=-1, keepdims=True)
    return x * lax.rsqrt(ms + EPS) * g


def _norm_mod(x, g, shift, scale):
    return _rms(x, g) * (1.0 + scale) + shift


def _dot(a, b):
    return jnp.dot(a, b, preferred_element_type=F32)


def _dot_t(a, b):
    return lax.dot_general(a, b, (((1,), (1,)), ((), ())), preferred_element_type=F32)


def _side_cast_plan(weights, n_steps, step_of):
    args = [w.reshape(-1, w.shape[-1]) for w in weights]
    specs = [pl.BlockSpec((a.shape[0] // n_steps, a.shape[1]), lambda *g: (step_of(*g), 0))
             for a in args]
    shapes = [jax.ShapeDtypeStruct(a.shape, BF16) for a in args]
    return args, specs, shapes


def _side_cast(in_refs, out_refs):
    for src, dst in zip(in_refs, out_refs):
        dst[...] = src[...].astype(dst.dtype)


W_IN_HEADS = 4


def _w_in_kernel(w_ref, p_ref, o_ref):
    q_blocks = (QA_COLS + B_COLS) // HEAD_DIM_A
    for k in range(W_IN_HEADS):
        head = pl.program_id(0) * W_IN_HEADS + k
        rotary = (head < HEADS_A) | ((head >= q_blocks) & (head < q_blocks + KV_HEADS_A))
        cols = slice(k * HEAD_DIM_A, (k + 1) * HEAD_DIM_A)
        w = w_ref[:, cols].astype(BF16)

        @pl.when(rotary)
        def _():
            o_ref[:, cols] = _dot(w, p_ref[...]).astype(BF16)

        @pl.when(jnp.logical_not(rotary))
        def _():
            o_ref[:, cols] = w


def _prep_w_in(attn_w_in, perm):
    _, d, ncol = attn_w_in.shape
    p = np.zeros((HEAD_DIM_A, HEAD_DIM_A), np.float32)
    p[perm, np.arange(HEAD_DIM_A)] = 1.0
    return pl.pallas_call(
        _w_in_kernel,
        grid=(ncol // (W_IN_HEADS * HEAD_DIM_A),),
        in_specs=[
            pl.BlockSpec((None, d, W_IN_HEADS * HEAD_DIM_A), lambda j: (0, 0, j)),
            pl.BlockSpec((HEAD_DIM_A, HEAD_DIM_A), lambda j: (0, 0)),
        ],
        out_specs=pl.BlockSpec((d, W_IN_HEADS * HEAD_DIM_A), lambda j: (0, j)),
        out_shape=jax.ShapeDtypeStruct((d, ncol), BF16),
        compiler_params=_cparams(("parallel",)),
        name="prep_w_in",
    )(attn_w_in, jnp.asarray(p).astype(BF16))


def _adaln_kernel(v_ref, w_ref, b_ref, o_ref):
    v = v_ref[...]
    s = v * jax.nn.sigmoid(v)
    w = w_ref[0]
    s_hi = s.astype(BF16)
    s_lo = (s - s_hi.astype(F32)).astype(BF16)
    w_hi = w.astype(BF16)
    w_lo = (w - w_hi.astype(F32)).astype(BF16)
    rows = s.shape[0]
    t = _dot(jnp.concatenate([s_hi, s_lo], axis=0), w_hi)
    o_ref[0] = t[:rows] + t[rows:] + _dot(s_hi, w_lo) + b_ref[0]


def _adaln(vec8, mod_w, mod_b):
    depth, d, n = mod_w.shape
    tn = 1536
    return pl.pallas_call(
        _adaln_kernel,
        grid=(depth, n // tn),
        in_specs=[
            pl.BlockSpec((8, d), lambda l, j: (0, 0)),
            pl.BlockSpec((1, d, tn), lambda l, j: (l, 0, j)),
            pl.BlockSpec((1, 1, tn), lambda l, j: (l, 0, j)),
        ],
        out_specs=pl.BlockSpec((1, 8, tn), lambda l, j: (l, 0, j)),
        out_shape=jax.ShapeDtypeStruct((depth, 8, n), F32),
        compiler_params=_cparams(("arbitrary", "arbitrary")),
        name="adaln",
    )(vec8, mod_w, mod_b.reshape(depth, 1, n))


PROJ_TM = 512


def _proj_kernel(x_ref, sh_ref, sc_ref, g_ref, w_ref, cos_ref, sin_ref, qg_ref, kg_ref,
                 *out_refs, with_q):
    h = _norm_mod(x_ref[0], g_ref[...], sh_ref[0], sc_ref[0]).astype(BF16)
    cosf = cos_ref[...]
    sinf = sin_ref[...]

    def rope_heads(p, n_heads, gain, scale, o_ref):
        for hd in range(n_heads):
            t = _rms(p[:, hd * HEAD_DIM_A:(hd + 1) * HEAD_DIM_A], gain)
            t = t * cosf + pltpu.roll(t, HEAD_DIM_A // 2, axis=1) * sinf
            o_ref[0, :, hd * HEAD_DIM_A:(hd + 1) * HEAD_DIM_A] = (t * scale).astype(BF16)

    col = QA_COLS + B_COLS
    if with_q:
        qa_ref, qb_ref, ka_ref, va_ref, kb_ref, vb_ref = out_refs
        rope_heads(_dot(h, w_ref[:, 0:QA_COLS]), HEADS_A, qg_ref[...], HEAD_DIM_A ** -0.5 * LOG2E,
                   qa_ref)
    else:
        ka_ref, va_ref, kb_ref, vb_ref = out_refs
    rope_heads(_dot(h, w_ref[:, col:col + KA_COLS]), KV_HEADS_A, kg_ref[...], 1.0, ka_ref)
    if with_q:
        qb = _dot(h, w_ref[:, QA_COLS:QA_COLS + B_COLS])
        qb_ref[0] = (qb * (HEAD_DIM_B ** -0.5 * LOG2E)).astype(BF16)
    col += KA_COLS
    va_ref[0] = _dot(h, w_ref[:, col:col + KA_COLS]).astype(BF16)
    col += KA_COLS
    kb_ref[0] = _dot(h, w_ref[:, col:col + B_COLS]).astype(BF16)
    col += B_COLS
    vb_ref[0] = _dot(h, w_ref[:, col:col + B_COLS]).astype(BF16)


def _proj(x, shift, scale, gain, w, rope, qg, kg, *, with_q, tm, name):
    b, n, d = x.shape
    ncol = w.shape[1]
    widths = ([QA_COLS, B_COLS] if with_q else []) + [KA_COLS, KA_COLS, B_COLS, B_COLS]
    vec = lambda bb, i: (bb, 0, 0)
    return pl.pallas_call(
        functools.partial(_proj_kernel, with_q=with_q),
        grid=(b, n // tm),
        in_specs=[
            pl.BlockSpec((1, tm, d), lambda bb, i: (bb, i, 0)),
            pl.BlockSpec((1, 1, d), vec),
            pl.BlockSpec((1, 1, d), vec),
            pl.BlockSpec((1, d), lambda bb, i: (0, 0)),
            pl.BlockSpec((d, ncol), lambda bb, i: (0, 0)),
            pl.BlockSpec((tm, HEAD_DIM_A), lambda bb, i: (i, 0)),
            pl.BlockSpec((tm, HEAD_DIM_A), lambda bb, i: (i, 0)),
            pl.BlockSpec((1, HEAD_DIM_A), lambda bb, i: (0, 0)),
            pl.BlockSpec((1, HEAD_DIM_A), lambda bb, i: (0, 0)),
        ],
        out_specs=[pl.BlockSpec((1, tm, wd), lambda bb, i: (bb, i, 0)) for wd in widths],
        out_shape=[jax.ShapeDtypeStruct((b, n, wd), BF16) for wd in widths],
        compiler_params=_cparams(("parallel", "parallel")),
        name=name,
    )(x, shift, scale, gain, w, *rope, qg, kg)


GQA_UNROLL = 8


def _flash_update(q, k, v, m_ref, l_ref, acc_ref):
    d = v.shape[1]
    s = _dot_t(q, k)
    m_old = m_ref[...]
    m_new = jnp.maximum(m_old, jnp.max(s, axis=-1, keepdims=True))
    alpha = jnp.exp2(m_old - m_new)
    p = jnp.exp2(s - jnp.tile(m_new, (1, s.shape[1] // m_new.shape[1])))
    o = _dot(p.astype(BF16), jnp.concatenate([v, jnp.ones_like(v)], axis=1))
    acc_ref[...] = alpha * acc_ref[...] + o[:, :d]
    l_ref[...] = alpha * l_ref[...] + o[:, d:]
    m_ref[...] = m_new


def _gqa_kernel(q_ref, k_ref, v_ref, ck_ref, cv_ref, *rest, tk, n_side):
    side_in, rp_ref = rest[:n_side], rest[n_side]
    o_ref, side_out, bias_ref = rest[n_side + 1], rest[n_side + 2:2 * n_side + 2], rest[2 * n_side + 2]
    m_ref, l_ref, acc_ref = rest[2 * n_side + 3:]
    _side_cast(side_in, side_out)
    step = (pl.program_id(0) * pl.num_programs(1) + pl.program_id(1)) * pl.num_programs(2) \
        + pl.program_id(2)
    _na_bias_block(rp_ref, bias_ref, step // NA_PAIRS, 2 * (step % NA_PAIRS))
    group = q_ref.shape[2] // HEAD_DIM_A
    m_ref[...] = jnp.full(m_ref.shape, NEG, F32)
    l_ref[...] = jnp.zeros(l_ref.shape, F32)
    acc_ref[...] = jnp.zeros(acc_ref.shape, F32)

    def update(k, v):
        for g in range(group):
            _flash_update(q_ref[0, :, g * HEAD_DIM_A:(g + 1) * HEAD_DIM_A], k, v,
                          m_ref.at[g], l_ref.at[g], acc_ref.at[g])

    def body(i, carry):
        off = pl.multiple_of(i * tk, tk)
        update(k_ref[0, pl.ds(off, tk), :], v_ref[0, pl.ds(off, tk), :])
        return carry

    lax.fori_loop(0, k_ref.shape[1] // tk, body, 0, unroll=GQA_UNROLL)
    update(ck_ref[0], cv_ref[0])
    o_ref[0] = jnp.concatenate([acc_ref[g] / l_ref[g] for g in range(group)], axis=1).astype(BF16)


def _gqa(qa, ka, va, cka, cva, side, rpb, *, tq, tk):
    b, n, _ = qa.shape
    c = cka.shape[1]
    group = HEADS_A // KV_HEADS_A
    gw = group * HEAD_DIM_A
    nq = n // tq
    n_steps = b * KV_HEADS_A * nq
    assert n_steps == NA_ROWS * NA_PAIRS, "one bias block (class, head pair) per grid step"
    step_of = lambda bb, h, i: (bb * KV_HEADS_A + h) * nq + i
    side_args, side_specs, side_shapes = _side_cast_plan(side, n_steps, step_of)
    rp = _na_bias_padded(rpb)
    bias_spec = pl.BlockSpec((1, 2 * GRID_W, NA_ROWS * GRID_W),
                             lambda *g: (step_of(*g) // NA_PAIRS, step_of(*g) % NA_PAIRS, 0))
    bias_shape = jax.ShapeDtypeStruct((NA_ROWS, HEADS_B * GRID_W, NA_ROWS * GRID_W), F32)
    out = pl.pallas_call(
        functools.partial(_gqa_kernel, tk=tk, n_side=len(side)),
        grid=(b, KV_HEADS_A, nq),
        in_specs=[
            pl.BlockSpec((1, tq, gw), lambda bb, h, i: (bb, i, h)),
            pl.BlockSpec((1, n, HEAD_DIM_A), lambda bb, h, i: (bb, 0, h)),
            pl.BlockSpec((1, n, HEAD_DIM_A), lambda bb, h, i: (bb, 0, h)),
            pl.BlockSpec((1, c, HEAD_DIM_A), lambda bb, h, i: (bb, 0, h)),
            pl.BlockSpec((1, c, HEAD_DIM_A), lambda bb, h, i: (bb, 0, h)),
        ] + side_specs + [pl.BlockSpec(rp.shape, lambda bb, h, i: (0, 0, 0))],
        out_specs=[pl.BlockSpec((1, tq, gw), lambda bb, h, i: (bb, i, h))] + side_specs
        + [bias_spec],
        out_shape=[jax.ShapeDtypeStruct((b, n, QA_COLS), BF16)] + side_shapes + [bias_shape],
        scratch_shapes=[
            pltpu.VMEM((group, tq, HEAD_DIM_A), F32),
            pltpu.VMEM((group, tq, HEAD_DIM_A), F32),
            pltpu.VMEM((group, tq, HEAD_DIM_A), F32),
        ],
        compiler_params=_cparams(("parallel", "parallel", "arbitrary")),
        name="gqa",
    )(qa, ka, va, cka, cva, *side_args, rp)
    return out[0], [o.reshape(w.shape) for o, w in zip(out[1:-1], side)], out[-1]


NA_UNROLL = 32
NA_PAIRS = HEADS_B // 2


def _na_kernel(q_ref, k_ref, v_ref, ck_ref, cv_ref, bias_ref, o_ref):
    rows = q_ref.shape[1] // GRID_W
    win = NA_ROWS * GRID_W
    lane = lax.broadcasted_iota(jnp.int32, (GRID_W, 2 * HEAD_DIM_B), 1)
    lo = lane < HEAD_DIM_B
    pw = 2 * HEAD_DIM_B

    def with_ones(v):
        return jnp.concatenate([v, jnp.ones_like(v)], axis=1)

    def body(r, carry):
        r0 = jnp.clip(r - NA_ROWS // 2, 0, rows - NA_ROWS)
        q = q_ref[0, pl.ds(pl.multiple_of(r * GRID_W, GRID_W), GRID_W), :]
        zero = jnp.zeros_like(q)
        qq = jnp.concatenate([jnp.where(lo, q, zero), jnp.where(lo, zero, q)], axis=0)
        koff = pl.multiple_of(r0 * GRID_W, GRID_W)
        s = _dot_t(qq, k_ref[0, pl.ds(koff, win), :]) + bias_ref[r - r0]
        sc = _dot_t(qq, ck_ref[0])
        m = jnp.maximum(jnp.max(s, axis=-1, keepdims=True), jnp.max(sc, axis=-1, keepdims=True))
        p = jnp.exp2(s - m).astype(BF16)
        pc = jnp.exp2(sc - m).astype(BF16)
        o = _dot(p, with_ones(v_ref[0, pl.ds(koff, win), :])) + _dot(pc, with_ones(cv_ref[0]))
        o = o[:, :pw] / o[:, pw:]
        res = jnp.where(lo, o[:GRID_W], o[GRID_W:])
        o_ref[0, pl.ds(pl.multiple_of(r * GRID_W, GRID_W), GRID_W), :] = res.astype(BF16)
        return carry

    lax.fori_loop(0, rows, body, 0, unroll=NA_UNROLL)


def _na(qb, kb, vb, ckb, cvb, bias):
    b, n, _ = qb.shape
    c = ckb.shape[1]
    pw = 2 * HEAD_DIM_B
    n_cls = bias.shape[0]
    tok = lambda bb, h: (bb, 0, h)
    return pl.pallas_call(
        _na_kernel,
        grid=(b, HEADS_B // 2),
        in_specs=[
            pl.BlockSpec((1, n, pw), tok),
            pl.BlockSpec((1, n, pw), tok),
            pl.BlockSpec((1, n, pw), tok),
            pl.BlockSpec((1, c, pw), tok),
            pl.BlockSpec((1, c, pw), tok),
            pl.BlockSpec((n_cls, 2 * GRID_W, NA_ROWS * GRID_W), lambda bb, h: (0, h, 0)),
        ],
        out_specs=pl.BlockSpec((1, n, pw), tok),
        out_shape=jax.ShapeDtypeStruct((b, n, B_COLS), BF16),
        compiler_params=_cparams(("parallel", "parallel")),
        name="natten",
    )(qb, kb, vb, ckb, cvb, bias)


def _na_bias_padded(rpb):
    edge = GRID_W - NA_COLS
    return jnp.pad(rpb, ((0, 0), (0, 1), (edge, 2 * GRID_W - rpb.shape[2] - edge)))


def _na_bias_block(rp_ref, o_ref, cls, head0):
    edge = GRID_W - NA_COLS
    shape = (GRID_W, 2 * GRID_W)
    c = lax.broadcasted_iota(jnp.int32, shape, 0)
    lane = lax.broadcasted_iota(jnp.int32, shape, 1)
    lo = lane < GRID_W
    j = jnp.where(lo, lane, lane - GRID_W)
    cs = jnp.clip(c - NA_COLS // 2, 0, edge)
    valid = (j >= cs) & (j < cs + NA_COLS)

    def toeplitz(h, i, shift):
        row = rp_ref[h, pl.ds(i - cls + NA_ROWS - 1, 1), :]
        return pltpu.roll(jnp.broadcast_to(row, shape), shift, 1, stride=1, stride_axis=0)

    for k in range(2):
        for pair in range(NA_ROWS // 2):
            t = jnp.where(lo, toeplitz(head0 + k, 2 * pair, GRID_W + 1),
                          toeplitz(head0 + k, 2 * pair + 1, 1))
            o_ref[0, k * GRID_W:(k + 1) * GRID_W, pair * 2 * GRID_W:(pair + 1) * 2 * GRID_W] = (
                jnp.where(valid, t * LOG2E, NEG))


HALO = 8
MIX_HALO = 16
FFN_CHUNK = 256
FFN_TM = 512
FFN_SLOTS = 4
FFN_ROWS = 32


def _ffn_kernel(x_ref, xp_ref, xn_ref, sh_ref, sc_ref, g2_ref, ng_ref, wup_ref, cw_ref, cb_ref,
                wdn_ref, fg_ref, *rest, tm, d_ff, final_norm, mixer):
    if mixer:
        (oa_ref, oap_ref, oan_ref, ob_ref, obp_ref, obn_ref, wo_ref, g1_ref,
         o_ref, u_ref, a_ref, y_ref) = rest
        ka = oa_ref.shape[2]
        rows = lambda main, prev, nxt: jnp.concatenate([main[0], prev[0], nxt[0]], axis=0)
        mix = (_dot(rows(oa_ref, oap_ref, oan_ref), wo_ref[0:ka, :])
               + _dot(rows(ob_ref, obp_ref, obn_ref), wo_ref[ka:, :]))
        g1 = g1_ref[0]
        lo = tm + MIX_HALO - HALO
        x_tile = x_ref[0] + g1 * mix[:tm]
        x_prev = xp_ref[0] + g1 * mix[lo:lo + HALO]
        x_next = xn_ref[0] + g1 * mix[tm + MIX_HALO:tm + MIX_HALO + HALO]
        o_ref[0] = x_tile
    else:
        o_ref, u_ref, a_ref, y_ref = rest
        x_tile, x_prev, x_next = x_ref[0], xp_ref[0], xn_ref[0]
    i = pl.program_id(1)
    ng, sh, sc = ng_ref[...], sh_ref[0], sc_ref[0]
    d = x_ref.shape[2]
    half = tm // 2
    slabs = FFN_CHUNK // LANES

    hp = jnp.where(i > 0, _norm_mod(x_prev, ng, sh, sc), 0.0)
    hn = jnp.where(i < pl.num_programs(1) - 1, _norm_mod(x_next, ng, sh, sc), 0.0)
    h = jnp.concatenate([hp, _norm_mod(x_tile, ng, sh, sc), hn], axis=0).astype(BF16)

    def project(buf, part, col):
        u = _dot(h, wup_ref[:, col:col + FFN_CHUNK])
        for s in range(slabs):
            u_ref[buf, part, s] = u[:, s * LANES:(s + 1) * LANES]

    def conv_gate(buf, c0):
        for s in range(slabs):
            cols = (c0 + s * LANES, d_ff + c0 + s * LANES)
            taps_w = [[cw_ref[k:k + 1, c:c + LANES] for k in range(3)] for c in cols]
            bias = [cb_ref[:, c:c + LANES] for c in cols]
            for rb in range(half // FFN_ROWS):
                res = []
                for part in range(2):
                    t = [u_ref[buf, part, s, pl.ds(HALO - 1 + k + 2 * rb * FFN_ROWS, FFN_ROWS, stride=2), :]
                         for k in range(4)]
                    w0, w1, w2 = taps_w[part]
                    res.append((t[0] * w0 + t[1] * w1 + t[2] * w2 + bias[part],
                                t[1] * w0 + t[2] * w1 + t[3] * w2 + bias[part]))
                (g_even, g_odd), (v_even, v_odd) = res
                r0 = rb * FFN_ROWS
                col = slice(c0 + s * LANES, c0 + (s + 1) * LANES)
                a_ref[r0:r0 + FFN_ROWS, col] = (g_even * jax.nn.sigmoid(g_even) * v_even).astype(BF16)
                a_ref[half + r0:half + r0 + FFN_ROWS, col] = (
                    g_odd * jax.nn.sigmoid(g_odd) * v_odd).astype(BF16)

    n_chunks = d_ff // FFN_CHUNK
    n_buf = u_ref.shape[0]

    def project_chunk(f):
        project(f % n_buf, 0, f * FFN_CHUNK)
        project(f % n_buf, 1, d_ff + f * FFN_CHUNK)

    for f in range(min(n_buf - 1, n_chunks)):
        project_chunk(f)
    for f in range(n_chunks):
        c0 = f * FFN_CHUNK
        if f + n_buf - 1 < n_chunks:
            project_chunk(f + n_buf - 1)
        conv_gate(f % n_buf, c0)
    acc = _dot(a_ref[...], wdn_ref[...])
    for s in range(d // LANES):
        y_ref[s, pl.ds(0, half, stride=2), :] = acc[:half, s * LANES:(s + 1) * LANES]
        y_ref[s, pl.ds(1, half, stride=2), :] = acc[half:, s * LANES:(s + 1) * LANES]
    ffn = jnp.concatenate([y_ref[s] for s in range(d // LANES)], axis=1)
    y = (o_ref[0] if mixer else x_ref[0]) + g2_ref[0] * ffn
    if final_norm:
        y = _rms(y, fg_ref[...])
    o_ref[0] = y


def _halo_specs(width, tm, n, rows):
    hb = tm // rows
    last = n // rows - 1
    return [
        pl.BlockSpec((1, tm, width), lambda bb, i: (bb, i, 0)),
        pl.BlockSpec((1, rows, width), lambda bb, i: (bb, jnp.maximum(i * hb - 1, 0), 0)),
        pl.BlockSpec((1, rows, width), lambda bb, i: (bb, jnp.minimum((i + 1) * hb, last), 0)),
    ]


def _ffn(x, shift, scale, gate, norm_g, w_up, conv_w, conv_b, w_down, final_g, *, layer, tm,
         final_norm, mixer=None):
    b, n, d = x.shape
    d_ff = w_down.shape[1]
    vec = lambda bb, i: (bb, 0, 0)
    const2 = lambda bb, i: (0, 0)
    this_layer = lambda bb, i: (layer, 0, 0)
    resident = dict(pipeline_mode=pl.Buffered(1))
    mix_specs, mix_args = [], []
    if mixer is not None:
        oa, ob, w_out, g1 = mixer
        for o in (oa, ob):
            mix_specs += _halo_specs(o.shape[2], tm, n, MIX_HALO)
            mix_args += [o, o, o]
        mix_specs += [pl.BlockSpec(w_out.shape, const2, **resident), pl.BlockSpec((1, 1, d), vec)]
        mix_args += [w_out, g1]
    return pl.pallas_call(
        functools.partial(_ffn_kernel, tm=tm, d_ff=d_ff, final_norm=final_norm,
                          mixer=mixer is not None),
        grid=(b, n // tm),
        in_specs=_halo_specs(d, tm, n, HALO) + [
            pl.BlockSpec((1, 1, d), vec),
            pl.BlockSpec((1, 1, d), vec),
            pl.BlockSpec((1, 1, d), vec),
            pl.BlockSpec((1, d), const2),
            pl.BlockSpec((None,) + w_up.shape[1:], this_layer, **resident),
            pl.BlockSpec(conv_w.shape, const2),
            pl.BlockSpec((1, 2 * d_ff), const2),
            pl.BlockSpec((None,) + w_down.shape[1:], this_layer, **resident),
            pl.BlockSpec((1, d), const2),
        ] + mix_specs,
        out_specs=pl.BlockSpec((1, tm, d), lambda bb, i: (bb, i, 0)),
        out_shape=jax.ShapeDtypeStruct(x.shape, F32),
        scratch_shapes=[
            pltpu.VMEM((FFN_SLOTS, 2, FFN_CHUNK // LANES, tm + 2 * HALO, LANES), F32),
            pltpu.VMEM((tm, d_ff), BF16),
            pltpu.VMEM((d // LANES, tm, LANES), F32),
        ],
        compiler_params=_cparams(("parallel", "parallel")),
        name="conv_ffn",
    )(x, x, x, shift, scale, gate, norm_g, w_up, conv_w, conv_b.reshape(1, -1), w_down, final_g,
      *mix_args)


F1_NC = 8
F2_NK = 16


def _dft_tables(n_tok, cg):
    w = GRID_W
    idx = np.arange(w, dtype=np.float64)
    a64 = 2.0 * np.pi * np.outer(idx, idx) / w
    fc, fs = np.cos(a64), np.sin(a64)
    a1 = 2.0 * np.pi * idx[None, :, None] * (w * idx[None, None, :] + idx[:, None, None]) / n_tok
    step1 = np.block([[np.cos(a1), np.sin(a1)], [-np.sin(a1), np.cos(a1)]])
    eye = np.eye(F2_NK)
    step3 = (np.kron(fc[:w // 2 + 1], eye), np.kron(fs[:w // 2 + 1], eye))
    ch = np.arange(cg, dtype=np.float64)
    ach = 2.0 * np.pi * np.outer(ch, ch) / cg
    chan = np.concatenate([np.cos(ach), -np.sin(ach)], axis=1)
    f = lambda t: jnp.asarray(t, F32)
    return f(chan), f(step1), f(step3[0]), f(step3[1])


def _f1_kernel(x_ref, sh_ref, sc_ref, g_ref, chan_ref, s1_ref, ar_ref, ai_ref, u_ref, *, nc, d):
    w = GRID_W
    cg = d // FOURIER_GROUPS
    tiles = d // LANES
    gt = cg // LANES
    x = x_ref[0].reshape(w * nc, d)
    h = _norm_mod(x, g_ref[...], sh_ref[0], sc_ref[0]).astype(BF16)
    for g in range(FOURIER_GROUPS):
        u = _dot(h[:, g * cg:(g + 1) * cg], chan_ref[...])
        for t in range(2 * gt):
            part, tt = divmod(t, gt)
            u_ref[part * tiles + g * gt + tt] = u[:, t * LANES:(t + 1) * LANES]
    for c in range(nc):
        rows = pl.ds(c, w, stride=nc)
        ur = jnp.concatenate([u_ref[s, rows, :] for s in range(tiles)], axis=1)
        ui = jnp.concatenate([u_ref[tiles + s, rows, :] for s in range(tiles)], axis=1)
        a = _dot(s1_ref[c], jnp.concatenate([ur, ui], axis=0).astype(BF16))
        ar_ref[0, c] = a[:w].astype(BF16)
        ai_ref[0, c] = a[w:].astype(BF16)


def _f2_kernel(ar_ref, ai_ref, x_ref, g1_ref, kc_ref, ks_ref, w_ref, o_ref, *, nk, d, inv_norm):
    w = GRID_W
    a_r = ar_ref[0].reshape(w * nk, d)
    a_i = ai_ref[0].reshape(w * nk, d)
    p = _dot(kc_ref[...], a_r)
    q = _dot(ks_ref[...], a_i)
    diff = p - q
    mirrored = [diff[(w - k2) * nk:(w - k2 + 1) * nk] for k2 in range(w // 2 + 1, w)]
    z = jnp.concatenate([p + q] + mirrored, axis=0)
    y = _dot((z * inv_norm).astype(BF16), w_ref[...])
    o_ref[0] = x_ref[0] + g1_ref[0] * y.reshape(w, nk, d)


def _fourier(x, shift, scale, gate, norm_g, w_out):
    b, n, d = x.shape
    w = GRID_W
    assert n == w * w
    nc, nk = F1_NC, F2_NK
    cg = d // FOURIER_GROUPS
    chan, step1, kc, ks = _dft_tables(n, cg)
    chan, step1, kc, ks = (t.astype(BF16) for t in (chan, step1, kc, ks))
    x4 = x.reshape(b, w, w, d)
    vec = lambda bb, j: (bb, 0, 0)
    const2 = lambda bb, j: (0, 0)
    resident = dict(pipeline_mode=pl.Buffered(1))
    a_r, a_i = pl.pallas_call(
        functools.partial(_f1_kernel, nc=nc, d=d),
        grid=(b, w // nc),
        in_specs=[
            pl.BlockSpec((1, w, nc, d), lambda bb, j: (bb, 0, j, 0)),
            pl.BlockSpec((1, 1, d), vec),
            pl.BlockSpec((1, 1, d), vec),
            pl.BlockSpec((1, d), const2),
            pl.BlockSpec(chan.shape, const2),
            pl.BlockSpec((nc,) + step1.shape[1:], lambda bb, j: (j, 0, 0)),
        ],
        out_specs=[pl.BlockSpec((1, nc, w, d), lambda bb, j: (bb, j, 0, 0))] * 2,
        out_shape=[jax.ShapeDtypeStruct((b, w, w, d), BF16)] * 2,
        scratch_shapes=[pltpu.VMEM((2 * d // LANES, w * nc, LANES), F32)],
        compiler_params=_cparams(("parallel", "parallel")),
        name="fourier_rows",
    )(x4, shift, scale, norm_g, chan, step1)
    out = pl.pallas_call(
        functools.partial(_f2_kernel, nk=nk, d=d, inv_norm=1.0 / float(np.sqrt(n * cg))),
        grid=(b, w // nk),
        in_specs=[
            pl.BlockSpec((1, w, nk, d), lambda bb, j: (bb, 0, j, 0)),
            pl.BlockSpec((1, w, nk, d), lambda bb, j: (bb, 0, j, 0)),
            pl.BlockSpec((1, w, nk, d), lambda bb, j: (bb, 0, j, 0)),
            pl.BlockSpec((1, 1, d), vec),
            pl.BlockSpec(kc.shape, const2, **resident),
            pl.BlockSpec(ks.shape, const2, **resident),
            pl.BlockSpec(w_out.shape, const2, **resident),
        ],
        out_specs=pl.BlockSpec((1, w, nk, d), lambda bb, j: (bb, 0, j, 0)),
        out_shape=jax.ShapeDtypeStruct((b, w, w, d), F32),
        compiler_params=_cparams(("parallel", "parallel")),
        name="fourier_cols",
    )(a_r, a_i, x4, gate, kc, ks, w_out)
    return out.reshape(b, n, d)


def _rope_tables(n):
    t = np.arange(n)
    row = (t // GRID_W).astype(np.float64)
    col = (t % GRID_W).astype(np.float64)
    pairs = HEAD_DIM_A // 4
    freqs = ROPE_THETA ** (-np.arange(pairs, dtype=np.float64) / pairs)
    ang = np.concatenate([row[:, None] * freqs, col[:, None] * freqs], axis=-1)
    cos, sin = np.cos(ang), np.sin(ang)
    return (jnp.asarray(np.concatenate([cos, cos], axis=-1), F32),
            jnp.asarray(np.concatenate([-sin, sin], axis=-1), F32))


def kernel(x, c, ctx, c_ctx, mod_w, mod_b, norm1_g, norm2_g, attn_w_in, attn_w_out, q_norm_g,
           k_norm_g, na_rpb, fourier_w_out, ffn_w_up, ffn_conv_w, ffn_conv_b, ffn_w_down, final_g):
    b, n, d = x.shape
    depth = mod_w.shape[0]
    assert depth == 2 and b + 1 <= 8

    vec8 = jnp.zeros((8, d), F32).at[:b].set(c).at[b].set(c_ctx)
    mod = _adaln(vec8, mod_w, mod_b)
    term = lambda l, k: mod[l, :b, k * d:(k + 1) * d].reshape(b, 1, d)
    cterm = lambda l, k: jnp.broadcast_to(mod[l, b, k * d:(k + 1) * d].reshape(1, 1, d), (b, 1, d))

    perm = np.concatenate([np.arange(0, HEAD_DIM_A, 2), np.arange(1, HEAD_DIM_A, 2)])
    w_in = _prep_w_in(attn_w_in, perm)
    qg = q_norm_g[0][perm].reshape(1, -1)
    kg = k_norm_g[0][perm].reshape(1, -1)
    n_ctx = ctx.shape[1]
    no_rope = (jnp.ones((n_ctx, HEAD_DIM_A), F32), jnp.zeros((n_ctx, HEAD_DIM_A), F32))
    g1row = norm1_g[0].reshape(1, d)

    qa, qb, ka, va, kb, vb = _proj(x, term(0, 0), term(0, 1), g1row, w_in, _rope_tables(n), qg, kg,
                                   with_q=True, tm=PROJ_TM, name="proj_latent")
    cka, cva, ckb, cvb = _proj(ctx, cterm(0, 0), cterm(0, 1), g1row, w_in, no_rope, qg, kg,
                               with_q=False, tm=n_ctx, name="proj_context")
    oa, (w_up, w_down, w_attn_out, w_fourier_out), na_bias = _gqa(
        qa, ka, va, cka, cva, (ffn_w_up, ffn_w_down, attn_w_out, fourier_w_out), na_rpb[0],
        tq=1024, tk=1024)
    ob = _na(qb, kb, vb, ckb, cvb, na_bias)
    x = _ffn(x, term(0, 3), term(0, 4), term(0, 5), norm2_g[0].reshape(1, d), w_up,
             ffn_conv_w[0], ffn_conv_b[0], w_down, final_g.reshape(1, d),
             layer=0, tm=FFN_TM, final_norm=False,
             mixer=(oa, ob, w_attn_out[0], term(0, 2)))

    x = _fourier(x, term(1, 0), term(1, 1), term(1, 2), norm1_g[1].reshape(1, d),
                 w_fourier_out[0])
    x = _ffn(x, term(1, 3), term(1, 4), term(1, 5), norm2_g[1].reshape(1, d), w_up,
             ffn_conv_w[1], ffn_conv_b[1], w_down, final_g.reshape(1, d),
             layer=1, tm=FFN_TM, final_norm=True)
    return x
```

```python
import functools

import numpy as np
import jax
import jax.numpy as jnp
from jax import lax
from jax.experimental import pallas as pl
from jax.experimental.pallas import tpu as pltpu

F32 = jnp.float32
BF16 = jnp.bfloat16

GRID_W = 64
HEAD_DIM_A = 128
HEADS_A = 4
KV_HEADS_A = 2
HEAD_DIM_B = 64
HEADS_B = 8
NA_ROWS = 8
NA_COLS = 16
FOURIER_GROUPS = 4
ROPE_THETA = 10000.0
EPS = 1e-6
QA_COLS = HEADS_A * HEAD_DIM_A
KA_COLS = KV_HEADS_A * HEAD_DIM_A
B_COLS = HEADS_B * HEAD_DIM_B
NEG = -1e30
LOG2E = 1.4426950408889634
LANES = 128

VMEM_LIMIT = 56 * 1024 * 1024


def _cparams(sem):
    return pltpu.CompilerParams(dimension_semantics=sem, vmem_limit_bytes=VMEM_LIMIT)


def _rms(x, g):
    ms = jnp.mean(x * x, axis=-1, keepdims=True)
    return x * lax.rsqrt(ms + EPS) * g


def _norm_mod(x, g, shift, scale):
    return _rms(x, g) * (1.0 + scale) + shift


def _dot(a, b):
    return jnp.dot(a, b, preferred_element_type=F32)


def _dot_t(a, b):
    return lax.dot_general(a, b, (((1,), (1,)), ((), ())), preferred_element_type=F32)


def _side_cast_plan(weights, n_steps, step_of):
    args = [w.reshape(-1, w.shape[-1]) for w in weights]
    specs = [pl.BlockSpec((a.shape[0] // n_steps, a.shape[1]), lambda *g: (step_of(*g), 0))
             for a in args]
    shapes = [jax.ShapeDtypeStruct(a.shape, BF16) for a in args]
    return args, specs, shapes


def _side_cast(in_refs, out_refs):
    for src, dst in zip(in_refs, out_refs):
        dst[...] = src[...].astype(dst.dtype)


W_IN_HEADS = 4


def _w_in_kernel(w_ref, p_ref, o_ref):
    q_blocks = (QA_COLS + B_COLS) // HEAD_DIM_A
    for k in range(W_IN_HEADS):
        head = pl.program_id(0) * W_IN_HEADS + k
        rotary = (head < HEADS_A) | ((head >= q_blocks) & (head < q_blocks + KV_HEADS_A))
        cols = slice(k * HEAD_DIM_A, (k + 1) * HEAD_DIM_A)
        w = w_ref[:, cols].astype(BF16)

        @pl.when(rotary)
        def _():
            o_ref[:, cols] = _dot(w, p_ref[...]).astype(BF16)

        @pl.when(jnp.logical_not(rotary))
        def _():
            o_ref[:, cols] = w


def _prep_w_in(attn_w_in, perm):
    _, d, ncol = attn_w_in.shape
    p = np.zeros((HEAD_DIM_A, HEAD_DIM_A), np.float32)
    p[perm, np.arange(HEAD_DIM_A)] = 1.0
    return pl.pallas_call(
        _w_in_kernel,
        grid=(ncol // (W_IN_HEADS * HEAD_DIM_A),),
        in_specs=[
            pl.BlockSpec((None, d, W_IN_HEADS * HEAD_DIM_A), lambda j: (0, 0, j)),
            pl.BlockSpec((HEAD_DIM_A, HEAD_DIM_A), lambda j: (0, 0)),
        ],
        out_specs=pl.BlockSpec((d, W_IN_HEADS * HEAD_DIM_A), lambda j: (0, j)),
        out_shape=jax.ShapeDtypeStruct((d, ncol), BF16),
        compiler_params=_cparams(("parallel",)),
        name="prep_w_in",
    )(attn_w_in, jnp.asarray(p).astype(BF16))


def _adaln_kernel(v_ref, w_ref, b_ref, o_ref):
    v = v_ref[...]
    s = v * jax.nn.sigmoid(v)
    w = w_ref[0]
    s_hi = s.astype(BF16)
    s_lo = (s - s_hi.astype(F32)).astype(BF16)
    w_hi = w.astype(BF16)
    w_lo = (w - w_hi.astype(F32)).astype(BF16)
    rows = s.shape[0]
    t = _dot(jnp.concatenate([s_hi, s_lo], axis=0), w_hi)
    o_ref[0] = t[:rows] + t[rows:] + _dot(s_hi, w_lo) + b_ref[0]


def _adaln(vec8, mod_w, mod_b):
    depth, d, n = mod_w.shape
    tn = 3072
    return pl.pallas_call(
        _adaln_kernel,
        grid=(depth, n // tn),
        in_specs=[
            pl.BlockSpec((8, d), lambda l, j: (0, 0)),
            pl.BlockSpec((1, d, tn), lambda l, j: (l, 0, j)),
            pl.BlockSpec((1, 1, tn), lambda l, j: (l, 0, j)),
        ],
        out_specs=pl.BlockSpec((1, 8, tn), lambda l, j: (l, 0, j)),
        out_shape=jax.ShapeDtypeStruct((depth, 8, n), F32),
        compiler_params=_cparams(("arbitrary", "arbitrary")),
        name="adaln",
    )(vec8, mod_w, mod_b.reshape(depth, 1, n))


PROJ_TM = 512


def _proj_kernel(x_ref, sh_ref, sc_ref, g_ref, w_ref, cos_ref, sin_ref, qg_ref, kg_ref,
                 *out_refs, with_q):
    h = _norm_mod(x_ref[0], g_ref[...], sh_ref[0], sc_ref[0]).astype(BF16)
    cosf = cos_ref[...]
    sinf = sin_ref[...]

    def rope_heads(p, n_heads, gain, scale, o_ref):
        for hd in range(n_heads):
            t = _rms(p[:, hd * HEAD_DIM_A:(hd + 1) * HEAD_DIM_A], gain)
            t = t * cosf + pltpu.roll(t, HEAD_DIM_A // 2, axis=1) * sinf
            o_ref[0, :, hd * HEAD_DIM_A:(hd + 1) * HEAD_DIM_A] = (t * scale).astype(BF16)

    col = QA_COLS + B_COLS
    if with_q:
        qa_ref, qb_ref, ka_ref, va_ref, kb_ref, vb_ref = out_refs
        rope_heads(_dot(h, w_ref[:, 0:QA_COLS]), HEADS_A, qg_ref[...], HEAD_DIM_A ** -0.5 * LOG2E,
                   qa_ref)
    else:
        ka_ref, va_ref, kb_ref, vb_ref = out_refs
    rope_heads(_dot(h, w_ref[:, col:col + KA_COLS]), KV_HEADS_A, kg_ref[...], 1.0, ka_ref)
    if with_q:
        qb = _dot(h, w_ref[:, QA_COLS:QA_COLS + B_COLS])
        qb_ref[0] = (qb * (HEAD_DIM_B ** -0.5 * LOG2E)).astype(BF16)
    col += KA_COLS
    va_ref[0] = _dot(h, w_ref[:, col:col + KA_COLS]).astype(BF16)
    col += KA_COLS
    kb_ref[0] = _dot(h, w_ref[:, col:col + B_COLS]).astype(BF16)
    col += B_COLS
    vb_ref[0] = _dot(h, w_ref[:, col:col + B_COLS]).astype(BF16)


def _proj(x, shift, scale, gain, w, rope, qg, kg, *, with_q, tm, name):
    b, n, d = x.shape
    ncol = w.shape[1]
    widths = ([QA_COLS, B_COLS] if with_q else []) + [KA_COLS, KA_COLS, B_COLS, B_COLS]
    vec = lambda bb, i: (bb, 0, 0)
    return pl.pallas_call(
        functools.partial(_proj_kernel, with_q=with_q),
        grid=(b, n // tm),
        in_specs=[
            pl.BlockSpec((1, tm, d), lambda bb, i: (bb, i, 0)),
            pl.BlockSpec((1, 1, d), vec),
            pl.BlockSpec((1, 1, d), vec),
            pl.BlockSpec((1, d), lambda bb, i: (0, 0)),
            pl.BlockSpec((d, ncol), lambda bb, i: (0, 0)),
            pl.BlockSpec((tm, HEAD_DIM_A), lambda bb, i: (i, 0)),
            pl.BlockSpec((tm, HEAD_DIM_A), lambda bb, i: (i, 0)),
            pl.BlockSpec((1, HEAD_DIM_A), lambda bb, i: (0, 0)),
            pl.BlockSpec((1, HEAD_DIM_A), lambda bb, i: (0, 0)),
        ],
        out_specs=[pl.BlockSpec((1, tm, wd), lambda bb, i: (bb, i, 0)) for wd in widths],
        out_shape=[jax.ShapeDtypeStruct((b, n, wd), BF16) for wd in widths],
        compiler_params=_cparams(("parallel", "parallel")),
        name=name,
    )(x, shift, scale, gain, w, *rope, qg, kg)


GQA_UNROLL = 8


def _flash_update(q, k, v, m_ref, l_ref, acc_ref):
    d = v.shape[1]
    s = _dot_t(q, k)
    m_old = m_ref[...]
    m_new = jnp.maximum(m_old, jnp.max(s, axis=-1, keepdims=True))
    alpha = jnp.exp2(m_old - m_new)
    p = jnp.exp2(s - jnp.tile(m_new, (1, s.shape[1] // m_new.shape[1])))
    o = _dot(p.astype(BF16), jnp.concatenate([v, jnp.ones_like(v)], axis=1))
    acc_ref[...] = alpha * acc_ref[...] + o[:, :d]
    l_ref[...] = alpha * l_ref[...] + o[:, d:]
    m_ref[...] = m_new


def _gqa_kernel(q_ref, k_ref, v_ref, ck_ref, cv_ref, *rest, tk, n_side):
    side_in, rp_ref = rest[:n_side], rest[n_side]
    o_ref, side_out, bias_ref = rest[n_side + 1], rest[n_side + 2:2 * n_side + 2], rest[2 * n_side + 2]
    m_ref, l_ref, acc_ref = rest[2 * n_side + 3:]
    _side_cast(side_in, side_out)
    step = (pl.program_id(0) * pl.num_programs(1) + pl.program_id(1)) * pl.num_programs(2) \
        + pl.program_id(2)
    _na_bias_block(rp_ref, bias_ref, step // NA_PAIRS, 2 * (step % NA_PAIRS))
    group = q_ref.shape[2] // HEAD_DIM_A
    m_ref[...] = jnp.full(m_ref.shape, NEG, F32)
    l_ref[...] = jnp.zeros(l_ref.shape, F32)
    acc_ref[...] = jnp.zeros(acc_ref.shape, F32)

    def update(k, v):
        for g in range(group):
            _flash_update(q_ref[0, :, g * HEAD_DIM_A:(g + 1) * HEAD_DIM_A], k, v,
                          m_ref.at[g], l_ref.at[g], acc_ref.at[g])

    def body(i, carry):
        off = pl.multiple_of(i * tk, tk)
        update(k_ref[0, pl.ds(off, tk), :], v_ref[0, pl.ds(off, tk), :])
        return carry

    lax.fori_loop(0, k_ref.shape[1] // tk, body, 0, unroll=GQA_UNROLL)
    update(ck_ref[0], cv_ref[0])
    o_ref[0] = jnp.concatenate([acc_ref[g] / l_ref[g] for g in range(group)], axis=1).astype(BF16)


def _gqa(qa, ka, va, cka, cva, side, rpb, *, tq, tk):
    b, n, _ = qa.shape
    c = cka.shape[1]
    group = HEADS_A // KV_HEADS_A
    gw = group * HEAD_DIM_A
    nq = n // tq
    n_steps = b * KV_HEADS_A * nq
    assert n_steps == NA_ROWS * NA_PAIRS, "one bias block (class, head pair) per grid step"
    step_of = lambda bb, h, i: (bb * KV_HEADS_A + h) * nq + i
    side_args, side_specs, side_shapes = _side_cast_plan(side, n_steps, step_of)
    rp = _na_bias_padded(rpb)
    bias_spec = pl.BlockSpec((1, 2 * GRID_W, NA_ROWS * GRID_W),
                             lambda *g: (step_of(*g) // NA_PAIRS, step_of(*g) % NA_PAIRS, 0))
    bias_shape = jax.ShapeDtypeStruct((NA_ROWS, HEADS_B * GRID_W, NA_ROWS * GRID_W), F32)
    out = pl.pallas_call(
        functools.partial(_gqa_kernel, tk=tk, n_side=len(side)),
        grid=(b, KV_HEADS_A, nq),
        in_specs=[
            pl.BlockSpec((1, tq, gw), lambda bb, h, i: (bb, i, h)),
            pl.BlockSpec((1, n, HEAD_DIM_A), lambda bb, h, i: (bb, 0, h)),
            pl.BlockSpec((1, n, HEAD_DIM_A), lambda bb, h, i: (bb, 0, h)),
            pl.BlockSpec((1, c, HEAD_DIM_A), lambda bb, h, i: (bb, 0, h)),
            pl.BlockSpec((1, c, HEAD_DIM_A), lambda bb, h, i: (bb, 0, h)),
        ] + side_specs + [pl.BlockSpec(rp.shape, lambda bb, h, i: (0, 0, 0))],
        out_specs=[pl.BlockSpec((1, tq, gw), lambda bb, h, i: (bb, i, h))] + side_specs
        + [bias_spec],
        out_shape=[jax.ShapeDtypeStruct((b, n, QA_COLS), BF16)] + side_shapes + [bias_shape],
        scratch_shapes=[
            pltpu.VMEM((group, tq, HEAD_DIM_A), F32),
            pltpu.VMEM((group, tq, HEAD_DIM_A), F32),
            pltpu.VMEM((group, tq, HEAD_DIM_A), F32),
        ],
        compiler_params=_cparams(("parallel", "parallel", "arbitrary")),
        name="gqa",
    )(qa, ka, va, cka, cva, *side_args, rp)
    return out[0], [o.reshape(w.shape) for o, w in zip(out[1:-1], side)], out[-1]


NA_UNROLL = 64
NA_PAIRS = HEADS_B // 2


def _na_kernel(q_ref, k_ref, v_ref, ck_ref, cv_ref, bias_ref, o_ref):
    rows = q_ref.shape[1] // GRID_W
    win = NA_ROWS * GRID_W
    lane = lax.broadcasted_iota(jnp.int32, (GRID_W, 2 * HEAD_DIM_B), 1)
    lo = lane < HEAD_DIM_B
    pw = 2 * HEAD_DIM_B

    def with_ones(v):
        return jnp.concatenate([v, jnp.ones_like(v)], axis=1)

    def body(r, carry):
        r0 = jnp.clip(r - NA_ROWS // 2, 0, rows - NA_ROWS)
        q = q_ref[0, pl.ds(pl.multiple_of(r * GRID_W, GRID_W), GRID_W), :]
        zero = jnp.zeros_like(q)
        qq = jnp.concatenate([jnp.where(lo, q, zero), jnp.where(lo, zero, q)], axis=0)
        koff = pl.multiple_of(r0 * GRID_W, GRID_W)
        s = _dot_t(qq, k_ref[0, pl.ds(koff, win), :]) + bias_ref[r - r0]
        sc = _dot_t(qq, ck_ref[0])
        m = jnp.maximum(jnp.max(s, axis=-1, keepdims=True), jnp.max(sc, axis=-1, keepdims=True))
        p = jnp.exp2(s - m).astype(BF16)
        pc = jnp.exp2(sc - m).astype(BF16)
        o = _dot(p, with_ones(v_ref[0, pl.ds(koff, win), :])) + _dot(pc, with_ones(cv_ref[0]))
        o = o[:, :pw] / o[:, pw:]
        res = jnp.where(lo, o[:GRID_W], o[GRID_W:])
        o_ref[0, pl.ds(pl.multiple_of(r * GRID_W, GRID_W), GRID_W), :] = res.astype(BF16)
        return carry

    lax.fori_loop(0, rows, body, 0, unroll=NA_UNROLL)


def _na(qb, kb, vb, ckb, cvb, bias):
    b, n, _ = qb.shape
    c = ckb.shape[1]
    pw = 2 * HEAD_DIM_B
    n_cls = bias.shape[0]
    tok = lambda h, bb: (bb, 0, h)
    return pl.pallas_call(
        _na_kernel,
        grid=(NA_PAIRS, b),
        in_specs=[
            pl.BlockSpec((1, n, pw), tok),
            pl.BlockSpec((1, n, pw), tok),
            pl.BlockSpec((1, n, pw), tok),
            pl.BlockSpec((1, c, pw), tok),
            pl.BlockSpec((1, c, pw), tok),
            pl.BlockSpec((n_cls, 2 * GRID_W, NA_ROWS * GRID_W), lambda h, bb: (0, h, 0)),
        ],
        out_specs=pl.BlockSpec((1, n, pw), tok),
        out_shape=jax.ShapeDtypeStruct((b, n, B_COLS), BF16),
        compiler_params=_cparams(("parallel", "parallel")),
        name="natten",
    )(qb, kb, vb, ckb, cvb, bias)


def _na_bias_padded(rpb):
    edge = GRID_W - NA_COLS
    return jnp.pad(rpb, ((0, 0), (0, 1), (edge, 2 * GRID_W - rpb.shape[2] - edge)))


def _na_bias_block(rp_ref, o_ref, cls, head0):
    edge = GRID_W - NA_COLS
    shape = (GRID_W, 2 * GRID_W)
    c = lax.broadcasted_iota(jnp.int32, shape, 0)
    lane = lax.broadcasted_iota(jnp.int32, shape, 1)
    lo = lane < GRID_W
    j = jnp.where(lo, lane, lane - GRID_W)
    cs = jnp.clip(c - NA_COLS // 2, 0, edge)
    valid = (j >= cs) & (j < cs + NA_COLS)

    def toeplitz(h, i, shift):
        row = rp_ref[h, pl.ds(i - cls + NA_ROWS - 1, 1), :]
        return pltpu.roll(jnp.broadcast_to(row, shape), shift, 1, stride=1, stride_axis=0)

    for k in range(2):
        for pair in range(NA_ROWS // 2):
            t = jnp.where(lo, toeplitz(head0 + k, 2 * pair, GRID_W + 1),
                          toeplitz(head0 + k, 2 * pair + 1, 1))
            o_ref[0, k * GRID_W:(k + 1) * GRID_W, pair * 2 * GRID_W:(pair + 1) * 2 * GRID_W] = (
                jnp.where(valid, t * LOG2E, NEG))


HALO = 8
MIX_HALO = 16
FFN_CHUNK = 256
FFN_TM = 512
FFN_SLOTS = 4
FFN_ROWS = 32


def _ffn_kernel(x_ref, xp_ref, xn_ref, sh_ref, sc_ref, g2_ref, ng_ref, wup_ref, cw_ref, cb_ref,
                wdn_ref, fg_ref, *rest, tm, d_ff, final_norm, mixer):
    if mixer:
        (oa_ref, oap_ref, oan_ref, ob_ref, obp_ref, obn_ref, wo_ref, g1_ref,
         o_ref, u_ref, a_ref, y_ref) = rest
        ka = oa_ref.shape[2]
        rows = lambda main, prev, nxt: jnp.concatenate([main[0], prev[0], nxt[0]], axis=0)
        mix = (_dot(rows(oa_ref, oap_ref, oan_ref), wo_ref[0:ka, :])
               + _dot(rows(ob_ref, obp_ref, obn_ref), wo_ref[ka:, :]))
        g1 = g1_ref[0]
        lo = tm + MIX_HALO - HALO
        x_tile = x_ref[0] + g1 * mix[:tm]
        x_prev = xp_ref[0] + g1 * mix[lo:lo + HALO]
        x_next = xn_ref[0] + g1 * mix[tm + MIX_HALO:tm + MIX_HALO + HALO]
        o_ref[0] = x_tile
    else:
        o_ref, u_ref, a_ref, y_ref = rest
        x_tile, x_prev, x_next = x_ref[0], xp_ref[0], xn_ref[0]
    i = pl.program_id(1)
    ng, sh, sc = ng_ref[...], sh_ref[0], sc_ref[0]
    d = x_ref.shape[2]
    half = tm // 2
    slabs = FFN_CHUNK // LANES

    hp = jnp.where(i > 0, _norm_mod(x_prev, ng, sh, sc), 0.0)
    hn = jnp.where(i < pl.num_programs(1) - 1, _norm_mod(x_next, ng, sh, sc), 0.0)
    h = jnp.concatenate([hp, _norm_mod(x_tile, ng, sh, sc), hn], axis=0).astype(BF16)

    def project(buf, part, col):
        u = _dot(h, wup_ref[:, col:col + FFN_CHUNK])
        for s in range(slabs):
            u_ref[buf, part, s] = u[:, s * LANES:(s + 1) * LANES]

    def conv_gate(buf, c0):
        for s in range(slabs):
            cols = (c0 + s * LANES, d_ff + c0 + s * LANES)
            taps_w = [[cw_ref[k:k + 1, c:c + LANES] for k in range(3)] for c in cols]
            bias = [cb_ref[:, c:c + LANES] for c in cols]
            for rb in range(half // FFN_ROWS):
                res = []
                for part in range(2):
                    t = [u_ref[buf, part, s, pl.ds(HALO - 1 + k + 2 * rb * FFN_ROWS, FFN_ROWS, stride=2), :]
                         for k in range(4)]
                    w0, w1, w2 = taps_w[part]
                    res.append((t[0] * w0 + t[1] * w1 + t[2] * w2 + bias[part],
                                t[1] * w0 + t[2] * w1 + t[3] * w2 + bias[part]))
                (g_even, g_odd), (v_even, v_odd) = res
                r0 = rb * FFN_ROWS
                col = slice(c0 + s * LANES, c0 + (s + 1) * LANES)
                a_ref[r0:r0 + FFN_ROWS, col] = (g_even * jax.nn.sigmoid(g_even) * v_even).astype(BF16)
                a_ref[half + r0:half + r0 + FFN_ROWS, col] = (
                    g_odd * jax.nn.sigmoid(g_odd) * v_odd).astype(BF16)

    n_chunks = d_ff // FFN_CHUNK
    n_buf = u_ref.shape[0]

    def project_chunk(f):
        project(f % n_buf, 0, f * FFN_CHUNK)
        project(f % n_buf, 1, d_ff + f * FFN_CHUNK)

    for f in range(min(n_buf - 1, n_chunks)):
        project_chunk(f)
    for f in range(n_chunks):
        c0 = f * FFN_CHUNK
        if f + n_buf - 1 < n_chunks:
            project_chunk(f + n_buf - 1)
        conv_gate(f % n_buf, c0)
    acc = _dot(a_ref[...], wdn_ref[...])
    for s in range(d // LANES):
        y_ref[s, pl.ds(0, half, stride=2), :] = acc[:half, s * LANES:(s + 1) * LANES]
        y_ref[s, pl.ds(1, half, stride=2), :] = acc[half:, s * LANES:(s + 1) * LANES]
    ffn = jnp.concatenate([y_ref[s] for s in range(d // LANES)], axis=1)
    y = (o_ref[0] if mixer else x_ref[0]) + g2_ref[0] * ffn
    if final_norm:
        y = _rms(y, fg_ref[...])
    o_ref[0] = y


def _halo_specs(width, tm, n, rows):
    hb = tm // rows
    last = n // rows - 1
    return [
        pl.BlockSpec((1, tm, width), lambda bb, i: (bb, i, 0)),
        pl.BlockSpec((1, rows, width), lambda bb, i: (bb, jnp.maximum(i * hb - 1, 0), 0)),
        pl.BlockSpec((1, rows, width), lambda bb, i: (bb, jnp.minimum((i + 1) * hb, last), 0)),
    ]


def _ffn(x, shift, scale, gate, norm_g, w_up, conv_w, conv_b, w_down, final_g, *, layer, tm,
         final_norm, mixer=None):
    b, n, d = x.shape
    d_ff = w_down.shape[1]
    vec = lambda bb, i: (bb, 0, 0)
    const2 = lambda bb, i: (0, 0)
    this_layer = lambda bb, i: (layer, 0, 0)
    resident = dict(pipeline_mode=pl.Buffered(1))
    mix_specs, mix_args = [], []
    if mixer is not None:
        oa, ob, w_out, g1 = mixer
        for o in (oa, ob):
            mix_specs += _halo_specs(o.shape[2], tm, n, MIX_HALO)
            mix_args += [o, o, o]
        mix_specs += [pl.BlockSpec(w_out.shape, const2, **resident), pl.BlockSpec((1, 1, d), vec)]
        mix_args += [w_out, g1]
    return pl.pallas_call(
        functools.partial(_ffn_kernel, tm=tm, d_ff=d_ff, final_norm=final_norm,
                          mixer=mixer is not None),
        grid=(b, n // tm),
        in_specs=_halo_specs(d, tm, n, HALO) + [
            pl.BlockSpec((1, 1, d), vec),
            pl.BlockSpec((1, 1, d), vec),
            pl.BlockSpec((1, 1, d), vec),
            pl.BlockSpec((1, d), const2),
            pl.BlockSpec((None,) + w_up.shape[1:], this_layer, **resident),
            pl.BlockSpec(conv_w.shape, const2),
            pl.BlockSpec((1, 2 * d_ff), const2),
            pl.BlockSpec((None,) + w_down.shape[1:], this_layer, **resident),
            pl.BlockSpec((1, d), const2),
        ] + mix_specs,
        out_specs=pl.BlockSpec((1, tm, d), lambda bb, i: (bb, i, 0)),
        out_shape=jax.ShapeDtypeStruct(x.shape, F32),
        scratch_shapes=[
            pltpu.VMEM((FFN_SLOTS, 2, FFN_CHUNK // LANES, tm + 2 * HALO, LANES), F32),
            pltpu.VMEM((tm, d_ff), BF16),
            pltpu.VMEM((d // LANES, tm, LANES), F32),
        ],
        compiler_params=_cparams(("parallel", "parallel")),
        name="conv_ffn",
    )(x, x, x, shift, scale, gate, norm_g, w_up, conv_w, conv_b.reshape(1, -1), w_down, final_g,
      *mix_args)


F1_NC = 8
F2_NK = 16


def _dft_tables(n_tok, cg):
    w = GRID_W
    idx = np.arange(w, dtype=np.float64)
    a64 = 2.0 * np.pi * np.outer(idx, idx) / w
    fc, fs = np.cos(a64), np.sin(a64)
    a1 = 2.0 * np.pi * idx[None, :, None] * (w * idx[None, None, :] + idx[:, None, None]) / n_tok
    step1 = np.block([[np.cos(a1), np.sin(a1)], [-np.sin(a1), np.cos(a1)]])
    eye = np.eye(F2_NK)
    step3 = (np.kron(fc[:w // 2 + 1], eye), np.kron(fs[:w // 2 + 1], eye))
    ch = np.arange(cg, dtype=np.float64)
    ach = 2.0 * np.pi * np.outer(ch, ch) / cg
    chan = np.concatenate([np.cos(ach), -np.sin(ach)], axis=1)
    f = lambda t: jnp.asarray(t, F32)
    return f(chan), f(step1), f(step3[0]), f(step3[1])


def _f1_kernel(x_ref, sh_ref, sc_ref, g_ref, chan_ref, s1_ref, ar_ref, ai_ref, u_ref, *, nc, d):
    w = GRID_W
    cg = d // FOURIER_GROUPS
    tiles = d // LANES
    gt = cg // LANES
    x = x_ref[0].reshape(w * nc, d)
    h = _norm_mod(x, g_ref[...], sh_ref[0], sc_ref[0]).astype(BF16)
    for g in range(FOURIER_GROUPS):
        u = _dot(h[:, g * cg:(g + 1) * cg], chan_ref[...])
        for t in range(2 * gt):
            part, tt = divmod(t, gt)
            u_ref[part * tiles + g * gt + tt] = u[:, t * LANES:(t + 1) * LANES]
    for c in range(nc):
        rows = pl.ds(c, w, stride=nc)
        ur = jnp.concatenate([u_ref[s, rows, :] for s in range(tiles)], axis=1)
        ui = jnp.concatenate([u_ref[tiles + s, rows, :] for s in range(tiles)], axis=1)
        a = _dot(s1_ref[c], jnp.concatenate([ur, ui], axis=0).astype(BF16))
        ar_ref[0, c] = a[:w].astype(BF16)
        ai_ref[0, c] = a[w:].astype(BF16)


def _f2_kernel(ar_ref, ai_ref, x_ref, g1_ref, kc_ref, ks_ref, w_ref, o_ref, *, nk, d, inv_norm):
    w = GRID_W
    a_r = ar_ref[0].reshape(w * nk, d)
    a_i = ai_ref[0].reshape(w * nk, d)
    p = _dot(kc_ref[...], a_r)
    q = _dot(ks_ref[...], a_i)
    diff = p - q
    mirrored = [diff[(w - k2) * nk:(w - k2 + 1) * nk] for k2 in range(w // 2 + 1, w)]
    z = jnp.concatenate([p + q] + mirrored, axis=0)
    y = _dot((z * inv_norm).astype(BF16), w_ref[...])
    o_ref[0] = x_ref[0] + g1_ref[0] * y.reshape(w, nk, d)


def _fourier(x, shift, scale, gate, norm_g, w_out):
    b, n, d = x.shape
    w = GRID_W
    assert n == w * w
    nc, nk = F1_NC, F2_NK
    cg = d // FOURIER_GROUPS
    chan, step1, kc, ks = _dft_tables(n, cg)
    chan, step1, kc, ks = (t.astype(BF16) for t in (chan, step1, kc, ks))
    x4 = x.reshape(b, w, w, d)
    vec = lambda bb, j: (bb, 0, 0)
    const2 = lambda bb, j: (0, 0)
    resident = dict(pipeline_mode=pl.Buffered(1))
    a_r, a_i = pl.pallas_call(
        functools.partial(_f1_kernel, nc=nc, d=d),
        grid=(b, w // nc),
        in_specs=[
            pl.BlockSpec((1, w, nc, d), lambda bb, j: (bb, 0, j, 0)),
            pl.BlockSpec((1, 1, d), vec),
            pl.BlockSpec((1, 1, d), vec),
            pl.BlockSpec((1, d), const2),
            pl.BlockSpec(chan.shape, const2),
            pl.BlockSpec((nc,) + step1.shape[1:], lambda bb, j: (j, 0, 0)),
        ],
        out_specs=[pl.BlockSpec((1, nc, w, d), lambda bb, j: (bb, j, 0, 0))] * 2,
        out_shape=[jax.ShapeDtypeStruct((b, w, w, d), BF16)] * 2,
        scratch_shapes=[pltpu.VMEM((2 * d // LANES, w * nc, LANES), F32)],
        compiler_params=_cparams(("parallel", "parallel")),
        name="fourier_rows",
    )(x4, shift, scale, norm_g, chan, step1)
    out = pl.pallas_call(
        functools.partial(_f2_kernel, nk=nk, d=d, inv_norm=1.0 / float(np.sqrt(n * cg))),
        grid=(b, w // nk),
        in_specs=[
            pl.BlockSpec((1, w, nk, d), lambda bb, j: (bb, 0, j, 0)),
            pl.BlockSpec((1, w, nk, d), lambda bb, j: (bb, 0, j, 0)),
            pl.BlockSpec((1, w, nk, d), lambda bb, j: (bb, 0, j, 0)),
            pl.BlockSpec((1, 1, d), vec),
            pl.BlockSpec(kc.shape, const2, **resident),
            pl.BlockSpec(ks.shape, const2, **resident),
            pl.BlockSpec(w_out.shape, const2, **resident),
        ],
        out_specs=pl.BlockSpec((1, w, nk, d), lambda bb, j: (bb, 0, j, 0)),
        out_shape=jax.ShapeDtypeStruct((b, w, w, d), F32),
        compiler_params=_cparams(("parallel", "parallel")),
        name="fourier_cols",
    )(a_r, a_i, x4, gate, kc, ks, w_out)
    return out.reshape(b, n, d)


def _rope_tables(n):
    t = np.arange(n)
    row = (t // GRID_W).astype(np.float64)
    col = (t % GRID_W).astype(np.float64)
    pairs = HEAD_DIM_A // 4
    freqs = ROPE_THETA ** (-np.arange(pairs, dtype=np.float64) / pairs)
    ang = np.concatenate([row[:, None] * freqs, col[:, None] * freqs], axis=-1)
    cos, sin = np.cos(ang), np.sin(ang)
    return (jnp.asarray(np.concatenate([cos, cos], axis=-1), F32),
            jnp.asarray(np.concatenate([-sin, sin], axis=-1), F32))


def kernel(x, c, ctx, c_ctx, mod_w, mod_b, norm1_g, norm2_g, attn_w_in, attn_w_out, q_norm_g,
           k_norm_g, na_rpb, fourier_w_out, ffn_w_up, ffn_conv_w, ffn_conv_b, ffn_w_down, final_g):
    b, n, d = x.shape
    depth = mod_w.shape[0]
    assert depth == 2 and b + 1 <= 8

    vec8 = jnp.zeros((8, d), F32).at[:b].set(c).at[b].set(c_ctx)
    mod = _adaln(vec8, mod_w, mod_b)
    term = lambda l, k: mod[l, :b, k * d:(k + 1) * d].reshape(b, 1, d)
    cterm = lambda l, k: jnp.broadcast_to(mod[l, b, k * d:(k + 1) * d].reshape(1, 1, d), (b, 1, d))

    perm = np.concatenate([np.arange(0, HEAD_DIM_A, 2), np.arange(1, HEAD_DIM_A, 2)])
    w_in = _prep_w_in(attn_w_in, perm)
    qg = q_norm_g[0][perm].reshape(1, -1)
    kg = k_norm_g[0][perm].reshape(1, -1)
    n_ctx = ctx.shape[1]
    no_rope = (jnp.ones((n_ctx, HEAD_DIM_A), F32), jnp.zeros((n_ctx, HEAD_DIM_A), F32))
    g1row = norm1_g[0].reshape(1, d)

    qa, qb, ka, va, kb, vb = _proj(x, term(0, 0), term(0, 1), g1row, w_in, _rope_tables(n), qg, kg,
                                   with_q=True, tm=PROJ_TM, name="proj_latent")
    cka, cva, ckb, cvb = _proj(ctx, cterm(0, 0), cterm(0, 1), g1row, w_in, no_rope, qg, kg,
                               with_q=False, tm=n_ctx, name="proj_context")
    oa, (w_up, w_down, w_attn_out, w_fourier_out), na_bias = _gqa(
        qa, ka, va, cka, cva, (ffn_w_up, ffn_w_down, attn_w_out, fourier_w_out), na_rpb[0],
        tq=1024, tk=1024)
    ob = _na(qb, kb, vb, ckb, cvb, na_bias)
    x = _ffn(x, term(0, 3), term(0, 4), term(0, 5), norm2_g[0].reshape(1, d), w_up,
             ffn_conv_w[0], ffn_conv_b[0], w_down, final_g.reshape(1, d),
             layer=0, tm=FFN_TM, final_norm=False,
             mixer=(oa, ob, w_attn_out[0], term(0, 2)))

    x = _fourier(x, term(1, 0), term(1, 1), term(1, 2), norm1_g[1].reshape(1, d),
                 w_fourier_out[0])
    x = _ffn(x, term(1, 3), term(1, 4), term(1, 5), norm2_g[1].reshape(1, d), w_up,
             ffn_conv_w[1], ffn_conv_b[1], w_down, final_g.reshape(1, d),
             layer=1, tm=FFN_TM, final_norm=True)
    return x
```

```python
import functools

import numpy as np
import jax
import jax.numpy as jnp
from jax import lax
from jax.experimental import pallas as pl
from jax.experimental.pallas import tpu as pltpu

F32 = jnp.float32
BF16 = jnp.bfloat16

GRID_W = 64
HEAD_DIM_A = 128
HEADS_A = 4
KV_HEADS_A = 2
HEAD_DIM_B = 64
HEADS_B = 8
NA_ROWS = 8
NA_COLS = 16
FOURIER_GROUPS = 4
ROPE_THETA = 10000.0
EPS = 1e-6
QA_COLS = HEADS_A * HEAD_DIM_A
KA_COLS = KV_HEADS_A * HEAD_DIM_A
B_COLS = HEADS_B * HEAD_DIM_B
NEG = -1e30
LOG2E = 1.4426950408889634
LANES = 128

VMEM_LIMIT = 56 * 1024 * 1024


def _cparams(sem):
    return pltpu.CompilerParams(dimension_semantics=sem, vmem_limit_bytes=VMEM_LIMIT)


def _rms(x, g):
    ms = jnp.mean(x * x, axis=-1, keepdims=True)
    return x * lax.rsqrt(ms + EPS) * g


def _norm_mod(x, g, shift, scale):
    return _rms(x, g) * (1.0 + scale) + shift


def _dot(a, b):
    return jnp.dot(a, b, preferred_element_type=F32)


def _dot_t(a, b):
    return lax.dot_general(a, b, (((1,), (1,)), ((), ())), preferred_element_type=F32)


def _side_cast_plan(weights, n_steps, step_of):
    args = [w.reshape(-1, w.shape[-1]) for w in weights]
    specs = [pl.BlockSpec((a.shape[0] // n_steps, a.shape[1]), lambda *g: (step_of(*g), 0))
             for a in args]
    shapes = [jax.ShapeDtypeStruct(a.shape, BF16) for a in args]
    return args, specs, shapes


def _side_cast(in_refs, out_refs):
    for src, dst in zip(in_refs, out_refs):
        dst[...] = src[...].astype(dst.dtype)


W_IN_HEADS = 4


def _w_in_kernel(w_ref, p_ref, o_ref):
    q_blocks = (QA_COLS + B_COLS) // HEAD_DIM_A
    for k in range(W_IN_HEADS):
        head = pl.program_id(0) * W_IN_HEADS + k
        rotary = (head < HEADS_A) | ((head >= q_blocks) & (head < q_blocks + KV_HEADS_A))
        cols = slice(k * HEAD_DIM_A, (k + 1) * HEAD_DIM_A)
        w = w_ref[:, cols].astype(BF16)

        @pl.when(rotary)
        def _():
            o_ref[:, cols] = _dot(w, p_ref[...]).astype(BF16)

        @pl.when(jnp.logical_not(rotary))
        def _():
            o_ref[:, cols] = w


def _prep_w_in(attn_w_in, perm):
    _, d, ncol = attn_w_in.shape
    p = np.zeros((HEAD_DIM_A, HEAD_DIM_A), np.float32)
    p[perm, np.arange(HEAD_DIM_A)] = 1.0
    return pl.pallas_call(
        _w_in_kernel,
        grid=(ncol // (W_IN_HEADS * HEAD_DIM_A),),
        in_specs=[
            pl.BlockSpec((None, d, W_IN_HEADS * HEAD_DIM_A), lambda j: (0, 0, j)),
            pl.BlockSpec((HEAD_DIM_A, HEAD_DIM_A), lambda j: (0, 0)),
        ],
        out_specs=pl.BlockSpec((d, W_IN_HEADS * HEAD_DIM_A), lambda j: (0, j)),
        out_shape=jax.ShapeDtypeStruct((d, ncol), BF16),
        compiler_params=_cparams(("parallel",)),
        name="prep_w_in",
    )(attn_w_in, jnp.asarray(p).astype(BF16))


def _adaln_kernel(v_ref, w_ref, b_ref, o_ref):
    v = v_ref[...]
    s = v * jax.nn.sigmoid(v)
    w = w_ref[0]
    s_hi = s.astype(BF16)
    s_lo = (s - s_hi.astype(F32)).astype(BF16)
    w_hi = w.astype(BF16)
    w_lo = (w - w_hi.astype(F32)).astype(BF16)
    rows = s.shape[0]
    t = _dot(jnp.concatenate([s_hi, s_lo], axis=0), w_hi)
    o_ref[0] = t[:rows] + t[rows:] + _dot(s_hi, w_lo) + b_ref[0]


def _adaln(vec8, mod_w, mod_b):
    depth, d, n = mod_w.shape
    tn = 3072
    return pl.pallas_call(
        _adaln_kernel,
        grid=(depth, n // tn),
        in_specs=[
            pl.BlockSpec((8, d), lambda l, j: (0, 0)),
            pl.BlockSpec((1, d, tn), lambda l, j: (l, 0, j)),
            pl.BlockSpec((1, 1, tn), lambda l, j: (l, 0, j)),
        ],
        out_specs=pl.BlockSpec((1, 8, tn), lambda l, j: (l, 0, j)),
        out_shape=jax.ShapeDtypeStruct((depth, 8, n), F32),
        compiler_params=_cparams(("arbitrary", "arbitrary")),
        name="adaln",
    )(vec8, mod_w, mod_b.reshape(depth, 1, n))


PROJ_TM = 512


def _proj_kernel(x_ref, sh_ref, sc_ref, g_ref, w_ref, cos_ref, sin_ref, qg_ref, kg_ref,
                 cx_ref, csh_ref, csc_ref, ccos_ref, csin_ref, *out_refs):
    n_lat = 6
    _project_rows(x_ref, sh_ref, sc_ref, g_ref, w_ref, cos_ref, sin_ref, qg_ref, kg_ref,
                  out_refs[:n_lat], with_q=True)

    @pl.when(pl.program_id(1) == 0)
    def _():
        _project_rows(cx_ref, csh_ref, csc_ref, g_ref, w_ref, ccos_ref, csin_ref, qg_ref, kg_ref,
                      out_refs[n_lat:], with_q=False)


def _project_rows(x_ref, sh_ref, sc_ref, g_ref, w_ref, cos_ref, sin_ref, qg_ref, kg_ref,
                  out_refs, *, with_q):
    h = _norm_mod(x_ref[0], g_ref[...], sh_ref[0], sc_ref[0]).astype(BF16)
    cosf = cos_ref[...]
    sinf = sin_ref[...]

    def rope_heads(p, n_heads, gain, scale, o_ref):
        for hd in range(n_heads):
            t = _rms(p[:, hd * HEAD_DIM_A:(hd + 1) * HEAD_DIM_A], gain)
            t = t * cosf + pltpu.roll(t, HEAD_DIM_A // 2, axis=1) * sinf
            o_ref[0, :, hd * HEAD_DIM_A:(hd + 1) * HEAD_DIM_A] = (t * scale).astype(BF16)

    col = QA_COLS + B_COLS
    if with_q:
        qa_ref, qb_ref, ka_ref, va_ref, kb_ref, vb_ref = out_refs
        rope_heads(_dot(h, w_ref[:, 0:QA_COLS]), HEADS_A, qg_ref[...], HEAD_DIM_A ** -0.5 * LOG2E,
                   qa_ref)
    else:
        ka_ref, va_ref, kb_ref, vb_ref = out_refs
    rope_heads(_dot(h, w_ref[:, col:col + KA_COLS]), KV_HEADS_A, kg_ref[...], 1.0, ka_ref)
    if with_q:
        qb = _dot(h, w_ref[:, QA_COLS:QA_COLS + B_COLS])
        qb_ref[0] = (qb * (HEAD_DIM_B ** -0.5 * LOG2E)).astype(BF16)
    col += KA_COLS
    va_ref[0] = _dot(h, w_ref[:, col:col + KA_COLS]).astype(BF16)
    col += KA_COLS
    kb_ref[0] = _dot(h, w_ref[:, col:col + B_COLS]).astype(BF16)
    col += B_COLS
    vb_ref[0] = _dot(h, w_ref[:, col:col + B_COLS]).astype(BF16)


def _proj(x, shift, scale, ctx, cshift, cscale, gain, w, rope, ctx_rope, qg, kg, *, tm):
    b, n, d = x.shape
    nc = ctx.shape[1]
    ncol = w.shape[1]
    kv_widths = [KA_COLS, KA_COLS, B_COLS, B_COLS]
    widths = [QA_COLS, B_COLS] + kv_widths
    vec = lambda bb, i: (bb, 0, 0)
    const2 = lambda bb, i: (0, 0)
    out = pl.pallas_call(
        _proj_kernel,
        grid=(b, n // tm),
        in_specs=[
            pl.BlockSpec((1, tm, d), lambda bb, i: (bb, i, 0)),
            pl.BlockSpec((1, 1, d), vec),
            pl.BlockSpec((1, 1, d), vec),
            pl.BlockSpec((1, d), const2),
            pl.BlockSpec((d, ncol), const2),
            pl.BlockSpec((tm, HEAD_DIM_A), lambda bb, i: (i, 0)),
            pl.BlockSpec((tm, HEAD_DIM_A), lambda bb, i: (i, 0)),
            pl.BlockSpec((1, HEAD_DIM_A), const2),
            pl.BlockSpec((1, HEAD_DIM_A), const2),
            pl.BlockSpec((1, nc, d), vec),
            pl.BlockSpec((1, 1, d), vec),
            pl.BlockSpec((1, 1, d), vec),
            pl.BlockSpec((nc, HEAD_DIM_A), const2),
            pl.BlockSpec((nc, HEAD_DIM_A), const2),
        ],
        out_specs=[pl.BlockSpec((1, tm, wd), lambda bb, i: (bb, i, 0)) for wd in widths]
        + [pl.BlockSpec((1, nc, wd), vec) for wd in kv_widths],
        out_shape=[jax.ShapeDtypeStruct((b, n, wd), BF16) for wd in widths]
        + [jax.ShapeDtypeStruct((b, nc, wd), BF16) for wd in kv_widths],
        compiler_params=_cparams(("parallel", "arbitrary")),
        name="proj",
    )(x, shift, scale, gain, w, *rope, qg, kg, ctx, cshift, cscale, *ctx_rope)
    return out[:len(widths)], out[len(widths):]


GQA_UNROLL = 8


def _flash_update(q, k, v, m_ref, l_ref, acc_ref):
    d = v.shape[1]
    s = _dot_t(q, k)
    m_old = m_ref[...]
    m_new = jnp.maximum(m_old, jnp.max(s, axis=-1, keepdims=True))
    alpha = jnp.exp2(m_old - m_new)
    p = jnp.exp2(s - jnp.tile(m_new, (1, s.shape[1] // m_new.shape[1])))
    o = _dot(p.astype(BF16), jnp.concatenate([v, jnp.ones_like(v)], axis=1))
    acc_ref[...] = alpha * acc_ref[...] + o[:, :d]
    l_ref[...] = alpha * l_ref[...] + o[:, d:]
    m_ref[...] = m_new


def _gqa_kernel(q_ref, k_ref, v_ref, ck_ref, cv_ref, *rest, tk, n_side):
    side_in, rp_ref = rest[:n_side], rest[n_side]
    o_ref, side_out, bias_ref = rest[n_side + 1], rest[n_side + 2:2 * n_side + 2], rest[2 * n_side + 2]
    m_ref, l_ref, acc_ref = rest[2 * n_side + 3:]
    _side_cast(side_in, side_out)
    step = (pl.program_id(0) * pl.num_programs(1) + pl.program_id(1)) * pl.num_programs(2) \
        + pl.program_id(2)
    _na_bias_block(rp_ref, bias_ref, step // NA_PAIRS, 2 * (step % NA_PAIRS))
    group = q_ref.shape[2] // HEAD_DIM_A
    m_ref[...] = jnp.full(m_ref.shape, NEG, F32)
    l_ref[...] = jnp.zeros(l_ref.shape, F32)
    acc_ref[...] = jnp.zeros(acc_ref.shape, F32)

    def update(k, v):
        for g in range(group):
            _flash_update(q_ref[0, :, g * HEAD_DIM_A:(g + 1) * HEAD_DIM_A], k, v,
                          m_ref.at[g], l_ref.at[g], acc_ref.at[g])

    def body(i, carry):
        off = pl.multiple_of(i * tk, tk)
        update(k_ref[0, pl.ds(off, tk), :], v_ref[0, pl.ds(off, tk), :])
        return carry

    lax.fori_loop(0, k_ref.shape[1] // tk, body, 0, unroll=GQA_UNROLL)
    update(ck_ref[0], cv_ref[0])
    o_ref[0] = jnp.concatenate([acc_ref[g] / l_ref[g] for g in range(group)], axis=1).astype(BF16)


def _gqa(qa, ka, va, cka, cva, side, rpb, *, tq, tk):
    b, n, _ = qa.shape
    c = cka.shape[1]
    group = HEADS_A // KV_HEADS_A
    gw = group * HEAD_DIM_A
    nq = n // tq
    n_steps = b * KV_HEADS_A * nq
    assert n_steps == NA_ROWS * NA_PAIRS, "one bias block (class, head pair) per grid step"
    step_of = lambda bb, h, i: (bb * KV_HEADS_A + h) * nq + i
    side_args, side_specs, side_shapes = _side_cast_plan(side, n_steps, step_of)
    rp = _na_bias_padded(rpb)
    bias_spec = pl.BlockSpec((1, 2 * GRID_W, NA_ROWS * GRID_W),
                             lambda *g: (step_of(*g) // NA_PAIRS, step_of(*g) % NA_PAIRS, 0))
    bias_shape = jax.ShapeDtypeStruct((NA_ROWS, HEADS_B * GRID_W, NA_ROWS * GRID_W), F32)
    out = pl.pallas_call(
        functools.partial(_gqa_kernel, tk=tk, n_side=len(side)),
        grid=(b, KV_HEADS_A, nq),
        in_specs=[
            pl.BlockSpec((1, tq, gw), lambda bb, h, i: (bb, i, h)),
            pl.BlockSpec((1, n, HEAD_DIM_A), lambda bb, h, i: (bb, 0, h)),
            pl.BlockSpec((1, n, HEAD_DIM_A), lambda bb, h, i: (bb, 0, h)),
            pl.BlockSpec((1, c, HEAD_DIM_A), lambda bb, h, i: (bb, 0, h)),
            pl.BlockSpec((1, c, HEAD_DIM_A), lambda bb, h, i: (bb, 0, h)),
        ] + side_specs + [pl.BlockSpec(rp.shape, lambda bb, h, i: (0, 0, 0))],
        out_specs=[pl.BlockSpec((1, tq, gw), lambda bb, h, i: (bb, i, h))] + side_specs
        + [bias_spec],
        out_shape=[jax.ShapeDtypeStruct((b, n, QA_COLS), BF16)] + side_shapes + [bias_shape],
        scratch_shapes=[
            pltpu.VMEM((group, tq, HEAD_DIM_A), F32),
            pltpu.VMEM((group, tq, HEAD_DIM_A), F32),
            pltpu.VMEM((group, tq, HEAD_DIM_A), F32),
        ],
        compiler_params=_cparams(("parallel", "parallel", "arbitrary")),
        name="gqa",
    )(qa, ka, va, cka, cva, *side_args, rp)
    return out[0], [o.reshape(w.shape) for o, w in zip(out[1:-1], side)], out[-1]


NA_UNROLL = 64
NA_PAIRS = HEADS_B // 2


def _na_kernel(q_ref, k_ref, v_ref, ck_ref, cv_ref, bias_ref, o_ref):
    rows = q_ref.shape[1] // GRID_W
    win = NA_ROWS * GRID_W
    lane = lax.broadcasted_iota(jnp.int32, (GRID_W, 2 * HEAD_DIM_B), 1)
    lo = lane < HEAD_DIM_B
    pw = 2 * HEAD_DIM_B

    def with_ones(v):
        return jnp.concatenate([v, jnp.ones_like(v)], axis=1)

    def body(r, carry):
        r0 = jnp.clip(r - NA_ROWS // 2, 0, rows - NA_ROWS)
        q = q_ref[0, pl.ds(pl.multiple_of(r * GRID_W, GRID_W), GRID_W), :]
        zero = jnp.zeros_like(q)
        qq = jnp.concatenate([jnp.where(lo, q, zero), jnp.where(lo, zero, q)], axis=0)
        koff = pl.multiple_of(r0 * GRID_W, GRID_W)
        s = _dot_t(qq, k_ref[0, pl.ds(koff, win), :]) + bias_ref[r - r0]
        sc = _dot_t(qq, ck_ref[0])
        m = jnp.maximum(jnp.max(s, axis=-1, keepdims=True), jnp.max(sc, axis=-1, keepdims=True))
        p = jnp.exp2(s - m).astype(BF16)
        pc = jnp.exp2(sc - m).astype(BF16)
        o = _dot(p, with_ones(v_ref[0, pl.ds(koff, win), :])) + _dot(pc, with_ones(cv_ref[0]))
        o = o[:, :pw] / o[:, pw:]
        res = jnp.where(lo, o[:GRID_W], o[GRID_W:])
        o_ref[0, pl.ds(pl.multiple_of(r * GRID_W, GRID_W), GRID_W), :] = res.astype(BF16)
        return carry

    lax.fori_loop(0, rows, body, 0, unroll=NA_UNROLL)


def _na(qb, kb, vb, ckb, cvb, bias):
    b, n, _ = qb.shape
    c = ckb.shape[1]
    pw = 2 * HEAD_DIM_B
    n_cls = bias.shape[0]
    tok = lambda h, bb: (bb, 0, h)
    return pl.pallas_call(
        _na_kernel,
        grid=(NA_PAIRS, b),
        in_specs=[
            pl.BlockSpec((1, n, pw), tok),
            pl.BlockSpec((1, n, pw), tok),
            pl.BlockSpec((1, n, pw), tok),
            pl.BlockSpec((1, c, pw), tok),
            pl.BlockSpec((1, c, pw), tok),
            pl.BlockSpec((n_cls, 2 * GRID_W, NA_ROWS * GRID_W), lambda h, bb: (0, h, 0)),
        ],
        out_specs=pl.BlockSpec((1, n, pw), tok),
        out_shape=jax.ShapeDtypeStruct((b, n, B_COLS), BF16),
        compiler_params=_cparams(("parallel", "parallel")),
        name="natten",
    )(qb, kb, vb, ckb, cvb, bias)


def _na_bias_padded(rpb):
    edge = GRID_W - NA_COLS
    return jnp.pad(rpb, ((0, 0), (0, 1), (edge, 2 * GRID_W - rpb.shape[2] - edge)))


def _na_bias_block(rp_ref, o_ref, cls, head0):
    edge = GRID_W - NA_COLS
    shape = (GRID_W, 2 * GRID_W)
    c = lax.broadcasted_iota(jnp.int32, shape, 0)
    lane = lax.broadcasted_iota(jnp.int32, shape, 1)
    lo = lane < GRID_W
    j = jnp.where(lo, lane, lane - GRID_W)
    cs = jnp.clip(c - NA_COLS // 2, 0, edge)
    valid = (j >= cs) & (j < cs + NA_COLS)

    def toeplitz(h, i, shift):
        row = rp_ref[h, pl.ds(i - cls + NA_ROWS - 1, 1), :]
        return pltpu.roll(jnp.broadcast_to(row, shape), shift, 1, stride=1, stride_axis=0)

    for k in range(2):
        for pair in range(NA_ROWS // 2):
            t = jnp.where(lo, toeplitz(head0 + k, 2 * pair, GRID_W + 1),
                          toeplitz(head0 + k, 2 * pair + 1, 1))
            o_ref[0, k * GRID_W:(k + 1) * GRID_W, pair * 2 * GRID_W:(pair + 1) * 2 * GRID_W] = (
                jnp.where(valid, t * LOG2E, NEG))


HALO = 8
MIX_HALO = 16
FFN_CHUNK = 256
FFN_TM = 512
FFN_SLOTS = 4
FFN_ROWS = 32


def _ffn_kernel(x_ref, xp_ref, xn_ref, sh_ref, sc_ref, g2_ref, ng_ref, wup_ref, cw_ref, cb_ref,
                wdn_ref, fg_ref, *rest, tm, d_ff, final_norm, mixer):
    if mixer:
        (oa_ref, oap_ref, oan_ref, ob_ref, obp_ref, obn_ref, wo_ref, g1_ref,
         o_ref, u_ref, a_ref, y_ref) = rest
        ka = oa_ref.shape[2]
        rows = lambda main, prev, nxt: jnp.concatenate([main[0], prev[0], nxt[0]], axis=0)
        mix = (_dot(rows(oa_ref, oap_ref, oan_ref), wo_ref[0:ka, :])
               + _dot(rows(ob_ref, obp_ref, obn_ref), wo_ref[ka:, :]))
        g1 = g1_ref[0]
        lo = tm + MIX_HALO - HALO
        x_tile = x_ref[0] + g1 * mix[:tm]
        x_prev = xp_ref[0] + g1 * mix[lo:lo + HALO]
        x_next = xn_ref[0] + g1 * mix[tm + MIX_HALO:tm + MIX_HALO + HALO]
        o_ref[0] = x_tile
    else:
        o_ref, u_ref, a_ref, y_ref = rest
        x_tile, x_prev, x_next = x_ref[0], xp_ref[0], xn_ref[0]
    i = pl.program_id(1)
    ng, sh, sc = ng_ref[...], sh_ref[0], sc_ref[0]
    d = x_ref.shape[2]
    half = tm // 2
    slabs = FFN_CHUNK // LANES

    hp = jnp.where(i > 0, _norm_mod(x_prev, ng, sh, sc), 0.0)
    hn = jnp.where(i < pl.num_programs(1) - 1, _norm_mod(x_next, ng, sh, sc), 0.0)
    h = jnp.concatenate([hp, _norm_mod(x_tile, ng, sh, sc), hn], axis=0).astype(BF16)

    def project(buf, part, col):
        u = _dot(h, wup_ref[:, col:col + FFN_CHUNK])
        for s in range(slabs):
            u_ref[buf, part, s] = u[:, s * LANES:(s + 1) * LANES]

    def conv_gate(buf, c0):
        for s in range(slabs):
            cols = (c0 + s * LANES, d_ff + c0 + s * LANES)
            taps_w = [[cw_ref[k:k + 1, c:c + LANES] for k in range(3)] for c in cols]
            bias = [cb_ref[:, c:c + LANES] for c in cols]
            for rb in range(half // FFN_ROWS):
                res = []
                for part in range(2):
                    t = [u_ref[buf, part, s, pl.ds(HALO - 1 + k + 2 * rb * FFN_ROWS, FFN_ROWS, stride=2), :]
                         for k in range(4)]
                    w0, w1, w2 = taps_w[part]
                    res.append((t[0] * w0 + t[1] * w1 + t[2] * w2 + bias[part],
                                t[1] * w0 + t[2] * w1 + t[3] * w2 + bias[part]))
                (g_even, g_odd), (v_even, v_odd) = res
                r0 = rb * FFN_ROWS
                col = slice(c0 + s * LANES, c0 + (s + 1) * LANES)
                a_ref[r0:r0 + FFN_ROWS, col] = (g_even * jax.nn.sigmoid(g_even) * v_even).astype(BF16)
                a_ref[half + r0:half + r0 + FFN_ROWS, col] = (
                    g_odd * jax.nn.sigmoid(g_odd) * v_odd).astype(BF16)

    n_chunks = d_ff // FFN_CHUNK
    n_buf = u_ref.shape[0]

    def project_chunk(f):
        project(f % n_buf, 0, f * FFN_CHUNK)
        project(f % n_buf, 1, d_ff + f * FFN_CHUNK)

    for f in range(min(n_buf - 1, n_chunks)):
        project_chunk(f)
    for f in range(n_chunks):
        c0 = f * FFN_CHUNK
        if f + n_buf - 1 < n_chunks:
            project_chunk(f + n_buf - 1)
        conv_gate(f % n_buf, c0)
    acc = _dot(a_ref[...], wdn_ref[...])
    for s in range(d // LANES):
        y_ref[s, pl.ds(0, half, stride=2), :] = acc[:half, s * LANES:(s + 1) * LANES]
        y_ref[s, pl.ds(1, half, stride=2), :] = acc[half:, s * LANES:(s + 1) * LANES]
    ffn = jnp.concatenate([y_ref[s] for s in range(d // LANES)], axis=1)
    y = (o_ref[0] if mixer else x_ref[0]) + g2_ref[0] * ffn
    if final_norm:
        y = _rms(y, fg_ref[...])
    o_ref[0] = y


def _halo_specs(width, tm, n, rows):
    hb = tm // rows
    last = n // rows - 1
    return [
        pl.BlockSpec((1, tm, width), lambda bb, i: (bb, i, 0)),
        pl.BlockSpec((1, rows, width), lambda bb, i: (bb, jnp.maximum(i * hb - 1, 0), 0)),
        pl.BlockSpec((1, rows, width), lambda bb, i: (bb, jnp.minimum((i + 1) * hb, last), 0)),
    ]


def _ffn(x, shift, scale, gate, norm_g, w_up, conv_w, conv_b, w_down, final_g, *, layer, tm,
         final_norm, mixer=None):
    b, n, d = x.shape
    d_ff = w_down.shape[1]
    vec = lambda bb, i: (bb, 0, 0)
    const2 = lambda bb, i: (0, 0)
    this_layer = lambda bb, i: (layer, 0, 0)
    resident = dict(pipeline_mode=pl.Buffered(1))
    mix_specs, mix_args = [], []
    if mixer is not None:
        oa, ob, w_out, g1 = mixer
        for o in (oa, ob):
            mix_specs += _halo_specs(o.shape[2], tm, n, MIX_HALO)
            mix_args += [o, o, o]
        mix_specs += [pl.BlockSpec(w_out.shape, const2, **resident), pl.BlockSpec((1, 1, d), vec)]
        mix_args += [w_out, g1]
    return pl.pallas_call(
        functools.partial(_ffn_kernel, tm=tm, d_ff=d_ff, final_norm=final_norm,
                          mixer=mixer is not None),
        grid=(b, n // tm),
        in_specs=_halo_specs(d, tm, n, HALO) + [
            pl.BlockSpec((1, 1, d), vec),
            pl.BlockSpec((1, 1, d), vec),
            pl.BlockSpec((1, 1, d), vec),
            pl.BlockSpec((1, d), const2),
            pl.BlockSpec((None,) + w_up.shape[1:], this_layer, **resident),
            pl.BlockSpec(conv_w.shape, const2),
            pl.BlockSpec((1, 2 * d_ff), const2),
            pl.BlockSpec((None,) + w_down.shape[1:], this_layer, **resident),
            pl.BlockSpec((1, d), const2),
        ] + mix_specs,
        out_specs=pl.BlockSpec((1, tm, d), lambda bb, i: (bb, i, 0)),
        out_shape=jax.ShapeDtypeStruct(x.shape, F32),
        scratch_shapes=[
            pltpu.VMEM((FFN_SLOTS, 2, FFN_CHUNK // LANES, tm + 2 * HALO, LANES), F32),
            pltpu.VMEM((tm, d_ff), BF16),
            pltpu.VMEM((d // LANES, tm, LANES), F32),
        ],
        compiler_params=_cparams(("parallel", "parallel")),
        name="conv_ffn",
    )(x, x, x, shift, scale, gate, norm_g, w_up, conv_w, conv_b.reshape(1, -1), w_down, final_g,
      *mix_args)


F1_NC = 8
F2_NK = 16


def _dft_tables(n_tok, cg):
    w = GRID_W
    idx = np.arange(w, dtype=np.float64)
    a64 = 2.0 * np.pi * np.outer(idx, idx) / w
    fc, fs = np.cos(a64), np.sin(a64)
    a1 = 2.0 * np.pi * idx[None, :, None] * (w * idx[None, None, :] + idx[:, None, None]) / n_tok
    step1 = np.block([[np.cos(a1), np.sin(a1)], [-np.sin(a1), np.cos(a1)]])
    eye = np.eye(F2_NK)
    step3 = (np.kron(fc[:w // 2 + 1], eye), np.kron(fs[:w // 2 + 1], eye))
    ch = np.arange(cg, dtype=np.float64)
    ach = 2.0 * np.pi * np.outer(ch, ch) / cg
    chan = np.concatenate([np.cos(ach), -np.sin(ach)], axis=1)
    f = lambda t: jnp.asarray(t, F32)
    return f(chan), f(step1), f(step3[0]), f(step3[1])


def _f1_kernel(x_ref, sh_ref, sc_ref, g_ref, chan_ref, s1_ref, ar_ref, ai_ref, u_ref, *, nc, d):
    w = GRID_W
    cg = d // FOURIER_GROUPS
    tiles = d // LANES
    gt = cg // LANES
    x = x_ref[0].reshape(w * nc, d)
    h = _norm_mod(x, g_ref[...], sh_ref[0], sc_ref[0]).astype(BF16)
    for g in range(FOURIER_GROUPS):
        u = _dot(h[:, g * cg:(g + 1) * cg], chan_ref[...])
        for t in range(2 * gt):
            part, tt = divmod(t, gt)
            u_ref[part * tiles + g * gt + tt] = u[:, t * LANES:(t + 1) * LANES]
    for c in range(nc):
        rows = pl.ds(c, w, stride=nc)
        ur = jnp.concatenate([u_ref[s, rows, :] for s in range(tiles)], axis=1)
        ui = jnp.concatenate([u_ref[tiles + s, rows, :] for s in range(tiles)], axis=1)
        a = _dot(s1_ref[c], jnp.concatenate([ur, ui], axis=0).astype(BF16))
        ar_ref[0, c] = a[:w].astype(BF16)
        ai_ref[0, c] = a[w:].astype(BF16)


def _f2_kernel(ar_ref, ai_ref, x_ref, g1_ref, kc_ref, ks_ref, w_ref, o_ref, *, nk, d, inv_norm):
    w = GRID_W
    a_r = ar_ref[0].reshape(w * nk, d)
    a_i = ai_ref[0].reshape(w * nk, d)
    p = _dot(kc_ref[...], a_r)
    q = _dot(ks_ref[...], a_i)
    diff = p - q
    mirrored = [diff[(w - k2) * nk:(w - k2 + 1) * nk] for k2 in range(w // 2 + 1, w)]
    z = jnp.concatenate([p + q] + mirrored, axis=0)
    y = _dot((z * inv_norm).astype(BF16), w_ref[...])
    o_ref[0] = x_ref[0] + g1_ref[0] * y.reshape(w, nk, d)


def _fourier(x, shift, scale, gate, norm_g, w_out):
    b, n, d = x.shape
    w = GRID_W
    assert n == w * w
    nc, nk = F1_NC, F2_NK
    cg = d // FOURIER_GROUPS
    chan, step1, kc, ks = _dft_tables(n, cg)
    chan, step1, kc, ks = (t.astype(BF16) for t in (chan, step1, kc, ks))
    x4 = x.reshape(b, w, w, d)
    vec = lambda bb, j: (bb, 0, 0)
    const2 = lambda bb, j: (0, 0)
    resident = dict(pipeline_mode=pl.Buffered(1))
    a_r, a_i = pl.pallas_call(
        functools.partial(_f1_kernel, nc=nc, d=d),
        grid=(b, w // nc),
        in_specs=[
            pl.BlockSpec((1, w, nc, d), lambda bb, j: (bb, 0, j, 0)),
            pl.BlockSpec((1, 1, d), vec),
            pl.BlockSpec((1, 1, d), vec),
            pl.BlockSpec((1, d), const2),
            pl.BlockSpec(chan.shape, const2),
            pl.BlockSpec((nc,) + step1.shape[1:], lambda bb, j: (j, 0, 0)),
        ],
        out_specs=[pl.BlockSpec((1, nc, w, d), lambda bb, j: (bb, j, 0, 0))] * 2,
        out_shape=[jax.ShapeDtypeStruct((b, w, w, d), BF16)] * 2,
        scratch_shapes=[pltpu.VMEM((2 * d // LANES, w * nc, LANES), F32)],
        compiler_params=_cparams(("parallel", "parallel")),
        name="fourier_rows",
    )(x4, shift, scale, norm_g, chan, step1)
    out = pl.pallas_call(
        functools.partial(_f2_kernel, nk=nk, d=d, inv_norm=1.0 / float(np.sqrt(n * cg))),
        grid=(b, w // nk),
        in_specs=[
            pl.BlockSpec((1, w, nk, d), lambda bb, j: (bb, 0, j, 0)),
            pl.BlockSpec((1, w, nk, d), lambda bb, j: (bb, 0, j, 0)),
            pl.BlockSpec((1, w, nk, d), lambda bb, j: (bb, 0, j, 0)),
            pl.BlockSpec((1, 1, d), vec),
            pl.BlockSpec(kc.shape, const2, **resident),
            pl.BlockSpec(ks.shape, const2, **resident),
            pl.BlockSpec(w_out.shape, const2, **resident),
        ],
        out_specs=pl.BlockSpec((1, w, nk, d), lambda bb, j: (bb, 0, j, 0)),
        out_shape=jax.ShapeDtypeStruct((b, w, w, d), F32),
        compiler_params=_cparams(("parallel", "parallel")),
        name="fourier_cols",
    )(a_r, a_i, x4, gate, kc, ks, w_out)
    return out.reshape(b, n, d)


def _rope_tables(n):
    t = np.arange(n)
    row = (t // GRID_W).astype(np.float64)
    col = (t % GRID_W).astype(np.float64)
    pairs = HEAD_DIM_A // 4
    freqs = ROPE_THETA ** (-np.arange(pairs, dtype=np.float64) / pairs)
    ang = np.concatenate([row[:, None] * freqs, col[:, None] * freqs], axis=-1)
    cos, sin = np.cos(ang), np.sin(ang)
    return (jnp.asarray(np.concatenate([cos, cos], axis=-1), F32),
            jnp.asarray(np.concatenate([-sin, sin], axis=-1), F32))


def kernel(x, c, ctx, c_ctx, mod_w, mod_b, norm1_g, norm2_g, attn_w_in, attn_w_out, q_norm_g,
           k_norm_g, na_rpb, fourier_w_out, ffn_w_up, ffn_conv_w, ffn_conv_b, ffn_w_down, final_g):
    b, n, d = x.shape
    depth = mod_w.shape[0]
    assert depth == 2 and b + 1 <= 8

    vec8 = jnp.zeros((8, d), F32).at[:b].set(c).at[b].set(c_ctx)
    mod = _adaln(vec8, mod_w, mod_b)
    term = lambda l, k: mod[l, :b, k * d:(k + 1) * d].reshape(b, 1, d)
    cterm = lambda l, k: jnp.broadcast_to(mod[l, b, k * d:(k + 1) * d].reshape(1, 1, d), (b, 1, d))

    perm = np.concatenate([np.arange(0, HEAD_DIM_A, 2), np.arange(1, HEAD_DIM_A, 2)])
    w_in = _prep_w_in(attn_w_in, perm)
    qg = q_norm_g[0][perm].reshape(1, -1)
    kg = k_norm_g[0][perm].reshape(1, -1)
    n_ctx = ctx.shape[1]
    no_rope = (jnp.ones((n_ctx, HEAD_DIM_A), F32), jnp.zeros((n_ctx, HEAD_DIM_A), F32))
    g1row = norm1_g[0].reshape(1, d)

    (qa, qb, ka, va, kb, vb), (cka, cva, ckb, cvb) = _proj(
        x, term(0, 0), term(0, 1), ctx, cterm(0, 0), cterm(0, 1), g1row, w_in,
        _rope_tables(n), no_rope, qg, kg, tm=PROJ_TM)
    oa, (w_up, w_down, w_attn_out, w_fourier_out), na_bias = _gqa(
        qa, ka, va, cka, cva, (ffn_w_up, ffn_w_down, attn_w_out, fourier_w_out), na_rpb[0],
        tq=1024, tk=1024)
    ob = _na(qb, kb, vb, ckb, cvb, na_bias)
    x = _ffn(x, term(0, 3), term(0, 4), term(0, 5), norm2_g[0].reshape(1, d), w_up,
             ffn_conv_w[0], ffn_conv_b[0], w_down, final_g.reshape(1, d),
             layer=0, tm=FFN_TM, final_norm=False,
             mixer=(oa, ob, w_attn_out[0], term(0, 2)))

    x = _fourier(x, term(1, 0), term(1, 1), term(1, 2), norm1_g[1].reshape(1, d),
                 w_fourier_out[0])
    x = _ffn(x, term(1, 3), term(1, 4), term(1, 5), norm2_g[1].reshape(1, d), w_up,
             ffn_conv_w[1], ffn_conv_b[1], w_down, final_g.reshape(1, d),
             layer=1, tm=FFN_TM, final_norm=True)
    return x
```

```python
import functools

import numpy as np
import jax
import jax.numpy as jnp
from jax import lax
from jax.experimental import pallas as pl
from jax.experimental.pallas import tpu as pltpu

F32 = jnp.float32
BF16 = jnp.bfloat16

GRID_W = 64
HEAD_DIM_A = 128
HEADS_A = 4
KV_HEADS_A = 2
HEAD_DIM_B = 64
HEADS_B = 8
NA_ROWS = 8
NA_COLS = 16
FOURIER_GROUPS = 4
ROPE_THETA = 10000.0
EPS = 1e-6
QA_COLS = HEADS_A * HEAD_DIM_A
KA_COLS = KV_HEADS_A * HEAD_DIM_A
B_COLS = HEADS_B * HEAD_DIM_B
NEG = -1e30
LOG2E = 1.4426950408889634
LANES = 128

VMEM_LIMIT = 56 * 1024 * 1024


def _cparams(sem):
    return pltpu.CompilerParams(dimension_semantics=sem, vmem_limit_bytes=VMEM_LIMIT)


def _rms(x, g):
    ms = jnp.mean(x * x, axis=-1, keepdims=True)
    return x * lax.rsqrt(ms + EPS) * g


def _norm_mod(x, g, shift, scale):
    return _rms(x, g) * (1.0 + scale) + shift


def _dot(a, b):
    return jnp.dot(a, b, preferred_element_type=F32)


def _dot_t(a, b):
    return lax.dot_general(a, b, (((1,), (1,)), ((), ())), preferred_element_type=F32)


def _side_cast_plan(weights, n_steps, step_of):
    args = [w.reshape(-1, w.shape[-1]) for w in weights]
    specs = [pl.BlockSpec((a.shape[0] // n_steps, a.shape[1]), lambda *g: (step_of(*g), 0))
             for a in args]
    shapes = [jax.ShapeDtypeStruct(a.shape, BF16) for a in args]
    return args, specs, shapes


def _side_cast(in_refs, out_refs):
    for src, dst in zip(in_refs, out_refs):
        dst[...] = src[...].astype(dst.dtype)


W_IN_HEADS = 4


def _w_in_kernel(w_ref, p_ref, o_ref):
    q_blocks = (QA_COLS + B_COLS) // HEAD_DIM_A
    for k in range(W_IN_HEADS):
        head = pl.program_id(0) * W_IN_HEADS + k
        rotary = (head < HEADS_A) | ((head >= q_blocks) & (head < q_blocks + KV_HEADS_A))
        cols = slice(k * HEAD_DIM_A, (k + 1) * HEAD_DIM_A)
        w = w_ref[:, cols].astype(BF16)

        @pl.when(rotary)
        def _():
            o_ref[:, cols] = _dot(w, p_ref[...]).astype(BF16)

        @pl.when(jnp.logical_not(rotary))
        def _():
            o_ref[:, cols] = w


def _prep_w_in(attn_w_in, perm):
    _, d, ncol = attn_w_in.shape
    p = np.zeros((HEAD_DIM_A, HEAD_DIM_A), np.float32)
    p[perm, np.arange(HEAD_DIM_A)] = 1.0
    return pl.pallas_call(
        _w_in_kernel,
        grid=(ncol // (W_IN_HEADS * HEAD_DIM_A),),
        in_specs=[
            pl.BlockSpec((None, d, W_IN_HEADS * HEAD_DIM_A), lambda j: (0, 0, j)),
            pl.BlockSpec((HEAD_DIM_A, HEAD_DIM_A), lambda j: (0, 0)),
        ],
        out_specs=pl.BlockSpec((d, W_IN_HEADS * HEAD_DIM_A), lambda j: (0, j)),
        out_shape=jax.ShapeDtypeStruct((d, ncol), BF16),
        compiler_params=_cparams(("parallel",)),
        name="prep_w_in",
    )(attn_w_in, jnp.asarray(p).astype(BF16))


def _adaln_kernel(v_ref, w_ref, b_ref, o_ref):
    v = v_ref[...]
    s = v * jax.nn.sigmoid(v)
    w = w_ref[0]
    s_hi = s.astype(BF16)
    s_lo = (s - s_hi.astype(F32)).astype(BF16)
    w_hi = w.astype(BF16)
    w_lo = (w - w_hi.astype(F32)).astype(BF16)
    rows = s.shape[0]
    t = _dot(jnp.concatenate([s_hi, s_lo], axis=0), w_hi)
    o_ref[0] = t[:rows] + t[rows:] + _dot(s_hi, w_lo) + b_ref[0]


def _adaln(vec8, mod_w, mod_b):
    depth, d, n = mod_w.shape
    tn = 3072
    return pl.pallas_call(
        _adaln_kernel,
        grid=(depth, n // tn),
        in_specs=[
            pl.BlockSpec((8, d), lambda l, j: (0, 0)),
            pl.BlockSpec((1, d, tn), lambda l, j: (l, 0, j)),
            pl.BlockSpec((1, 1, tn), lambda l, j: (l, 0, j)),
        ],
        out_specs=pl.BlockSpec((1, 8, tn), lambda l, j: (l, 0, j)),
        out_shape=jax.ShapeDtypeStruct((depth, 8, n), F32),
        compiler_params=_cparams(("arbitrary", "arbitrary")),
        name="adaln",
    )(vec8, mod_w, mod_b.reshape(depth, 1, n))


PROJ_TM = 512


def _proj_kernel(x_ref, sh_ref, sc_ref, g_ref, w_ref, cos_ref, sin_ref, qg_ref, kg_ref,
                 *out_refs, with_q):
    h = _norm_mod(x_ref[0], g_ref[...], sh_ref[0], sc_ref[0]).astype(BF16)
    cosf = cos_ref[...]
    sinf = sin_ref[...]

    def rope_heads(p, n_heads, gain, scale, o_ref):
        for hd in range(n_heads):
            t = _rms(p[:, hd * HEAD_DIM_A:(hd + 1) * HEAD_DIM_A], gain)
            t = t * cosf + pltpu.roll(t, HEAD_DIM_A // 2, axis=1) * sinf
            o_ref[0, :, hd * HEAD_DIM_A:(hd + 1) * HEAD_DIM_A] = (t * scale).astype(BF16)

    col = QA_COLS + B_COLS
    if with_q:
        qa_ref, qb_ref, ka_ref, va_ref, kb_ref, vb_ref = out_refs
        rope_heads(_dot(h, w_ref[:, 0:QA_COLS]), HEADS_A, qg_ref[...], HEAD_DIM_A ** -0.5 * LOG2E,
                   qa_ref)
    else:
        ka_ref, va_ref, kb_ref, vb_ref = out_refs
    rope_heads(_dot(h, w_ref[:, col:col + KA_COLS]), KV_HEADS_A, kg_ref[...], 1.0, ka_ref)
    if with_q:
        qb = _dot(h, w_ref[:, QA_COLS:QA_COLS + B_COLS])
        qb_ref[0] = (qb * (HEAD_DIM_B ** -0.5 * LOG2E)).astype(BF16)
    col += KA_COLS
    va_ref[0] = _dot(h, w_ref[:, col:col + KA_COLS]).astype(BF16)
    col += KA_COLS
    kb_ref[0] = _dot(h, w_ref[:, col:col + B_COLS]).astype(BF16)
    col += B_COLS
    vb_ref[0] = _dot(h, w_ref[:, col:col + B_COLS]).astype(BF16)


def _proj(x, shift, scale, gain, w, rope, qg, kg, *, with_q, tm, name):
    b, n, d = x.shape
    ncol = w.shape[1]
    widths = ([QA_COLS, B_COLS] if with_q else []) + [KA_COLS, KA_COLS, B_COLS, B_COLS]
    vec = lambda i, bb: (bb, 0, 0)
    return pl.pallas_call(
        functools.partial(_proj_kernel, with_q=with_q),
        grid=(n // tm, b),
        in_specs=[
            pl.BlockSpec((1, tm, d), lambda i, bb: (bb, i, 0)),
            pl.BlockSpec((1, 1, d), vec),
            pl.BlockSpec((1, 1, d), vec),
            pl.BlockSpec((1, d), lambda i, bb: (0, 0)),
            pl.BlockSpec((d, ncol), lambda i, bb: (0, 0)),
            pl.BlockSpec((tm, HEAD_DIM_A), lambda i, bb: (i, 0)),
            pl.BlockSpec((tm, HEAD_DIM_A), lambda i, bb: (i, 0)),
            pl.BlockSpec((1, HEAD_DIM_A), lambda i, bb: (0, 0)),
            pl.BlockSpec((1, HEAD_DIM_A), lambda i, bb: (0, 0)),
        ],
        out_specs=[pl.BlockSpec((1, tm, wd), lambda i, bb: (bb, i, 0)) for wd in widths],
        out_shape=[jax.ShapeDtypeStruct((b, n, wd), BF16) for wd in widths],
        compiler_params=_cparams(("parallel", "parallel")),
        name=name,
    )(x, shift, scale, gain, w, *rope, qg, kg)


GQA_UNROLL = 8


def _flash_update(q, k, v, m_ref, l_ref, acc_ref):
    d = v.shape[1]
    s = _dot_t(q, k)
    m_old = m_ref[...]
    m_new = jnp.maximum(m_old, jnp.max(s, axis=-1, keepdims=True))
    alpha = jnp.exp2(m_old - m_new)
    p = jnp.exp2(s - jnp.tile(m_new, (1, s.shape[1] // m_new.shape[1])))
    o = _dot(p.astype(BF16), jnp.concatenate([v, jnp.ones_like(v)], axis=1))
    acc_ref[...] = alpha * acc_ref[...] + o[:, :d]
    l_ref[...] = alpha * l_ref[...] + o[:, d:]
    m_ref[...] = m_new


def _gqa_kernel(q_ref, k_ref, v_ref, ck_ref, cv_ref, *rest, tk, n_side):
    side_in, rp_ref = rest[:n_side], rest[n_side]
    o_ref, side_out, bias_ref = rest[n_side + 1], rest[n_side + 2:2 * n_side + 2], rest[2 * n_side + 2]
    m_ref, l_ref, acc_ref = rest[2 * n_side + 3:]
    _side_cast(side_in, side_out)
    step = (pl.program_id(0) * pl.num_programs(1) + pl.program_id(1)) * pl.num_programs(2) \
        + pl.program_id(2)
    _na_bias_block(rp_ref, bias_ref, step // NA_PAIRS, 2 * (step % NA_PAIRS))
    group = q_ref.shape[2] // HEAD_DIM_A
    m_ref[...] = jnp.full(m_ref.shape, NEG, F32)
    l_ref[...] = jnp.zeros(l_ref.shape, F32)
    acc_ref[...] = jnp.zeros(acc_ref.shape, F32)

    def update(k, v):
        for g in range(group):
            _flash_update(q_ref[0, :, g * HEAD_DIM_A:(g + 1) * HEAD_DIM_A], k, v,
                          m_ref.at[g], l_ref.at[g], acc_ref.at[g])

    def body(i, carry):
        off = pl.multiple_of(i * tk, tk)
        update(k_ref[0, pl.ds(off, tk), :], v_ref[0, pl.ds(off, tk), :])
        return carry

    lax.fori_loop(0, k_ref.shape[1] // tk, body, 0, unroll=GQA_UNROLL)
    update(ck_ref[0], cv_ref[0])
    o_ref[0] = jnp.concatenate([acc_ref[g] / l_ref[g] for g in range(group)], axis=1).astype(BF16)


def _gqa(qa, ka, va, cka, cva, side, rpb, *, tq, tk):
    b, n, _ = qa.shape
    c = cka.shape[1]
    group = HEADS_A // KV_HEADS_A
    gw = group * HEAD_DIM_A
    nq = n // tq
    n_steps = b * KV_HEADS_A * nq
    assert n_steps == NA_ROWS * NA_PAIRS, "one bias block (class, head pair) per grid step"
    step_of = lambda bb, h, i: (bb * KV_HEADS_A + h) * nq + i
    side_args, side_specs, side_shapes = _side_cast_plan(side, n_steps, step_of)
    rp = _na_bias_padded(rpb)
    bias_spec = pl.BlockSpec((1, 2 * GRID_W, NA_ROWS * GRID_W),
                             lambda *g: (step_of(*g) // NA_PAIRS, step_of(*g) % NA_PAIRS, 0))
    bias_shape = jax.ShapeDtypeStruct((NA_ROWS, HEADS_B * GRID_W, NA_ROWS * GRID_W), F32)
    out = pl.pallas_call(
        functools.partial(_gqa_kernel, tk=tk, n_side=len(side)),
        grid=(b, KV_HEADS_A, nq),
        in_specs=[
            pl.BlockSpec((1, tq, gw), lambda bb, h, i: (bb, i, h)),
            pl.BlockSpec((1, n, HEAD_DIM_A), lambda bb, h, i: (bb, 0, h)),
            pl.BlockSpec((1, n, HEAD_DIM_A), lambda bb, h, i: (bb, 0, h)),
            pl.BlockSpec((1, c, HEAD_DIM_A), lambda bb, h, i: (bb, 0, h)),
            pl.BlockSpec((1, c, HEAD_DIM_A), lambda bb, h, i: (bb, 0, h)),
        ] + side_specs + [pl.BlockSpec(rp.shape, lambda bb, h, i: (0, 0, 0))],
        out_specs=[pl.BlockSpec((1, tq, gw), lambda bb, h, i: (bb, i, h))] + side_specs
        + [bias_spec],
        out_shape=[jax.ShapeDtypeStruct((b, n, QA_COLS), BF16)] + side_shapes + [bias_shape],
        scratch_shapes=[
            pltpu.VMEM((group, tq, HEAD_DIM_A), F32),
            pltpu.VMEM((group, tq, HEAD_DIM_A), F32),
            pltpu.VMEM((group, tq, HEAD_DIM_A), F32),
        ],
        compiler_params=_cparams(("parallel", "parallel", "arbitrary")),
        name="gqa",
    )(qa, ka, va, cka, cva, *side_args, rp)
    return out[0], [o.reshape(w.shape) for o, w in zip(out[1:-1], side)], out[-1]


NA_UNROLL = 64
NA_PAIRS = HEADS_B // 2


def _na_kernel(q_ref, k_ref, v_ref, ck_ref, cv_ref, bias_ref, o_ref):
    rows = q_ref.shape[1] // GRID_W
    win = NA_ROWS * GRID_W
    lane = lax.broadcasted_iota(jnp.int32, (GRID_W, 2 * HEAD_DIM_B), 1)
    lo = lane < HEAD_DIM_B
    pw = 2 * HEAD_DIM_B

    def with_ones(v):
        return jnp.concatenate([v, jnp.ones_like(v)], axis=1)

    def body(r, carry):
        r0 = jnp.clip(r - NA_ROWS // 2, 0, rows - NA_ROWS)
        q = q_ref[0, pl.ds(pl.multiple_of(r * GRID_W, GRID_W), GRID_W), :]
        zero = jnp.zeros_like(q)
        qq = jnp.concatenate([jnp.where(lo, q, zero), jnp.where(lo, zero, q)], axis=0)
        koff = pl.multiple_of(r0 * GRID_W, GRID_W)
        s = _dot_t(qq, k_ref[0, pl.ds(koff, win), :]) + bias_ref[r - r0]
        sc = _dot_t(qq, ck_ref[0])
        m = jnp.maximum(jnp.max(s, axis=-1, keepdims=True), jnp.max(sc, axis=-1, keepdims=True))
        p = jnp.exp2(s - m).astype(BF16)
        pc = jnp.exp2(sc - m).astype(BF16)
        o = _dot(p, with_ones(v_ref[0, pl.ds(koff, win), :])) + _dot(pc, with_ones(cv_ref[0]))
        o = o[:, :pw] / o[:, pw:]
        res = jnp.where(lo, o[:GRID_W], o[GRID_W:])
        o_ref[0, pl.ds(pl.multiple_of(r * GRID_W, GRID_W), GRID_W), :] = res.astype(BF16)
        return carry

    lax.fori_loop(0, rows, body, 0, unroll=NA_UNROLL)


def _na(qb, kb, vb, ckb, cvb, bias):
    b, n, _ = qb.shape
    c = ckb.shape[1]
    pw = 2 * HEAD_DIM_B
    n_cls = bias.shape[0]
    tok = lambda h, bb: (bb, 0, h)
    return pl.pallas_call(
        _na_kernel,
        grid=(NA_PAIRS, b),
        in_specs=[
            pl.BlockSpec((1, n, pw), tok),
            pl.BlockSpec((1, n, pw), tok),
            pl.BlockSpec((1, n, pw), tok),
            pl.BlockSpec((1, c, pw), tok),
            pl.BlockSpec((1, c, pw), tok),
            pl.BlockSpec((n_cls, 2 * GRID_W, NA_ROWS * GRID_W), lambda h, bb: (0, h, 0)),
        ],
        out_specs=pl.BlockSpec((1, n, pw), tok),
        out_shape=jax.ShapeDtypeStruct((b, n, B_COLS), BF16),
        compiler_params=_cparams(("parallel", "parallel")),
        name="natten",
    )(qb, kb, vb, ckb, cvb, bias)


def _na_bias_padded(rpb):
    edge = GRID_W - NA_COLS
    return jnp.pad(rpb, ((0, 0), (0, 1), (edge, 2 * GRID_W - rpb.shape[2] - edge)))


def _na_bias_block(rp_ref, o_ref, cls, head0):
    edge = GRID_W - NA_COLS
    shape = (GRID_W, 2 * GRID_W)
    c = lax.broadcasted_iota(jnp.int32, shape, 0)
    lane = lax.broadcasted_iota(jnp.int32, shape, 1)
    lo = lane < GRID_W
    j = jnp.where(lo, lane, lane - GRID_W)
    cs = jnp.clip(c - NA_COLS // 2, 0, edge)
    valid = (j >= cs) & (j < cs + NA_COLS)

    def toeplitz(h, i, shift):
        row = rp_ref[h, pl.ds(i - cls + NA_ROWS - 1, 1), :]
        return pltpu.roll(jnp.broadcast_to(row, shape), shift, 1, stride=1, stride_axis=0)

    for k in range(2):
        for pair in range(NA_ROWS // 2):
            t = jnp.where(lo, toeplitz(head0 + k, 2 * pair, GRID_W + 1),
                          toeplitz(head0 + k, 2 * pair + 1, 1))
            o_ref[0, k * GRID_W:(k + 1) * GRID_W, pair * 2 * GRID_W:(pair + 1) * 2 * GRID_W] = (
                jnp.where(valid, t * LOG2E, NEG))


HALO = 8
MIX_HALO = 16
FFN_CHUNK = 256
FFN_TM = 512
FFN_SLOTS = 4
FFN_ROWS = 32


def _ffn_kernel(x_ref, xp_ref, xn_ref, sh_ref, sc_ref, g2_ref, ng_ref, wup_ref, cw_ref, cb_ref,
                wdn_ref, fg_ref, *rest, tm, d_ff, final_norm, mixer):
    if mixer:
        (oa_ref, oap_ref, oan_ref, ob_ref, obp_ref, obn_ref, wo_ref, g1_ref,
         o_ref, u_ref, a_ref, y_ref) = rest
        ka = oa_ref.shape[2]
        rows = lambda main, prev, nxt: jnp.concatenate([main[0], prev[0], nxt[0]], axis=0)
        mix = (_dot(rows(oa_ref, oap_ref, oan_ref), wo_ref[0:ka, :])
               + _dot(rows(ob_ref, obp_ref, obn_ref), wo_ref[ka:, :]))
        g1 = g1_ref[0]
        lo = tm + MIX_HALO - HALO
        x_tile = x_ref[0] + g1 * mix[:tm]
        x_prev = xp_ref[0] + g1 * mix[lo:lo + HALO]
        x_next = xn_ref[0] + g1 * mix[tm + MIX_HALO:tm + MIX_HALO + HALO]
        o_ref[0] = x_tile
    else:
        o_ref, u_ref, a_ref, y_ref = rest
        x_tile, x_prev, x_next = x_ref[0], xp_ref[0], xn_ref[0]
    i = pl.program_id(1)
    ng, sh, sc = ng_ref[...], sh_ref[0], sc_ref[0]
    d = x_ref.shape[2]
    half = tm // 2
    slabs = FFN_CHUNK // LANES

    hp = jnp.where(i > 0, _norm_mod(x_prev, ng, sh, sc), 0.0)
    hn = jnp.where(i < pl.num_programs(1) - 1, _norm_mod(x_next, ng, sh, sc), 0.0)
    h = jnp.concatenate([hp, _norm_mod(x_tile, ng, sh, sc), hn], axis=0).astype(BF16)

    def project(buf, part, col):
        u = _dot(h, wup_ref[:, col:col + FFN_CHUNK])
        for s in range(slabs):
            u_ref[buf, part, s] = u[:, s * LANES:(s + 1) * LANES]

    def conv_gate(buf, c0):
        for s in range(slabs):
            cols = (c0 + s * LANES, d_ff + c0 + s * LANES)
            taps_w = [[cw_ref[k:k + 1, c:c + LANES] for k in range(3)] for c in cols]
            bias = [cb_ref[:, c:c + LANES] for c in cols]
            for rb in range(half // FFN_ROWS):
                res = []
                for part in range(2):
                    t = [u_ref[buf, part, s, pl.ds(HALO - 1 + k + 2 * rb * FFN_ROWS, FFN_ROWS, stride=2), :]
                         for k in range(4)]
                    w0, w1, w2 = taps_w[part]
                    res.append((t[0] * w0 + t[1] * w1 + t[2] * w2 + bias[part],
                                t[1] * w0 + t[2] * w1 + t[3] * w2 + bias[part]))
                (g_even, g_odd), (v_even, v_odd) = res
                r0 = rb * FFN_ROWS
                col = slice(c0 + s * LANES, c0 + (s + 1) * LANES)
                a_ref[r0:r0 + FFN_ROWS, col] = (g_even * jax.nn.sigmoid(g_even) * v_even).astype(BF16)
                a_ref[half + r0:half + r0 + FFN_ROWS, col] = (
                    g_odd * jax.nn.sigmoid(g_odd) * v_odd).astype(BF16)

    n_chunks = d_ff // FFN_CHUNK
    n_buf = u_ref.shape[0]

    def project_chunk(f):
        project(f % n_buf, 0, f * FFN_CHUNK)
        project(f % n_buf, 1, d_ff + f * FFN_CHUNK)

    for f in range(min(n_buf - 1, n_chunks)):
        project_chunk(f)
    for f in range(n_chunks):
        c0 = f * FFN_CHUNK
        if f + n_buf - 1 < n_chunks:
            project_chunk(f + n_buf - 1)
        conv_gate(f % n_buf, c0)
    acc = _dot(a_ref[...], wdn_ref[...])
    for s in range(d // LANES):
        y_ref[s, pl.ds(0, half, stride=2), :] = acc[:half, s * LANES:(s + 1) * LANES]
        y_ref[s, pl.ds(1, half, stride=2), :] = acc[half:, s * LANES:(s + 1) * LANES]
    ffn = jnp.concatenate([y_ref[s] for s in range(d // LANES)], axis=1)
    y = (o_ref[0] if mixer else x_ref[0]) + g2_ref[0] * ffn
    if final_norm:
        y = _rms(y, fg_ref[...])
    o_ref[0] = y


def _halo_specs(width, tm, n, rows):
    hb = tm // rows
    last = n // rows - 1
    return [
        pl.BlockSpec((1, tm, width), lambda bb, i: (bb, i, 0)),
        pl.BlockSpec((1, rows, width), lambda bb, i: (bb, jnp.maximum(i * hb - 1, 0), 0)),
        pl.BlockSpec((1, rows, width), lambda bb, i: (bb, jnp.minimum((i + 1) * hb, last), 0)),
    ]


def _ffn(x, shift, scale, gate, norm_g, w_up, conv_w, conv_b, w_down, final_g, *, layer, tm,
         final_norm, mixer=None):
    b, n, d = x.shape
    d_ff = w_down.shape[1]
    vec = lambda bb, i: (bb, 0, 0)
    const2 = lambda bb, i: (0, 0)
    this_layer = lambda bb, i: (layer, 0, 0)
    resident = dict(pipeline_mode=pl.Buffered(1))
    mix_specs, mix_args = [], []
    if mixer is not None:
        oa, ob, w_out, g1 = mixer
        for o in (oa, ob):
            mix_specs += _halo_specs(o.shape[2], tm, n, MIX_HALO)
            mix_args += [o, o, o]
        mix_specs += [pl.BlockSpec(w_out.shape, const2, **resident), pl.BlockSpec((1, 1, d), vec)]
        mix_args += [w_out, g1]
    return pl.pallas_call(
        functools.partial(_ffn_kernel, tm=tm, d_ff=d_ff, final_norm=final_norm,
                          mixer=mixer is not None),
        grid=(b, n // tm),
        in_specs=_halo_specs(d, tm, n, HALO) + [
            pl.BlockSpec((1, 1, d), vec),
            pl.BlockSpec((1, 1, d), vec),
            pl.BlockSpec((1, 1, d), vec),
            pl.BlockSpec((1, d), const2),
            pl.BlockSpec((None,) + w_up.shape[1:], this_layer, **resident),
            pl.BlockSpec(conv_w.shape, const2),
            pl.BlockSpec((1, 2 * d_ff), const2),
            pl.BlockSpec((None,) + w_down.shape[1:], this_layer, **resident),
            pl.BlockSpec((1, d), const2),
        ] + mix_specs,
        out_specs=pl.BlockSpec((1, tm, d), lambda bb, i: (bb, i, 0)),
        out_shape=jax.ShapeDtypeStruct(x.shape, F32),
        scratch_shapes=[
            pltpu.VMEM((FFN_SLOTS, 2, FFN_CHUNK // LANES, tm + 2 * HALO, LANES), F32),
            pltpu.VMEM((tm, d_ff), BF16),
            pltpu.VMEM((d // LANES, tm, LANES), F32),
        ],
        compiler_params=_cparams(("parallel", "parallel")),
        name="conv_ffn",
    )(x, x, x, shift, scale, gate, norm_g, w_up, conv_w, conv_b.reshape(1, -1), w_down, final_g,
      *mix_args)


F1_NC = 8
F2_NK = 16


def _dft_tables(n_tok, cg):
    w = GRID_W
    idx = np.arange(w, dtype=np.float64)
    a64 = 2.0 * np.pi * np.outer(idx, idx) / w
    fc, fs = np.cos(a64), np.sin(a64)
    a1 = 2.0 * np.pi * idx[None, :, None] * (w * idx[None, None, :] + idx[:, None, None]) / n_tok
    step1 = np.block([[np.cos(a1), np.sin(a1)], [-np.sin(a1), np.cos(a1)]])
    eye = np.eye(F2_NK)
    step3 = (np.kron(fc[:w // 2 + 1], eye), np.kron(fs[:w // 2 + 1], eye))
    ch = np.arange(cg, dtype=np.float64)
    ach = 2.0 * np.pi * np.outer(ch, ch) / cg
    chan = np.concatenate([np.cos(ach), -np.sin(ach)], axis=1)
    f = lambda t: jnp.asarray(t, F32)
    return f(chan), f(step1), f(step3[0]), f(step3[1])


def _f1_kernel(x_ref, sh_ref, sc_ref, g_ref, chan_ref, s1_ref, ar_ref, ai_ref, u_ref, *, nc, d):
    w = GRID_W
    cg = d // FOURIER_GROUPS
    tiles = d // LANES
    gt = cg // LANES
    x = x_ref[0].reshape(w * nc, d)
    h = _norm_mod(x, g_ref[...], sh_ref[0], sc_ref[0]).astype(BF16)
    for g in range(FOURIER_GROUPS):
        u = _dot(h[:, g * cg:(g + 1) * cg], chan_ref[...])
        for t in range(2 * gt):
            part, tt = divmod(t, gt)
            u_ref[part * tiles + g * gt + tt] = u[:, t * LANES:(t + 1) * LANES]
    for c in range(nc):
        rows = pl.ds(c, w, stride=nc)
        ur = jnp.concatenate([u_ref[s, rows, :] for s in range(tiles)], axis=1)
        ui = jnp.concatenate([u_ref[tiles + s, rows, :] for s in range(tiles)], axis=1)
        a = _dot(s1_ref[c], jnp.concatenate([ur, ui], axis=0).astype(BF16))
        ar_ref[0, c] = a[:w].astype(BF16)
        ai_ref[0, c] = a[w:].astype(BF16)


def _f2_kernel(ar_ref, ai_ref, x_ref, g1_ref, kc_ref, ks_ref, w_ref, o_ref, *, nk, d, inv_norm):
    w = GRID_W
    a_r = ar_ref[0].reshape(w * nk, d)
    a_i = ai_ref[0].reshape(w * nk, d)
    p = _dot(kc_ref[...], a_r)
    q = _dot(ks_ref[...], a_i)
    diff = p - q
    mirrored = [diff[(w - k2) * nk:(w - k2 + 1) * nk] for k2 in range(w // 2 + 1, w)]
    z = jnp.concatenate([p + q] + mirrored, axis=0)
    y = _dot((z * inv_norm).astype(BF16), w_ref[...])
    o_ref[0] = x_ref[0] + g1_ref[0] * y.reshape(w, nk, d)


def _fourier(x, shift, scale, gate, norm_g, w_out):
    b, n, d = x.shape
    w = GRID_W
    assert n == w * w
    nc, nk = F1_NC, F2_NK
    cg = d // FOURIER_GROUPS
    chan, step1, kc, ks = _dft_tables(n, cg)
    chan, step1, kc, ks = (t.astype(BF16) for t in (chan, step1, kc, ks))
    x4 = x.reshape(b, w, w, d)
    vec = lambda bb, j: (bb, 0, 0)
    const2 = lambda bb, j: (0, 0)
    resident = dict(pipeline_mode=pl.Buffered(1))
    a_r, a_i = pl.pallas_call(
        functools.partial(_f1_kernel, nc=nc, d=d),
        grid=(w // nc, b),
        in_specs=[
            pl.BlockSpec((1, w, nc, d), lambda j, bb: (bb, 0, j, 0)),
            pl.BlockSpec((1, 1, d), lambda j, bb: (bb, 0, 0)),
            pl.BlockSpec((1, 1, d), lambda j, bb: (bb, 0, 0)),
            pl.BlockSpec((1, d), const2),
            pl.BlockSpec(chan.shape, const2),
            pl.BlockSpec((nc,) + step1.shape[1:], lambda j, bb: (j, 0, 0)),
        ],
        out_specs=[pl.BlockSpec((1, nc, w, d), lambda j, bb: (bb, j, 0, 0))] * 2,
        out_shape=[jax.ShapeDtypeStruct((b, w, w, d), BF16)] * 2,
        scratch_shapes=[pltpu.VMEM((2 * d // LANES, w * nc, LANES), F32)],
        compiler_params=_cparams(("parallel", "parallel")),
        name="fourier_rows",
    )(x4, shift, scale, norm_g, chan, step1)
    out = pl.pallas_call(
        functools.partial(_f2_kernel, nk=nk, d=d, inv_norm=1.0 / float(np.sqrt(n * cg))),
        grid=(b, w // nk),
        in_specs=[
            pl.BlockSpec((1, w, nk, d), lambda bb, j: (bb, 0, j, 0)),
            pl.BlockSpec((1, w, nk, d), lambda bb, j: (bb, 0, j, 0)),
            pl.BlockSpec((1, w, nk, d), lambda bb, j: (bb, 0, j, 0)),
            pl.BlockSpec((1, 1, d), vec),
            pl.BlockSpec(kc.shape, const2, **resident),
            pl.BlockSpec(ks.shape, const2, **resident),
            pl.BlockSpec(w_out.shape, const2, **resident),
        ],
        out_specs=pl.BlockSpec((1, w, nk, d), lambda bb, j: (bb, 0, j, 0)),
        out_shape=jax.ShapeDtypeStruct((b, w, w, d), F32),
        compiler_params=_cparams(("parallel", "parallel")),
        name="fourier_cols",
    )(a_r, a_i, x4, gate, kc, ks, w_out)
    return out.reshape(b, n, d)


def _rope_tables(n):
    t = np.arange(n)
    row = (t // GRID_W).astype(np.float64)
    col = (t % GRID_W).astype(np.float64)
    pairs = HEAD_DIM_A // 4
    freqs = ROPE_THETA ** (-np.arange(pairs, dtype=np.float64) / pairs)
    ang = np.concatenate([row[:, None] * freqs, col[:, None] * freqs], axis=-1)
    cos, sin = np.cos(ang), np.sin(ang)
    return (jnp.asarray(np.concatenate([cos, cos], axis=-1), F32),
            jnp.asarray(np.concatenate([-sin, sin], axis=-1), F32))


def kernel(x, c, ctx, c_ctx, mod_w, mod_b, norm1_g, norm2_g, attn_w_in, attn_w_out, q_norm_g,
           k_norm_g, na_rpb, fourier_w_out, ffn_w_up, ffn_conv_w, ffn_conv_b, ffn_w_down, final_g):
    b, n, d = x.shape
    depth = mod_w.shape[0]
    assert depth == 2 and b + 1 <= 8

    vec8 = jnp.zeros((8, d), F32).at[:b].set(c).at[b].set(c_ctx)
    mod = _adaln(vec8, mod_w, mod_b)
    term = lambda l, k: mod[l, :b, k * d:(k + 1) * d].reshape(b, 1, d)
    cterm = lambda l, k: jnp.broadcast_to(mod[l, b, k * d:(k + 1) * d].reshape(1, 1, d), (b, 1, d))

    perm = np.concatenate([np.arange(0, HEAD_DIM_A, 2), np.arange(1, HEAD_DIM_A, 2)])
    w_in = _prep_w_in(attn_w_in, perm)
    qg = q_norm_g[0][perm].reshape(1, -1)
    kg = k_norm_g[0][perm].reshape(1, -1)
    n_ctx = ctx.shape[1]
    no_rope = (jnp.ones((n_ctx, HEAD_DIM_A), F32), jnp.zeros((n_ctx, HEAD_DIM_A), F32))
    g1row = norm1_g[0].reshape(1, d)

    qa, qb, ka, va, kb, vb = _proj(x, term(0, 0), term(0, 1), g1row, w_in, _rope_tables(n), qg, kg,
                                   with_q=True, tm=PROJ_TM, name="proj_latent")
    cka, cva, ckb, cvb = _proj(ctx, cterm(0, 0), cterm(0, 1), g1row, w_in, no_rope, qg, kg,
                               with_q=False, tm=n_ctx, name="proj_context")
    oa, (w_up, w_down, w_attn_out, w_fourier_out), na_bias = _gqa(
        qa, ka, va, cka, cva, (ffn_w_up, ffn_w_down, attn_w_out, fourier_w_out), na_rpb[0],
        tq=1024, tk=1024)
    ob = _na(qb, kb, vb, ckb, cvb, na_bias)
    x = _ffn(x, term(0, 3), term(0, 4), term(0, 5), norm2_g[0].reshape(1, d), w_up,
             ffn_conv_w[0], ffn_conv_b[0], w_down, final_g.reshape(1, d),
             layer=0, tm=FFN_TM, final_norm=False,
             mixer=(oa, ob, w_attn_out[0], term(0, 2)))

    x = _fourier(x, term(1, 0), term(1, 1), term(1, 2), norm1_g[1].reshape(1, d),
                 w_fourier_out[0])
    x = _ffn(x, term(1, 3), term(1, 4), term(1, 5), norm2_g[1].reshape(1, d), w_up,
             ffn_conv_w[1], ffn_conv_b[1], w_down, final_g.reshape(1, d),
             layer=1, tm=FFN_TM, final_norm=True)
    return x
```

```python
import functools

import numpy as np
import jax
import jax.numpy as jnp
from jax import lax
from jax.experimental import pallas as pl
from jax.experimental.pallas import tpu as pltpu

F32 = jnp.float32
BF16 = jnp.bfloat16

GRID_W = 64
HEAD_DIM_A = 128
HEADS_A = 4
KV_HEADS_A = 2
HEAD_DIM_B = 64
HEADS_B = 8
NA_ROWS = 8
NA_COLS = 16
FOURIER_GROUPS = 4
ROPE_THETA = 10000.0
EPS = 1e-6
QA_COLS = HEADS_A * HEAD_DIM_A
KA_COLS = KV_HEADS_A * HEAD_DIM_A
B_COLS = HEADS_B * HEAD_DIM_B
NEG = -1e30
LOG2E = 1.4426950408889634
LANES = 128

VMEM_LIMIT = 56 * 1024 * 1024


def _cparams(sem):
    return pltpu.CompilerParams(dimension_semantics=sem, vmem_limit_bytes=VMEM_LIMIT)


def _rms(x, g):
    ms = jnp.mean(x * x, axis=-1, keepdims=True)
    return x * lax.rsqrt(ms + EPS) * g


def _norm_mod(x, g, shift, scale):
    return _rms(x, g) * (1.0 + scale) + shift


def _dot(a, b):
    return jnp.dot(a, b, preferred_element_type=F32)


def _dot_t(a, b):
    return lax.dot_general(a, b, (((1,), (1,)), ((), ())), preferred_element_type=F32)


def _side_cast_plan(weights, n_steps, step_of):
    args = [w.reshape(-1, w.shape[-1]) for w in weights]
    specs = [pl.BlockSpec((a.shape[0] // n_steps, a.shape[1]), lambda *g: (step_of(*g), 0))
             for a in args]
    shapes = [jax.ShapeDtypeStruct(a.shape, BF16) for a in args]
    return args, specs, shapes


def _side_cast(in_refs, out_refs):
    for src, dst in zip(in_refs, out_refs):
        dst[...] = src[...].astype(dst.dtype)


W_IN_HEADS = 4


def _w_in_kernel(w_ref, p_ref, o_ref):
    q_blocks = (QA_COLS + B_COLS) // HEAD_DIM_A
    for k in range(W_IN_HEADS):
        head = pl.program_id(0) * W_IN_HEADS + k
        rotary = (head < HEADS_A) | ((head >= q_blocks) & (head < q_blocks + KV_HEADS_A))
        cols = slice(k * HEAD_DIM_A, (k + 1) * HEAD_DIM_A)
        w = w_ref[:, cols].astype(BF16)

        @pl.when(rotary)
        def _():
            o_ref[:, cols] = _dot(w, p_ref[...]).astype(BF16)

        @pl.when(jnp.logical_not(rotary))
        def _():
            o_ref[:, cols] = w


def _prep_w_in(attn_w_in, perm):
    _, d, ncol = attn_w_in.shape
    p = np.zeros((HEAD_DIM_A, HEAD_DIM_A), np.float32)
    p[perm, np.arange(HEAD_DIM_A)] = 1.0
    return pl.pallas_call(
        _w_in_kernel,
        grid=(ncol // (W_IN_HEADS * HEAD_DIM_A),),
        in_specs=[
            pl.BlockSpec((None, d, W_IN_HEADS * HEAD_DIM_A), lambda j: (0, 0, j)),
            pl.BlockSpec((HEAD_DIM_A, HEAD_DIM_A), lambda j: (0, 0)),
        ],
        out_specs=pl.BlockSpec((d, W_IN_HEADS * HEAD_DIM_A), lambda j: (0, j)),
        out_shape=jax.ShapeDtypeStruct((d, ncol), BF16),
        compiler_params=_cparams(("parallel",)),
        name="prep_w_in",
    )(attn_w_in, jnp.asarray(p).astype(BF16))


def _adaln_kernel(v_ref, w_ref, b_ref, o_ref):
    v = v_ref[...]
    s = v * jax.nn.sigmoid(v)
    w = w_ref[0]
    s_hi = s.astype(BF16)
    s_lo = (s - s_hi.astype(F32)).astype(BF16)
    w_hi = w.astype(BF16)
    w_lo = (w - w_hi.astype(F32)).astype(BF16)
    rows = s.shape[0]
    t = _dot(jnp.concatenate([s_hi, s_lo], axis=0), w_hi)
    o_ref[0] = t[:rows] + t[rows:] + _dot(s_hi, w_lo) + b_ref[0]


def _adaln(vec8, mod_w, mod_b):
    depth, d, n = mod_w.shape
    tn = 3072
    return pl.pallas_call(
        _adaln_kernel,
        grid=(depth, n // tn),
        in_specs=[
            pl.BlockSpec((8, d), lambda l, j: (0, 0)),
            pl.BlockSpec((1, d, tn), lambda l, j: (l, 0, j)),
            pl.BlockSpec((1, 1, tn), lambda l, j: (l, 0, j)),
        ],
        out_specs=pl.BlockSpec((1, 8, tn), lambda l, j: (l, 0, j)),
        out_shape=jax.ShapeDtypeStruct((depth, 8, n), F32),
        compiler_params=_cparams(("arbitrary", "arbitrary")),
        name="adaln",
    )(vec8, mod_w, mod_b.reshape(depth, 1, n))


PROJ_TM = 512


def _proj_kernel(x_ref, sh_ref, sc_ref, g_ref, w_ref, cos_ref, sin_ref, qg_ref, kg_ref,
                 *out_refs, with_q):
    h = _norm_mod(x_ref[0], g_ref[...], sh_ref[0], sc_ref[0]).astype(BF16)
    cosf = cos_ref[...]
    sinf = sin_ref[...]

    def rope_heads(p, n_heads, gain, scale, o_ref):
        for hd in range(n_heads):
            t = _rms(p[:, hd * HEAD_DIM_A:(hd + 1) * HEAD_DIM_A], gain)
            t = t * cosf + pltpu.roll(t, HEAD_DIM_A // 2, axis=1) * sinf
            o_ref[0, :, hd * HEAD_DIM_A:(hd + 1) * HEAD_DIM_A] = (t * scale).astype(BF16)

    col = QA_COLS + B_COLS
    if with_q:
        qa_ref, qb_ref, ka_ref, va_ref, kb_ref, vb_ref = out_refs
        rope_heads(_dot(h, w_ref[:, 0:QA_COLS]), HEADS_A, qg_ref[...], HEAD_DIM_A ** -0.5 * LOG2E,
                   qa_ref)
    else:
        ka_ref, va_ref, kb_ref, vb_ref = out_refs
    rope_heads(_dot(h, w_ref[:, col:col + KA_COLS]), KV_HEADS_A, kg_ref[...], 1.0, ka_ref)
    if with_q:
        qb = _dot(h, w_ref[:, QA_COLS:QA_COLS + B_COLS])
        qb_ref[0] = (qb * (HEAD_DIM_B ** -0.5 * LOG2E)).astype(BF16)
    col += KA_COLS
    va_ref[0] = _dot(h, w_ref[:, col:col + KA_COLS]).astype(BF16)
    col += KA_COLS
    kb_ref[0] = _dot(h, w_ref[:, col:col + B_COLS]).astype(BF16)
    col += B_COLS
    vb_ref[0] = _dot(h, w_ref[:, col:col + B_COLS]).astype(BF16)


def _proj(x, shift, scale, gain, w, rope, qg, kg, *, with_q, tm, name):
    b, n, d = x.shape
    ncol = w.shape[1]
    widths = ([QA_COLS, B_COLS] if with_q else []) + [KA_COLS, KA_COLS, B_COLS, B_COLS]
    vec = lambda i, bb: (bb, 0, 0)
    return pl.pallas_call(
        functools.partial(_proj_kernel, with_q=with_q),
        grid=(n // tm, b),
        in_specs=[
            pl.BlockSpec((1, tm, d), lambda i, bb: (bb, i, 0)),
            pl.BlockSpec((1, 1, d), vec),
            pl.BlockSpec((1, 1, d), vec),
            pl.BlockSpec((1, d), lambda i, bb: (0, 0)),
            pl.BlockSpec((d, ncol), lambda i, bb: (0, 0)),
            pl.BlockSpec((tm, HEAD_DIM_A), lambda i, bb: (i, 0)),
            pl.BlockSpec((tm, HEAD_DIM_A), lambda i, bb: (i, 0)),
            pl.BlockSpec((1, HEAD_DIM_A), lambda i, bb: (0, 0)),
            pl.BlockSpec((1, HEAD_DIM_A), lambda i, bb: (0, 0)),
        ],
        out_specs=[pl.BlockSpec((1, tm, wd), lambda i, bb: (bb, i, 0)) for wd in widths],
        out_shape=[jax.ShapeDtypeStruct((b, n, wd), BF16) for wd in widths],
        compiler_params=_cparams(("parallel", "parallel")),
        name=name,
    )(x, shift, scale, gain, w, *rope, qg, kg)


GQA_UNROLL = 8


def _flash_update(q, k, v, m_ref, l_ref, acc_ref):
    d = v.shape[1]
    s = _dot_t(q, k)
    m_old = m_ref[...]
    m_new = jnp.maximum(m_old, jnp.max(s, axis=-1, keepdims=True))
    alpha = jnp.exp2(m_old - m_new)
    p = jnp.exp2(s - jnp.tile(m_new, (1, s.shape[1] // m_new.shape[1])))
    o = _dot(p.astype(BF16), jnp.concatenate([v, jnp.ones_like(v)], axis=1))
    acc_ref[...] = alpha * acc_ref[...] + o[:, :d]
    l_ref[...] = alpha * l_ref[...] + o[:, d:]
    m_ref[...] = m_new


def _gqa_kernel(q_ref, k_ref, v_ref, ck_ref, cv_ref, *rest, tk, n_side):
    side_in, rp_ref = rest[:n_side], rest[n_side]
    o_ref, side_out, bias_ref = rest[n_side + 1], rest[n_side + 2:2 * n_side + 2], rest[2 * n_side + 2]
    m_ref, l_ref, acc_ref = rest[2 * n_side + 3:]
    _side_cast(side_in, side_out)
    step = (pl.program_id(0) * pl.num_programs(1) + pl.program_id(1)) * pl.num_programs(2) \
        + pl.program_id(2)
    _na_bias_block(rp_ref, bias_ref, step // NA_PAIRS, 2 * (step % NA_PAIRS))
    group = q_ref.shape[2] // HEAD_DIM_A
    m_ref[...] = jnp.full(m_ref.shape, NEG, F32)
    l_ref[...] = jnp.zeros(l_ref.shape, F32)
    acc_ref[...] = jnp.zeros(acc_ref.shape, F32)

    def update(k, v):
        for g in range(group):
            _flash_update(q_ref[0, :, g * HEAD_DIM_A:(g + 1) * HEAD_DIM_A], k, v,
                          m_ref.at[g], l_ref.at[g], acc_ref.at[g])

    def body(i, carry):
        off = pl.multiple_of(i * tk, tk)
        update(k_ref[0, pl.ds(off, tk), :], v_ref[0, pl.ds(off, tk), :])
        return carry

    lax.fori_loop(0, k_ref.shape[1] // tk, body, 0, unroll=GQA_UNROLL)
    update(ck_ref[0], cv_ref[0])
    o_ref[0] = jnp.concatenate([acc_ref[g] / l_ref[g] for g in range(group)], axis=1).astype(BF16)


def _gqa(qa, ka, va, cka, cva, side, rpb, *, tq, tk):
    b, n, _ = qa.shape
    c = cka.shape[1]
    group = HEADS_A // KV_HEADS_A
    gw = group * HEAD_DIM_A
    nq = n // tq
    n_steps = b * KV_HEADS_A * nq
    assert n_steps == NA_ROWS * NA_PAIRS, "one bias block (class, head pair) per grid step"
    step_of = lambda bb, h, i: (bb * KV_HEADS_A + h) * nq + i
    side_args, side_specs, side_shapes = _side_cast_plan(side, n_steps, step_of)
    rp = _na_bias_padded(rpb)
    bias_spec = pl.BlockSpec((1, 2 * GRID_W, NA_ROWS * GRID_W),
                             lambda *g: (step_of(*g) // NA_PAIRS, step_of(*g) % NA_PAIRS, 0))
    bias_shape = jax.ShapeDtypeStruct((NA_ROWS, HEADS_B * GRID_W, NA_ROWS * GRID_W), F32)
    out = pl.pallas_call(
        functools.partial(_gqa_kernel, tk=tk, n_side=len(side)),
        grid=(b, KV_HEADS_A, nq),
        in_specs=[
            pl.BlockSpec((1, tq, gw), lambda bb, h, i: (bb, i, h)),
            pl.BlockSpec((1, n, HEAD_DIM_A), lambda bb, h, i: (bb, 0, h)),
            pl.BlockSpec((1, n, HEAD_DIM_A), lambda bb, h, i: (bb, 0, h)),
            pl.BlockSpec((1, c, HEAD_DIM_A), lambda bb, h, i: (bb, 0, h)),
            pl.BlockSpec((1, c, HEAD_DIM_A), lambda bb, h, i: (bb, 0, h)),
        ] + side_specs + [pl.BlockSpec(rp.shape, lambda bb, h, i: (0, 0, 0))],
        out_specs=[pl.BlockSpec((1, tq, gw), lambda bb, h, i: (bb, i, h))] + side_specs
        + [bias_spec],
        out_shape=[jax.ShapeDtypeStruct((b, n, QA_COLS), BF16)] + side_shapes + [bias_shape],
        scratch_shapes=[
            pltpu.VMEM((group, tq, HEAD_DIM_A), F32),
            pltpu.VMEM((group, tq, HEAD_DIM_A), F32),
            pltpu.VMEM((group, tq, HEAD_DIM_A), F32),
        ],
        compiler_params=_cparams(("parallel", "parallel", "arbitrary")),
        name="gqa",
    )(qa, ka, va, cka, cva, *side_args, rp)
    return out[0], [o.reshape(w.shape) for o, w in zip(out[1:-1], side)], out[-1]


NA_UNROLL = 64
NA_PAIRS = HEADS_B // 2


def _na_kernel(q_ref, k_ref, v_ref, ck_ref, cv_ref, bias_ref, o_ref):
    rows = q_ref.shape[1] // GRID_W
    win = NA_ROWS * GRID_W
    lane = lax.broadcasted_iota(jnp.int32, (GRID_W, 2 * HEAD_DIM_B), 1)
    lo = lane < HEAD_DIM_B
    pw = 2 * HEAD_DIM_B

    def with_ones(v):
        return jnp.concatenate([v, jnp.ones_like(v)], axis=1)

    def body(r, carry):
        r0 = jnp.clip(r - NA_ROWS // 2, 0, rows - NA_ROWS)
        q = q_ref[0, pl.ds(pl.multiple_of(r * GRID_W, GRID_W), GRID_W), :]
        zero = jnp.zeros_like(q)
        qq = jnp.concatenate([jnp.where(lo, q, zero), jnp.where(lo, zero, q)], axis=0)
        koff = pl.multiple_of(r0 * GRID_W, GRID_W)
        s = _dot_t(qq, k_ref[0, pl.ds(koff, win), :]) + bias_ref[r - r0]
        sc = _dot_t(qq, ck_ref[0])
        m = jnp.maximum(jnp.max(s, axis=-1, keepdims=True), jnp.max(sc, axis=-1, keepdims=True))
        p = jnp.exp2(s - m).astype(BF16)
        pc = jnp.exp2(sc - m).astype(BF16)
        o = _dot(p, with_ones(v_ref[0, pl.ds(koff, win), :])) + _dot(pc, with_ones(cv_ref[0]))
        o = o[:, :pw] / o[:, pw:]
        res = jnp.where(lo, o[:GRID_W], o[GRID_W:])
        o_ref[0, pl.ds(pl.multiple_of(r * GRID_W, GRID_W), GRID_W), :] = res.astype(BF16)
        return carry

    lax.fori_loop(0, rows, body, 0, unroll=NA_UNROLL)


def _na(qb, kb, vb, ckb, cvb, bias):
    b, n, _ = qb.shape
    c = ckb.shape[1]
    pw = 2 * HEAD_DIM_B
    n_cls = bias.shape[0]
    tok = lambda h, bb: (bb, 0, h)
    return pl.pallas_call(
        _na_kernel,
        grid=(NA_PAIRS, b),
        in_specs=[
            pl.BlockSpec((1, n, pw), tok),
            pl.BlockSpec((1, n, pw), tok),
            pl.BlockSpec((1, n, pw), tok),
            pl.BlockSpec((1, c, pw), tok),
            pl.BlockSpec((1, c, pw), tok),
            pl.BlockSpec((n_cls, 2 * GRID_W, NA_ROWS * GRID_W), lambda h, bb: (0, h, 0)),
        ],
        out_specs=pl.BlockSpec((1, n, pw), tok),
        out_shape=jax.ShapeDtypeStruct((b, n, B_COLS), BF16),
        compiler_params=_cparams(("parallel", "parallel")),
        name="natten",
    )(qb, kb, vb, ckb, cvb, bias)


def _na_bias_padded(rpb):
    edge = GRID_W - NA_COLS
    return jnp.pad(rpb, ((0, 0), (0, 1), (edge, 2 * GRID_W - rpb.shape[2] - edge)))


def _na_bias_block(rp_ref, o_ref, cls, head0):
    edge = GRID_W - NA_COLS
    shape = (GRID_W, 2 * GRID_W)
    c = lax.broadcasted_iota(jnp.int32, shape, 0)
    lane = lax.broadcasted_iota(jnp.int32, shape, 1)
    lo = lane < GRID_W
    j = jnp.where(lo, lane, lane - GRID_W)
    cs = jnp.clip(c - NA_COLS // 2, 0, edge)
    valid = (j >= cs) & (j < cs + NA_COLS)

    def toeplitz(h, i, shift):
        row = rp_ref[h, pl.ds(i - cls + NA_ROWS - 1, 1), :]
        return pltpu.roll(jnp.broadcast_to(row, shape), shift, 1, stride=1, stride_axis=0)

    for k in range(2):
        for pair in range(NA_ROWS // 2):
            t = jnp.where(lo, toeplitz(head0 + k, 2 * pair, GRID_W + 1),
                          toeplitz(head0 + k, 2 * pair + 1, 1))
            o_ref[0, k * GRID_W:(k + 1) * GRID_W, pair * 2 * GRID_W:(pair + 1) * 2 * GRID_W] = (
                jnp.where(valid, t * LOG2E, NEG))


HALO = 8
MIX_HALO = 16
FFN_CHUNK = 256
FFN_TM = 512
FFN_SLOTS = 4
FFN_ROWS = 32


def _ffn_kernel(x_ref, xp_ref, xn_ref, sh_ref, sc_ref, g2_ref, ng_ref, wup_ref, cw_ref, cb_ref,
                wdn_ref, fg_ref, *rest, tm, d_ff, final_norm, mixer, resid):
    if mixer:
        (oa_ref, oap_ref, oan_ref, ob_ref, obp_ref, obn_ref, wo_ref, g1_ref,
         o_ref, u_ref, a_ref, y_ref) = rest
        ka = oa_ref.shape[2]
        rows = lambda main, prev, nxt: jnp.concatenate([main[0], prev[0], nxt[0]], axis=0)
        mix = (_dot(rows(oa_ref, oap_ref, oan_ref), wo_ref[0:ka, :])
               + _dot(rows(ob_ref, obp_ref, obn_ref), wo_ref[ka:, :]))
        g1 = g1_ref[0]
        lo = tm + MIX_HALO - HALO
        x_tile = x_ref[0] + g1 * mix[:tm]
        x_prev = xp_ref[0] + g1 * mix[lo:lo + HALO]
        x_next = xn_ref[0] + g1 * mix[tm + MIX_HALO:tm + MIX_HALO + HALO]
        o_ref[0] = x_tile
    elif resid:
        ym_ref, ymp_ref, ymn_ref, g1_ref, o_ref, u_ref, a_ref, y_ref = rest
        g1 = g1_ref[0]
        x_tile = x_ref[0] + g1 * ym_ref[0].astype(F32)
        x_prev = xp_ref[0] + g1 * ymp_ref[0].astype(F32)[MIX_HALO - HALO:]
        x_next = xn_ref[0] + g1 * ymn_ref[0].astype(F32)[:HALO]
        o_ref[0] = x_tile
    else:
        o_ref, u_ref, a_ref, y_ref = rest
        x_tile, x_prev, x_next = x_ref[0], xp_ref[0], xn_ref[0]
    i = pl.program_id(1)
    ng, sh, sc = ng_ref[...], sh_ref[0], sc_ref[0]
    d = x_ref.shape[2]
    half = tm // 2
    slabs = FFN_CHUNK // LANES

    hp = jnp.where(i > 0, _norm_mod(x_prev, ng, sh, sc), 0.0)
    hn = jnp.where(i < pl.num_programs(1) - 1, _norm_mod(x_next, ng, sh, sc), 0.0)
    h = jnp.concatenate([hp, _norm_mod(x_tile, ng, sh, sc), hn], axis=0).astype(BF16)

    def project(buf, part, col):
        u = _dot(h, wup_ref[:, col:col + FFN_CHUNK])
        for s in range(slabs):
            u_ref[buf, part, s] = u[:, s * LANES:(s + 1) * LANES]

    def conv_gate(buf, c0):
        for s in range(slabs):
            cols = (c0 + s * LANES, d_ff + c0 + s * LANES)
            taps_w = [[cw_ref[k:k + 1, c:c + LANES] for k in range(3)] for c in cols]
            bias = [cb_ref[:, c:c + LANES] for c in cols]
            for rb in range(half // FFN_ROWS):
                res = []
                for part in range(2):
                    t = [u_ref[buf, part, s, pl.ds(HALO - 1 + k + 2 * rb * FFN_ROWS, FFN_ROWS, stride=2), :]
                         for k in range(4)]
                    w0, w1, w2 = taps_w[part]
                    res.append((t[0] * w0 + t[1] * w1 + t[2] * w2 + bias[part],
                                t[1] * w0 + t[2] * w1 + t[3] * w2 + bias[part]))
                (g_even, g_odd), (v_even, v_odd) = res
                r0 = rb * FFN_ROWS
                col = slice(c0 + s * LANES, c0 + (s + 1) * LANES)
                a_ref[r0:r0 + FFN_ROWS, col] = (g_even * jax.nn.sigmoid(g_even) * v_even).astype(BF16)
                a_ref[half + r0:half + r0 + FFN_ROWS, col] = (
                    g_odd * jax.nn.sigmoid(g_odd) * v_odd).astype(BF16)

    n_chunks = d_ff // FFN_CHUNK
    n_buf = u_ref.shape[0]

    def project_chunk(f):
        project(f % n_buf, 0, f * FFN_CHUNK)
        project(f % n_buf, 1, d_ff + f * FFN_CHUNK)

    for f in range(min(n_buf - 1, n_chunks)):
        project_chunk(f)
    for f in range(n_chunks):
        c0 = f * FFN_CHUNK
        if f + n_buf - 1 < n_chunks:
            project_chunk(f + n_buf - 1)
        conv_gate(f % n_buf, c0)
    acc = _dot(a_ref[...], wdn_ref[...])
    for s in range(d // LANES):
        y_ref[s, pl.ds(0, half, stride=2), :] = acc[:half, s * LANES:(s + 1) * LANES]
        y_ref[s, pl.ds(1, half, stride=2), :] = acc[half:, s * LANES:(s + 1) * LANES]
    ffn = jnp.concatenate([y_ref[s] for s in range(d // LANES)], axis=1)
    y = (o_ref[0] if (mixer or resid) else x_ref[0]) + g2_ref[0] * ffn
    if final_norm:
        y = _rms(y, fg_ref[...])
    o_ref[0] = y


def _halo_specs(width, tm, n, rows):
    hb = tm // rows
    last = n // rows - 1
    return [
        pl.BlockSpec((1, tm, width), lambda bb, i: (bb, i, 0)),
        pl.BlockSpec((1, rows, width), lambda bb, i: (bb, jnp.maximum(i * hb - 1, 0), 0)),
        pl.BlockSpec((1, rows, width), lambda bb, i: (bb, jnp.minimum((i + 1) * hb, last), 0)),
    ]


def _ffn(x, shift, scale, gate, norm_g, w_up, conv_w, conv_b, w_down, final_g, *, layer, tm,
         final_norm, mixer=None, resid=None):
    b, n, d = x.shape
    d_ff = w_down.shape[1]
    vec = lambda bb, i: (bb, 0, 0)
    const2 = lambda bb, i: (0, 0)
    this_layer = lambda bb, i: (layer, 0, 0)
    resident = dict(pipeline_mode=pl.Buffered(1))
    mix_specs, mix_args = [], []
    if mixer is not None:
        oa, ob, w_out, g1 = mixer
        for o in (oa, ob):
            mix_specs += _halo_specs(o.shape[2], tm, n, MIX_HALO)
            mix_args += [o, o, o]
        mix_specs += [pl.BlockSpec(w_out.shape, const2, **resident), pl.BlockSpec((1, 1, d), vec)]
        mix_args += [w_out, g1]
    if resid is not None:
        ym, g1 = resid
        mix_specs += _halo_specs(d, tm, n, MIX_HALO) + [pl.BlockSpec((1, 1, d), vec)]
        mix_args += [ym, ym, ym, g1]
    return pl.pallas_call(
        functools.partial(_ffn_kernel, tm=tm, d_ff=d_ff, final_norm=final_norm,
                          mixer=mixer is not None, resid=resid is not None),
        grid=(b, n // tm),
        in_specs=_halo_specs(d, tm, n, HALO) + [
            pl.BlockSpec((1, 1, d), vec),
            pl.BlockSpec((1, 1, d), vec),
            pl.BlockSpec((1, 1, d), vec),
            pl.BlockSpec((1, d), const2),
            pl.BlockSpec((None,) + w_up.shape[1:], this_layer, **resident),
            pl.BlockSpec(conv_w.shape, const2),
            pl.BlockSpec((1, 2 * d_ff), const2),
            pl.BlockSpec((None,) + w_down.shape[1:], this_layer, **resident),
            pl.BlockSpec((1, d), const2),
        ] + mix_specs,
        out_specs=pl.BlockSpec((1, tm, d), lambda bb, i: (bb, i, 0)),
        out_shape=jax.ShapeDtypeStruct(x.shape, F32),
        scratch_shapes=[
            pltpu.VMEM((FFN_SLOTS, 2, FFN_CHUNK // LANES, tm + 2 * HALO, LANES), F32),
            pltpu.VMEM((tm, d_ff), BF16),
            pltpu.VMEM((d // LANES, tm, LANES), F32),
        ],
        compiler_params=_cparams(("parallel", "parallel")),
        name="conv_ffn",
    )(x, x, x, shift, scale, gate, norm_g, w_up, conv_w, conv_b.reshape(1, -1), w_down, final_g,
      *mix_args)


F1_NC = 8
F2_NK = 16


def _dft_tables(n_tok, cg):
    w = GRID_W
    idx = np.arange(w, dtype=np.float64)
    a64 = 2.0 * np.pi * np.outer(idx, idx) / w
    fc, fs = np.cos(a64), np.sin(a64)
    a1 = 2.0 * np.pi * idx[None, :, None] * (w * idx[None, None, :] + idx[:, None, None]) / n_tok
    step1 = np.block([[np.cos(a1), np.sin(a1)], [-np.sin(a1), np.cos(a1)]])
    eye = np.eye(F2_NK)
    step3 = (np.kron(fc[:w // 2 + 1], eye), np.kron(fs[:w // 2 + 1], eye))
    ch = np.arange(cg, dtype=np.float64)
    ach = 2.0 * np.pi * np.outer(ch, ch) / cg
    chan = np.concatenate([np.cos(ach), -np.sin(ach)], axis=1)
    f = lambda t: jnp.asarray(t, F32)
    return f(chan), f(step1), f(step3[0]), f(step3[1])


def _f1_kernel(x_ref, sh_ref, sc_ref, g_ref, chan_ref, s1_ref, ar_ref, ai_ref, u_ref, *, nc, d):
    w = GRID_W
    cg = d // FOURIER_GROUPS
    tiles = d // LANES
    gt = cg // LANES
    x = x_ref[0].reshape(w * nc, d)
    h = _norm_mod(x, g_ref[...], sh_ref[0], sc_ref[0]).astype(BF16)
    for g in range(FOURIER_GROUPS):
        u = _dot(h[:, g * cg:(g + 1) * cg], chan_ref[...])
        for t in range(2 * gt):
            part, tt = divmod(t, gt)
            u_ref[part * tiles + g * gt + tt] = u[:, t * LANES:(t + 1) * LANES]
    for c in range(nc):
        rows = pl.ds(c, w, stride=nc)
        ur = jnp.concatenate([u_ref[s, rows, :] for s in range(tiles)], axis=1)
        ui = jnp.concatenate([u_ref[tiles + s, rows, :] for s in range(tiles)], axis=1)
        a = _dot(s1_ref[c], jnp.concatenate([ur, ui], axis=0).astype(BF16))
        ar_ref[0, c] = a[:w].astype(BF16)
        ai_ref[0, c] = a[w:].astype(BF16)


def _f2_kernel(ar_ref, ai_ref, kc_ref, ks_ref, w_ref, o_ref, *, nk, d, inv_norm):
    w = GRID_W
    a_r = ar_ref[0].reshape(w * nk, d)
    a_i = ai_ref[0].reshape(w * nk, d)
    p = _dot(kc_ref[...], a_r)
    q = _dot(ks_ref[...], a_i)
    diff = p - q
    mirrored = [diff[(w - k2) * nk:(w - k2 + 1) * nk] for k2 in range(w // 2 + 1, w)]
    z = jnp.concatenate([p + q] + mirrored, axis=0)
    y = _dot((z * inv_norm).astype(BF16), w_ref[...])
    o_ref[0] = y.astype(BF16).reshape(w, nk, d)


def _fourier(x, shift, scale, norm_g, w_out):
    b, n, d = x.shape
    w = GRID_W
    assert n == w * w
    nc, nk = F1_NC, F2_NK
    cg = d // FOURIER_GROUPS
    chan, step1, kc, ks = _dft_tables(n, cg)
    chan, step1, kc, ks = (t.astype(BF16) for t in (chan, step1, kc, ks))
    x4 = x.reshape(b, w, w, d)
    vec = lambda bb, j: (bb, 0, 0)
    const2 = lambda bb, j: (0, 0)
    resident = dict(pipeline_mode=pl.Buffered(1))
    a_r, a_i = pl.pallas_call(
        functools.partial(_f1_kernel, nc=nc, d=d),
        grid=(w // nc, b),
        in_specs=[
            pl.BlockSpec((1, w, nc, d), lambda j, bb: (bb, 0, j, 0)),
            pl.BlockSpec((1, 1, d), lambda j, bb: (bb, 0, 0)),
            pl.BlockSpec((1, 1, d), lambda j, bb: (bb, 0, 0)),
            pl.BlockSpec((1, d), const2),
            pl.BlockSpec(chan.shape, const2),
            pl.BlockSpec((nc,) + step1.shape[1:], lambda j, bb: (j, 0, 0)),
        ],
        out_specs=[pl.BlockSpec((1, nc, w, d), lambda j, bb: (bb, j, 0, 0))] * 2,
        out_shape=[jax.ShapeDtypeStruct((b, w, w, d), BF16)] * 2,
        scratch_shapes=[pltpu.VMEM((2 * d // LANES, w * nc, LANES), F32)],
        compiler_params=_cparams(("parallel", "parallel")),
        name="fourier_rows",
    )(x4, shift, scale, norm_g, chan, step1)
    out = pl.pallas_call(
        functools.partial(_f2_kernel, nk=nk, d=d, inv_norm=1.0 / float(np.sqrt(n * cg))),
        grid=(b, w // nk),
        in_specs=[
            pl.BlockSpec((1, w, nk, d), lambda bb, j: (bb, 0, j, 0)),
            pl.BlockSpec((1, w, nk, d), lambda bb, j: (bb, 0, j, 0)),
            pl.BlockSpec(kc.shape, const2, **resident),
            pl.BlockSpec(ks.shape, const2, **resident),
            pl.BlockSpec(w_out.shape, const2, **resident),
        ],
        out_specs=pl.BlockSpec((1, w, nk, d), lambda bb, j: (bb, 0, j, 0)),
        out_shape=jax.ShapeDtypeStruct((b, w, w, d), BF16),
        compiler_params=_cparams(("parallel", "parallel")),
        name="fourier_cols",
    )(a_r, a_i, kc, ks, w_out)
    return out.reshape(b, n, d)


def _rope_tables(n):
    t = np.arange(n)
    row = (t // GRID_W).astype(np.float64)
    col = (t % GRID_W).astype(np.float64)
    pairs = HEAD_DIM_A // 4
    freqs = ROPE_THETA ** (-np.arange(pairs, dtype=np.float64) / pairs)
    ang = np.concatenate([row[:, None] * freqs, col[:, None] * freqs], axis=-1)
    cos, sin = np.cos(ang), np.sin(ang)
    return (jnp.asarray(np.concatenate([cos, cos], axis=-1), F32),
            jnp.asarray(np.concatenate([-sin, sin], axis=-1), F32))


def kernel(x, c, ctx, c_ctx, mod_w, mod_b, norm1_g, norm2_g, attn_w_in, attn_w_out, q_norm_g,
           k_norm_g, na_rpb, fourier_w_out, ffn_w_up, ffn_conv_w, ffn_conv_b, ffn_w_down, final_g):
    b, n, d = x.shape
    depth = mod_w.shape[0]
    assert depth == 2 and b + 1 <= 8

    vec8 = jnp.zeros((8, d), F32).at[:b].set(c).at[b].set(c_ctx)
    mod = _adaln(vec8, mod_w, mod_b)
    term = lambda l, k: mod[l, :b, k * d:(k + 1) * d].reshape(b, 1, d)
    cterm = lambda l, k: jnp.broadcast_to(mod[l, b, k * d:(k + 1) * d].reshape(1, 1, d), (b, 1, d))

    perm = np.concatenate([np.arange(0, HEAD_DIM_A, 2), np.arange(1, HEAD_DIM_A, 2)])
    w_in = _prep_w_in(attn_w_in, perm)
    qg = q_norm_g[0][perm].reshape(1, -1)
    kg = k_norm_g[0][perm].reshape(1, -1)
    n_ctx = ctx.shape[1]
    no_rope = (jnp.ones((n_ctx, HEAD_DIM_A), F32), jnp.zeros((n_ctx, HEAD_DIM_A), F32))
    g1row = norm1_g[0].reshape(1, d)

    qa, qb, ka, va, kb, vb = _proj(x, term(0, 0), term(0, 1), g1row, w_in, _rope_tables(n), qg, kg,
                                   with_q=True, tm=PROJ_TM, name="proj_latent")
    cka, cva, ckb, cvb = _proj(ctx, cterm(0, 0), cterm(0, 1), g1row, w_in, no_rope, qg, kg,
                               with_q=False, tm=n_ctx, name="proj_context")
    oa, (w_up, w_down, w_attn_out, w_fourier_out), na_bias = _gqa(
        qa, ka, va, cka, cva, (ffn_w_up, ffn_w_down, attn_w_out, fourier_w_out), na_rpb[0],
        tq=1024, tk=1024)
    ob = _na(qb, kb, vb, ckb, cvb, na_bias)
    x = _ffn(x, term(0, 3), term(0, 4), term(0, 5), norm2_g[0].reshape(1, d), w_up,
             ffn_conv_w[0], ffn_conv_b[0], w_down, final_g.reshape(1, d),
             layer=0, tm=FFN_TM, final_norm=False,
             mixer=(oa, ob, w_attn_out[0], term(0, 2)))

    y_mix = _fourier(x, term(1, 0), term(1, 1), norm1_g[1].reshape(1, d), w_fourier_out[0])
    x = _ffn(x, term(1, 3), term(1, 4), term(1, 5), norm2_g[1].reshape(1, d), w_up,
             ffn_conv_w[1], ffn_conv_b[1], w_down, final_g.reshape(1, d),
             layer=1, tm=FFN_TM, final_norm=True, resid=(y_mix, term(1, 2)))
    return x
```
